```python
import math
import jax, jax.numpy as jnp
from jax import lax
import numpy as np

D_MODEL = 1024
BATCH = 4
SEQ = 8192
DEPTH = 2

N_MIXERS = 2
EPS = 1e-6
ATTN_HEADS = 16
ATTN_KV_HEADS = 4
ATTN_HEAD_DIM = D_MODEL // ATTN_HEADS
ATTN_GROUP = ATTN_HEADS // ATTN_KV_HEADS
WINDOW = 128
BLOCK = 128
ROPE_THETA = 500000.0
ROPE_DIM = ATTN_HEAD_DIM // 4
Q_W = ATTN_HEADS * ATTN_HEAD_DIM
KV_W = ATTN_KV_HEADS * ATTN_HEAD_DIM
MEM_LEN = 256
MEM_HEADS = 4
MEM_HEAD_DIM = 128
MEM_W = MEM_HEADS * MEM_HEAD_DIM
LRU_WIDTH = D_MODEL
LRU_BLOCKS = 8
LRU_BLOCK_DIM = LRU_WIDTH // LRU_BLOCKS
LRU_C = 8.0
CONV_WIDTH = 4
CONV_LEFT = (CONV_WIDTH - 1) // 2
ATTN_IN_W = Q_W + 2 * KV_W + MEM_W
LRU_IN_W = 2 * LRU_WIDTH + MEM_W
MIX_OUT_W = Q_W + MEM_W
D_FF = 4 * D_MODEL
NEG = -1e30

kernel_name = "hybrid_window_gqa_rglru_memxattn_encoder"


def rms_norm(x, g):
    xf = x.astype(jnp.float32)
    y = xf * lax.rsqrt(jnp.mean(xf * xf, axis=-1, keepdims=True) + EPS) * g.astype(jnp.float32)
    return y.astype(x.dtype)


def partial_rotary(t, positions):
    half = ROPE_DIM // 2
    inv_freq = ROPE_THETA ** (-2.0 * jnp.arange(half, dtype=jnp.float32) / ROPE_DIM)
    ang = positions.astype(jnp.float32)[..., None] * inv_freq
    cos = jnp.cos(ang)[:, :, None, :]
    sin = jnp.sin(ang)[:, :, None, :]
    tr = t[..., :ROPE_DIM].astype(jnp.float32)
    t1, t2 = tr[..., :half], tr[..., half:]
    rot = jnp.concatenate([t1 * cos - t2 * sin, t2 * cos + t1 * sin], axis=-1)
    return jnp.concatenate([rot.astype(t.dtype), t[..., ROPE_DIM:]], axis=-1)


def window_gqa(q, k, v, sinks):
    B, S = q.shape[0], q.shape[1]
    nb = S // BLOCK
    qb = q.reshape(B, nb, BLOCK, ATTN_KV_HEADS, ATTN_GROUP, ATTN_HEAD_DIM)
    pad = ((0, 0), (BLOCK, BLOCK), (0, 0), (0, 0))

    def bands(t):
        tb = jnp.pad(t, pad).reshape(B, nb + 2, BLOCK, ATTN_KV_HEADS, ATTN_HEAD_DIM)
        return jnp.concatenate([tb[:, :nb], tb[:, 1:nb + 1], tb[:, 2:]], axis=2)

    kb, vb = bands(k), bands(v)
    scores = jnp.einsum('bnqhgd,bnkhd->bnhgqk', qb, kb,
                        preferred_element_type=jnp.float32) * (ATTN_HEAD_DIM ** -0.5)
    q_idx = jnp.arange(BLOCK)
    k_idx = jnp.arange(3 * BLOCK)
    rel = k_idx[None, :] - BLOCK - q_idx[:, None]
    k_abs = jnp.arange(nb)[:, None] * BLOCK - BLOCK + k_idx[None, :]
    mask = (jnp.abs(rel) <= WINDOW)[None] & ((k_abs >= 0) & (k_abs < S))[:, None, :]
    scores = jnp.where(mask[None, :, None, None], scores, NEG)
    s = sinks.astype(jnp.float32).reshape(ATTN_KV_HEADS, ATTN_GROUP)[None, None, :, :, None, None]
    m = jnp.maximum(jnp.max(scores, axis=-1, keepdims=True), s)
    p = jnp.exp(scores - m)
    probs = p / (jnp.sum(p, axis=-1, keepdims=True) + jnp.exp(s - m))
    out = jnp.einsum('bnhgqk,bnkhd->bnqhgd', probs.astype(v.dtype), vb)
    return out.reshape(B, S, Q_W)


def memory_attention(mq, mk, mv):
    B, S = mq.shape[0], mq.shape[1]
    sc = jnp.einsum('bshd,bmhd->bhsm', mq, mk,
                    preferred_element_type=jnp.float32) * (MEM_HEAD_DIM ** -0.5)
    p = jax.nn.softmax(sc, axis=-1)
    out = jnp.einsum('bhsm,bmhd->bshd', p.astype(mv.dtype), mv)
    return out.reshape(B, S, MEM_W)


def centred_depthwise_conv(x, w, b):
    S = x.shape[1]
    xp = jnp.pad(x, ((0, 0), (CONV_LEFT, CONV_WIDTH - 1 - CONV_LEFT), (0, 0)))
    y = b
    for tap in range(CONV_WIDTH):
        y = y + xp[:, tap:tap + S] * w[tap]
    return y


def block_diag_linear(x, w, b):
    B, S = x.shape[0], x.shape[1]
    xr = x.reshape(B, S, LRU_BLOCKS, LRU_BLOCK_DIM)
    return jnp.einsum('bsnd,nde->bsne', xr, w).reshape(B, S, LRU_WIDTH) + b


def _linear_combine(c1, c2):
    a1, b1 = c1
    a2, b2 = c2
    return a1 * a2, a2 * b1 + b2


def rg_lru(x, wa, ba, wx, bx, lam, reverse):
    xf = x.astype(jnp.float32)
    r = jax.nn.sigmoid(block_diag_linear(x, wa, ba).astype(jnp.float32))
    i = jax.nn.sigmoid(block_diag_linear(x, wx, bx).astype(jnp.float32))
    log_a = -LRU_C * r * jax.nn.softplus(-lam.astype(jnp.float32))
    a = jnp.exp(log_a)
    u = jnp.sqrt(-jnp.expm1(2.0 * log_a)) * (i * xf)
    _, h = lax.associative_scan(_linear_combine, (a, u), axis=1, reverse=reverse)
    return h.astype(x.dtype)


def attn_mixer(h, positions, w_in, sinks):
    B, S = h.shape[0], h.shape[1]
    p = h @ w_in
    q, k, v, mq = jnp.split(p, [Q_W, Q_W + KV_W, Q_W + 2 * KV_W], axis=-1)
    q = partial_rotary(q.reshape(B, S, ATTN_HEADS, ATTN_HEAD_DIM), positions)
    k = partial_rotary(k.reshape(B, S, ATTN_KV_HEADS, ATTN_HEAD_DIM), positions)
    v = v.reshape(B, S, ATTN_KV_HEADS, ATTN_HEAD_DIM)
    return window_gqa(q, k, v, sinks), mq.reshape(B, S, MEM_HEADS, MEM_HEAD_DIM)


def lru_mixer(h, w_in, conv_w, conv_b, wa, ba, wx, bx, lam):
    B, S = h.shape[0], h.shape[1]
    p = h @ w_in
    xb, gate, mq = jnp.split(p, [LRU_WIDTH, 2 * LRU_WIDTH], axis=-1)
    xc = centred_depthwise_conv(xb, conv_w, conv_b)
    y = (rg_lru(xc, wa[0], ba[0], wx[0], bx[0], lam[0], False)
         + rg_lru(xc, wa[1], ba[1], wx[1], bx[1], lam[1], True))
    y = y * jax.nn.gelu(gate)
    return y, mq.reshape(B, S, MEM_HEADS, MEM_HEAD_DIM)


def squared_relu_mlp(h, w_up, w_down):
    return jnp.square(jax.nn.relu(h @ w_up)) @ w_down


def setup_inputs(seed: int = 0) -> dict:
    key = jax.random.key(seed)
    ks = jax.random.split(key, 24)
    n_attn = (DEPTH + N_MIXERS - 1) // N_MIXERS
    n_lru = DEPTH // N_MIXERS
    f32 = jnp.float32

    def nrm(k, shape, scale):
        return jax.random.normal(k, shape, f32) * scale

    u = jax.random.uniform(ks[20], (n_lru, 2, LRU_WIDTH), f32, minval=0.9, maxval=0.999)
    s = u ** (1.0 / LRU_C)
    lam = jnp.log(s) - jnp.log1p(-s)
    positions = (jnp.arange(SEQ, dtype=jnp.int32)[None, :]
                 + jax.random.randint(ks[21], (BATCH, 1), 0, 1024, dtype=jnp.int32))
    return {
        "x": nrm(ks[0], (BATCH, SEQ, D_MODEL), 1.0),
        "mem": nrm(ks[1], (BATCH, MEM_LEN, D_MODEL), 1.0),
        "positions": positions,
        "mix_norm": 1.0 + nrm(ks[2], (DEPTH, D_MODEL), 0.05),
        "mlp_norm": 1.0 + nrm(ks[3], (DEPTH, D_MODEL), 0.05),
        "mem_norm": 1.0 + nrm(ks[4], (D_MODEL,), 0.05),
        "final_norm": 1.0 + nrm(ks[5], (D_MODEL,), 0.05),
        "w_mem_kv": nrm(ks[6], (DEPTH, D_MODEL, 2 * MEM_W), D_MODEL ** -0.5),
        "w_out": nrm(ks[7], (DEPTH, MIX_OUT_W, D_MODEL), MIX_OUT_W ** -0.5),
        "w_up": nrm(ks[8], (DEPTH, D_MODEL, D_FF), D_MODEL ** -0.5),
        "w_down": nrm(ks[9], (DEPTH, D_FF, D_MODEL), D_FF ** -0.5),
        "attn_w_in": nrm(ks[10], (n_attn, D_MODEL, ATTN_IN_W), D_MODEL ** -0.5),
        "attn_sinks": nrm(ks[11], (n_attn, ATTN_HEADS), 0.5),
        "lru_w_in": nrm(ks[12], (n_lru, D_MODEL, LRU_IN_W), D_MODEL ** -0.5),
        "lru_conv_w": nrm(ks[13], (n_lru, CONV_WIDTH, LRU_WIDTH), CONV_WIDTH ** -0.5),
        "lru_conv_b": nrm(ks[14], (n_lru, LRU_WIDTH), 0.02),
        "lru_wa": nrm(ks[15], (n_lru, 2, LRU_BLOCKS, LRU_BLOCK_DIM, LRU_BLOCK_DIM), LRU_BLOCK_DIM ** -0.5),
        "lru_ba": nrm(ks[16], (n_lru, 2, LRU_WIDTH), 0.02),
        "lru_wx": nrm(ks[17], (n_lru, 2, LRU_BLOCKS, LRU_BLOCK_DIM, LRU_BLOCK_DIM), LRU_BLOCK_DIM ** -0.5),
        "lru_bx": nrm(ks[18], (n_lru, 2, LRU_WIDTH), 0.02),
        "lru_lambda": lam,
    }


def reference(x, mem, positions, mix_norm, mlp_norm, mem_norm, final_norm, w_mem_kv, w_out,
              w_up, w_down, attn_w_in, attn_sinks, lru_w_in, lru_conv_w, lru_conv_b,
              lru_wa, lru_ba, lru_wx, lru_bx, lru_lambda):
    B = mem.shape[0]
    mem_n = rms_norm(mem, mem_norm)
    for l in range(DEPTH):
        kind = l % N_MIXERS
        j = l // N_MIXERS
        h = rms_norm(x, mix_norm[l])
        kv = mem_n @ w_mem_kv[l]
        mk = kv[..., :MEM_W].reshape(B, MEM_LEN, MEM_HEADS, MEM_HEAD_DIM)
        mv = kv[..., MEM_W:].reshape(B, MEM_LEN, MEM_HEADS, MEM_HEAD_DIM)
        if kind == 0:
            mixed, mq = attn_mixer(h, positions, attn_w_in[j], attn_sinks[j])
        else:
            mixed, mq = lru_mixer(h, lru_w_in[j], lru_conv_w[j], lru_conv_b[j], lru_wa[j],
                                  lru_ba[j], lru_wx[j], lru_bx[j], lru_lambda[j])
        mo = memory_attention(mq, mk, mv)
        x = x + jnp.concatenate([mixed, mo], axis=-1) @ w_out[l]
        x = x + squared_relu_mlp(rms_norm(x, mlp_norm[l]), w_up[l], w_down[l])
    return rms_norm(x, final_norm)
```

```python
import functools

import jax
import jax.numpy as jnp
from jax import lax
from jax.experimental import pallas as pl
from jax.experimental.pallas import tpu as pltpu

F32 = jnp.float32
BF16 = jnp.bfloat16

D_MODEL = 1024
EPS = 1e-6
ATTN_HEADS = 16
ATTN_KV_HEADS = 4
ATTN_HEAD_DIM = 64
ATTN_GROUP = 4
WINDOW = 128
BLOCK = 128
ROPE_THETA = 500000.0
ROPE_DIM = 16
Q_W = 1024
KV_W = 256
MEM_LEN = 256
MEM_HEADS = 4
MEM_HEAD_DIM = 128
MEM_W = 512
LRU_WIDTH = 1024
LRU_BLOCKS = 8
LRU_BLOCK_DIM = 128
LRU_C = 8.0
CONV_WIDTH = 4
D_FF = 4096
NEG = -1e30

LANES = 128
SUBLANES = 8
BF16_ROWS = 16
KV_DUP_W = 2 * KV_W
ATTN_IN_COLS = Q_W + 2 * KV_DUP_W + MEM_W
LRU_IN_COLS = 2 * LRU_WIDTH + MEM_W
VMEM_LIMIT = 56 * 1024 * 1024

ROW_TILE = 512
LRU_TILE = 256
FF_CHUNK = 1024


def _const_spec(shape):
    nd = len(shape)
    return pl.BlockSpec(shape, lambda *_: (0,) * nd, pipeline_mode=pl.Buffered(1))


def _params(n_axes):
    return pltpu.CompilerParams(dimension_semantics=("arbitrary",) * n_axes,
                                vmem_limit_bytes=VMEM_LIMIT)


def _rms(x, g):
    ms = jnp.mean(x * x, axis=-1, keepdims=True)
    return x * lax.rsqrt(ms + EPS) * g


def _norm_proj_kernel(x_ref, g_ref, cs_ref, w_ref, o_ref, *, chunk):
    xn = _rms(x_ref[...], g_ref[...]).astype(BF16)
    for c in range(0, o_ref.shape[-1], chunk):
        acc = jnp.dot(xn, w_ref[:, c:c + chunk], preferred_element_type=F32)
        o_ref[:, c:c + chunk] = (acc * cs_ref[:, c:c + chunk]).astype(BF16)


def _norm_proj(x, g, col_scale, w, tile):
    t, d = x.shape
    n = w.shape[1]
    return pl.pallas_call(
        functools.partial(_norm_proj_kernel, chunk=512),
        grid=(t // tile,),
        in_specs=[pl.BlockSpec((tile, d), lambda i: (i, 0)),
                  _const_spec((1, d)), _const_spec((1, n)), _const_spec((d, n))],
        out_specs=pl.BlockSpec((tile, n), lambda i: (i, 0)),
        out_shape=jax.ShapeDtypeStruct((t, n), BF16),
        compiler_params=_params(1),
        name="norm_proj",
    )(x, g, col_scale, w)


def _attn_in_kernel(x_ref, g_ref, pos_ref, tab_ref, w_ref, o_ref):
    xn = _rms(x_ref[...], g_ref[...]).astype(BF16)
    ang = pos_ref[...].astype(F32) * tab_ref[0:1, :]
    cos = jnp.cos(ang)
    sin = jnp.sin(ang)
    sin_lo = sin * tab_ref[1:2, :]
    sin_hi = sin * tab_ref[2:3, :]
    q_scale = ATTN_HEAD_DIM ** -0.5
    mem_scale = MEM_HEAD_DIM ** -0.5
    half = ROPE_DIM // 2
    rot_cols = Q_W + KV_DUP_W
    for c in range(0, ATTN_IN_COLS, 2 * LANES):
        acc = jnp.dot(xn, w_ref[:, c:c + 2 * LANES], preferred_element_type=F32)
        for s in range(2):
            col = c + s * LANES
            slab = acc[:, s * LANES:(s + 1) * LANES]
            if col < rot_cols:
                slab = (slab * cos + pltpu.roll(slab, LANES - half, 1) * sin_lo
                        + pltpu.roll(slab, half, 1) * sin_hi)
                if col < Q_W:
                    slab = slab * q_scale
            elif col >= Q_W + 2 * KV_DUP_W:
                slab = slab * mem_scale
            o_ref[:, col:col + LANES] = slab.astype(BF16)


def _attn_in(x, g, pos, tab, w, tile):
    t, d = x.shape
    n = w.shape[1]
    return pl.pallas_call(
        _attn_in_kernel,
        grid=(t // tile,),
        in_specs=[pl.BlockSpec((tile, d), lambda i: (i, 0)),
                  _const_spec((1, d)),
                  pl.BlockSpec((tile, 1), lambda i: (i, 0)),
                  _const_spec((SUBLANES, LANES)), _const_spec((d, n))],
        out_specs=pl.BlockSpec((tile, n), lambda i: (i, 0)),
        out_shape=jax.ShapeDtypeStruct((t, n), BF16),
        compiler_params=_params(1),
        name="attn_in",
    )(x, g, pos, tab, w)


def _mem_attention(mq_ref, mk_ref, mv_ref, mo_ref):
    for h in range(MEM_HEADS):
        sl = slice(h * MEM_HEAD_DIM, (h + 1) * MEM_HEAD_DIM)
        s = lax.dot_general(mq_ref[:, sl], mk_ref[:, sl], (((1,), (1,)), ((), ())),
                            preferred_element_type=F32)
        m = jnp.max(s, axis=-1, keepdims=True)
        p = jnp.exp(s - m)
        den = jnp.sum(p, axis=-1, keepdims=True)
        o = jnp.dot(p.astype(BF16), mv_ref[:, sl], preferred_element_type=F32)
        mo_ref[:, sl] = (o / den).astype(BF16)


def _attn_kernel(sink_ref, q_ref, kp_ref, kc_ref, kn_ref, vp_ref, vc_ref, vn_ref,
                 mq_ref, mk_ref, mv_ref, o_ref, mo_ref, *, seq):
    i = pl.program_id(1)
    row = lax.broadcasted_iota(jnp.int32, (BLOCK, 3 * BLOCK), 0)
    col = lax.broadcasted_iota(jnp.int32, (BLOCK, 3 * BLOCK), 1)
    rel = col - BLOCK - row
    k_abs = i * BLOCK - BLOCK + col
    mask = (jnp.abs(rel) <= WINDOW) & (k_abs >= 0) & (k_abs < seq)
    lo = lax.broadcasted_iota(jnp.int32, (BLOCK, LANES), 1) < ATTN_HEAD_DIM
    k3 = jnp.concatenate([kp_ref[...], kc_ref[...], kn_ref[...]], axis=0)
    v3 = jnp.concatenate([vp_ref[...], vc_ref[...], vn_ref[...]], axis=0)
    zero = jnp.zeros((), BF16)
    for h in range(ATTN_KV_HEADS):
        kh = k3[:, h * LANES:(h + 1) * LANES]
        vh = v3[:, h * LANES:(h + 1) * LANES]
        parts = []
        for g in range(ATTN_GROUP):
            slab = q_ref[:, (2 * h + g // 2) * LANES:(2 * h + g // 2 + 1) * LANES]
            parts.append(jnp.where(lo if g % 2 == 0 else ~lo, slab, zero))
        qs = jnp.concatenate(parts, axis=0)
        s = lax.dot_general(qs, kh, (((1,), (1,)), ((), ())),
                            preferred_element_type=F32)
        probs, dens = [], []
        for g in range(ATTN_GROUP):
            sg = jnp.where(mask, s[g * BLOCK:(g + 1) * BLOCK], NEG)
            sink = sink_ref[ATTN_GROUP * h + g]
            m = jnp.maximum(jnp.max(sg, axis=-1, keepdims=True), sink)
            p = jnp.exp(sg - m)
            dens.append(jnp.sum(p, axis=-1, keepdims=True) + jnp.exp(sink - m))
            probs.append(p.astype(BF16))
        o = jnp.dot(jnp.concatenate(probs, axis=0), vh, preferred_element_type=F32)
        outs = [o[g * BLOCK:(g + 1) * BLOCK] / dens[g] for g in range(ATTN_GROUP)]
        for pair in range(2):
            slab = jnp.where(lo, outs[2 * pair], outs[2 * pair + 1])
            o_ref[:, (2 * h + pair) * LANES:(2 * h + pair + 1) * LANES] = slab.astype(BF16)
    _mem_attention(mq_ref, mk_ref, mv_ref, mo_ref)


def _attention(p, memkv, sinks, layer, batch, seq):
    nb = seq // BLOCK
    p3 = p.reshape(batch, seq, ATTN_IN_COLS)
    kcol = Q_W // KV_DUP_W
    vcol = kcol + 1
    mqcol = (Q_W + 2 * KV_DUP_W) // MEM_W

    def kv_spec(colblk, shift):
        def idx(b, i):
            return (b, jnp.clip(i + shift, 0, nb - 1), colblk)
        return pl.BlockSpec((None, BLOCK, KV_DUP_W), idx)

    out, mo = pl.pallas_call(
        functools.partial(_attn_kernel, seq=seq),
        grid=(batch, nb),
        in_specs=[pl.BlockSpec(memory_space=pltpu.SMEM),
                  pl.BlockSpec((None, BLOCK, Q_W), lambda b, i: (b, i, 0)),
                  kv_spec(kcol, -1), kv_spec(kcol, 0), kv_spec(kcol, 1),
                  kv_spec(vcol, -1), kv_spec(vcol, 0), kv_spec(vcol, 1),
                  pl.BlockSpec((None, BLOCK, MEM_W), lambda b, i: (b, i, mqcol)),
                  pl.BlockSpec((None, MEM_LEN, MEM_W), lambda b, i: (b, 0, 2 * layer)),
                  pl.BlockSpec((None, MEM_LEN, MEM_W), lambda b, i: (b, 0, 2 * layer + 1))],
        out_specs=[pl.BlockSpec((None, BLOCK, Q_W), lambda b, i: (b, i, 0)),
                   pl.BlockSpec((None, BLOCK, MEM_W), lambda b, i: (b, i, 0))],
        out_shape=[jax.ShapeDtypeStruct((batch, seq, Q_W), BF16),
                   jax.ShapeDtypeStruct((batch, seq, MEM_W), BF16)],
        compiler_params=_params(2),
        name="attention",
    )(sinks, p3, p3, p3, p3, p3, p3, p3, p3, memkv, memkv)
    return out.reshape(batch * seq, Q_W), mo.reshape(batch * seq, MEM_W)


def _out_mlp_kernel(x_ref, y_ref, mo_ref, wo_ref, g_ref, wup_ref, wdn_ref, fg_ref, o_ref,
                    acc_ref, xn_ref, *, final):
    mix_w = y_ref.shape[-1]
    x1 = (x_ref[...]
          + jnp.dot(y_ref[...], wo_ref[0:mix_w, :], preferred_element_type=F32)
          + jnp.dot(mo_ref[...], wo_ref[mix_w:, :], preferred_element_type=F32))
    acc_ref[...] = x1
    xn_ref[...] = _rms(x1, g_ref[...]).astype(BF16)

    def ff_chunk(c, carry):
        c0 = pl.multiple_of(c * FF_CHUNK, FF_CHUNK)
        hid = jnp.dot(xn_ref[...], wup_ref[:, pl.ds(c0, FF_CHUNK)], preferred_element_type=F32)
        hid = jnp.maximum(hid, 0.0)
        hid = (hid * hid).astype(BF16)
        acc_ref[...] += jnp.dot(hid, wdn_ref[pl.ds(c0, FF_CHUNK), :], preferred_element_type=F32)
        return carry

    lax.fori_loop(0, D_FF // FF_CHUNK, ff_chunk, 0)
    acc = acc_ref[...]
    if final:
        acc = _rms(acc, fg_ref[...])
    o_ref[...] = acc


def _out_mlp(x, y, mo, wo, g, wup, wdn, fg, final, tile):
    t, d = x.shape
    return pl.pallas_call(
        functools.partial(_out_mlp_kernel, final=final),
        grid=(t // tile,),
        in_specs=[pl.BlockSpec((tile, d), lambda i: (i, 0)),
                  pl.BlockSpec((tile, y.shape[1]), lambda i: (i, 0)),
                  pl.BlockSpec((tile, mo.shape[1]), lambda i: (i, 0)),
                  _const_spec(wo.shape), _const_spec((1, d)),
                  _const_spec(wup.shape), _const_spec(wdn.shape), _const_spec((1, d))],
        out_specs=pl.BlockSpec((tile, d), lambda i: (i, 0)),
        out_shape=jax.ShapeDtypeStruct((t, d), F32),
        scratch_shapes=[pltpu.VMEM((tile, d), F32), pltpu.VMEM((tile, d), BF16)],
        compiler_params=_params(1),
        name="out_mlp",
    )(x, y, mo, wo, g, wup, wdn, fg)


def _conv_tile(prev_ref, cur_ref, next_ref, cw_ref, cb_ref, ext_ref, first, last):
    ts = cur_ref.shape[0]
    h = BF16_ROWS
    ext_ref[0:h, :] = jnp.where(first, 0.0, prev_ref[...].astype(F32))
    ext_ref[h:h + ts, :] = cur_ref[...].astype(F32)
    ext_ref[h + ts:2 * h + ts, :] = jnp.where(last, 0.0, next_ref[...].astype(F32))
    xc = cb_ref[...]
    for tap in range(CONV_WIDTH):
        xc = xc + ext_ref[h - 1 + tap:h - 1 + tap + ts, :] * cw_ref[tap:tap + 1, :]
    return xc


def _local_scan(a, u, reverse):
    ts, w = a.shape
    n = ts // SUBLANES
    a3 = a.reshape(n, SUBLANES, w)
    u3 = u.reshape(n, SUBLANES, w)
    row = lax.broadcasted_iota(jnp.int32, (n, SUBLANES, w), 1)
    for k in (1, 2, 4):
        if reverse:
            shift, valid = SUBLANES - k, row < SUBLANES - k
        else:
            shift, valid = k, row >= k
        a_s = jnp.where(valid, pltpu.roll(a3, shift, 1), 1.0)
        u_s = jnp.where(valid, pltpu.roll(u3, shift, 1), 0.0)
        u3 = u3 + a3 * u_s
        a3 = a3 * a_s
    return a3.reshape(ts, w), u3.reshape(ts, w)


def _lru_tile(xc, wg_ref, ba_ref, bx_ref, lam_ref, p_ref, h_ref, carry_ref, reverse):
    ts = xc.shape[0]
    xcb = xc.astype(BF16)
    lam = lam_ref[...]
    softplus_neg_lam = jnp.maximum(-lam, 0.0) + jnp.log1p(jnp.exp(-jnp.abs(lam)))
    for n in range(LRU_BLOCKS):
        sl = slice(n * LRU_BLOCK_DIM, (n + 1) * LRU_BLOCK_DIM)
        gates = jnp.dot(xcb[:, sl], wg_ref[n], preferred_element_type=F32)
        r = jax.nn.sigmoid(gates[:, :LRU_BLOCK_DIM] + ba_ref[:, sl])
        i_gate = jax.nn.sigmoid(gates[:, LRU_BLOCK_DIM:] + bx_ref[:, sl])
        log_a = (-LRU_C) * r * softplus_neg_lam[:, sl]
        a = jnp.exp(log_a)
        u = jnp.sqrt(1.0 - a * a) * (i_gate * xc[:, sl])
        p_loc, h_loc = _local_scan(a, u, reverse)
        p_ref[:, sl] = p_loc
        h_ref[:, sl] = h_loc
    n_groups = ts // SUBLANES

    def body(j, carry):
        jj = (n_groups - 1 - j) if reverse else j
        r0 = pl.multiple_of(jj * SUBLANES, SUBLANES)
        hh = h_ref[pl.ds(r0, SUBLANES), :] + p_ref[pl.ds(r0, SUBLANES), :] * carry
        h_ref[pl.ds(r0, SUBLANES), :] = hh
        return hh[0:1, :] if reverse else hh[SUBLANES - 1:SUBLANES, :]

    carry_ref[...] = lax.fori_loop(0, n_groups, body, carry_ref[...])


def _lru_fwd_kernel(xp_ref, xc_ref, xn_ref, mq_ref, mk_ref, mv_ref, cw_ref, cb_ref, wg_ref,
                    ba_ref, bx_ref, lam_ref, hf_ref, mo_ref, ext_ref, p_ref, h_ref, carry_ref):
    i = pl.program_id(1)
    nt = pl.num_programs(1)

    @pl.when(i == 0)
    def _():
        carry_ref[...] = jnp.zeros_like(carry_ref)

    xc = _conv_tile(xp_ref, xc_ref, xn_ref, cw_ref, cb_ref, ext_ref, i == 0, i == nt - 1)
    _lru_tile(xc, wg_ref, ba_ref, bx_ref, lam_ref, p_ref, h_ref, carry_ref, reverse=False)
    hf_ref[...] = h_ref[...].astype(BF16)
    _mem_attention(mq_ref, mk_ref, mv_ref, mo_ref)


def _lru_rev_kernel(xp_ref, xc_ref, xn_ref, gate_ref, hf_ref, cw_ref, cb_ref, wg_ref,
                    ba_ref, bx_ref, lam_ref, y_ref, ext_ref, p_ref, h_ref, carry_ref):
    i = pl.program_id(1)
    nt = pl.num_programs(1)

    @pl.when(i == 0)
    def _():
        carry_ref[...] = jnp.zeros_like(carry_ref)

    xc = _conv_tile(xp_ref, xc_ref, xn_ref, cw_ref, cb_ref, ext_ref, i == nt - 1, i == 0)
    _lru_tile(xc, wg_ref, ba_ref, bx_ref, lam_ref, p_ref, h_ref, carry_ref, reverse=True)
    y = (hf_ref[...].astype(F32) + h_ref[...]) * jax.nn.gelu(gate_ref[...].astype(F32))
    y_ref[...] = y.astype(BF16)


def _lru_specs(ts, nt, seq, reverse):
    per = ts // BF16_ROWS
    last_halo = seq // BF16_ROWS - 1

    def t_idx(i):
        return (nt - 1 - i) if reverse else i

    prev = pl.BlockSpec((None, BF16_ROWS, LRU_WIDTH),
                        lambda b, i: (b, jnp.maximum(t_idx(i) * per - 1, 0), 0))
    cur = pl.BlockSpec((None, ts, LRU_WIDTH), lambda b, i: (b, t_idx(i), 0))
    nxt = pl.BlockSpec((None, BF16_ROWS, LRU_WIDTH),
                       lambda b, i: (b, jnp.minimum((t_idx(i) + 1) * per, last_halo), 0))
    return t_idx, prev, cur, nxt


def _lru_weight_specs():
    return [_const_spec((CONV_WIDTH, LRU_WIDTH)), _const_spec((1, LRU_WIDTH)),
            _const_spec((LRU_BLOCKS, LRU_BLOCK_DIM, 2 * LRU_BLOCK_DIM)),
            _const_spec((1, LRU_WIDTH)), _const_spec((1, LRU_WIDTH)), _const_spec((1, LRU_WIDTH))]


def _lru_scratch(ts):
    return [pltpu.VMEM((ts + 2 * BF16_ROWS, LRU_WIDTH), F32),
            pltpu.VMEM((ts, LRU_WIDTH), F32),
            pltpu.VMEM((ts, LRU_WIDTH), F32),
            pltpu.VMEM((1, LRU_WIDTH), F32)]


def _lru_forward(p3, memkv, layer, weights, batch, seq, ts):
    nt = seq // ts
    t_idx, prev, cur, nxt = _lru_specs(ts, nt, seq, reverse=False)
    mqcol = (2 * LRU_WIDTH) // MEM_W
    return pl.pallas_call(
        _lru_fwd_kernel,
        grid=(batch, nt),
        in_specs=[prev, cur, nxt,
                  pl.BlockSpec((None, ts, MEM_W), lambda b, i: (b, i, mqcol)),
                  pl.BlockSpec((None, MEM_LEN, MEM_W), lambda b, i: (b, 0, 2 * layer)),
                  pl.BlockSpec((None, MEM_LEN, MEM_W), lambda b, i: (b, 0, 2 * layer + 1))]
        + _lru_weight_specs(),
        out_specs=[pl.BlockSpec((None, ts, LRU_WIDTH), lambda b, i: (b, i, 0)),
                   pl.BlockSpec((None, ts, MEM_W), lambda b, i: (b, i, 0))],
        out_shape=[jax.ShapeDtypeStruct((batch, seq, LRU_WIDTH), BF16),
                   jax.ShapeDtypeStruct((batch, seq, MEM_W), BF16)],
        scratch_shapes=_lru_scratch(ts),
        compiler_params=_params(2),
        name="lru_forward",
    )(p3, p3, p3, p3, memkv, memkv, *weights)


def _lru_reverse(p3, hf, weights, batch, seq, ts):
    nt = seq // ts
    t_idx, prev, cur, nxt = _lru_specs(ts, nt, seq, reverse=True)
    return pl.pallas_call(
        _lru_rev_kernel,
        grid=(batch, nt),
        in_specs=[prev, cur, nxt,
                  pl.BlockSpec((None, ts, LRU_WIDTH), lambda b, i: (b, t_idx(i), 1)),
                  pl.BlockSpec((None, ts, LRU_WIDTH), lambda b, i: (b, t_idx(i), 0))]
        + _lru_weight_specs(),
        out_specs=pl.BlockSpec((None, ts, LRU_WIDTH), lambda b, i: (b, t_idx(i), 0)),
        out_shape=jax.ShapeDtypeStruct((batch, seq, LRU_WIDTH), BF16),
        scratch_shapes=_lru_scratch(ts),
        compiler_params=_params(2),
        name="lru_reverse",
    )(p3, p3, p3, p3, hf, *weights)


def _dup_heads(w):
    d = w.shape[0]
    w4 = w.reshape(d, ATTN_KV_HEADS, 1, ATTN_HEAD_DIM)
    return jnp.broadcast_to(w4, (d, ATTN_KV_HEADS, 2, ATTN_HEAD_DIM)).reshape(d, KV_DUP_W)


def _rotary_table():
    half = ROPE_DIM // 2
    inv_freq = ROPE_THETA ** (-2.0 * jnp.arange(half, dtype=F32) / ROPE_DIM)
    d = jnp.arange(LANES) % ATTN_HEAD_DIM
    freq = jnp.where(d < ROPE_DIM, inv_freq[d % half], 0.0)
    lo = jnp.where(d < half, -1.0, 0.0)
    hi = jnp.where((d >= half) & (d < ROPE_DIM), 1.0, 0.0)
    tab = jnp.zeros((SUBLANES, LANES), F32)
    return tab.at[0].set(freq).at[1].set(lo).at[2].set(hi)


def _gate_weights(wa, wx):
    return jnp.concatenate([wa, wx], axis=-1).astype(BF16)


def kernel(x, mem, positions, mix_norm, mlp_norm, mem_norm, final_norm, w_mem_kv, w_out, w_up, w_down,
           attn_w_in, attn_sinks, lru_w_in, lru_conv_w, lru_conv_b, lru_wa, lru_ba, lru_wx, lru_bx,
           lru_lambda):
    batch, seq, d = x.shape
    t = batch * seq
    depth = mix_norm.shape[0]
    xf = x.reshape(t, d)
    ones_d = jnp.ones((1, d), F32)

    w_kv_all = jnp.concatenate([w_mem_kv[l] for l in range(depth)], axis=1).astype(BF16)
    memkv = _norm_proj(mem.reshape(batch * MEM_LEN, d), mem_norm.reshape(1, d),
                       jnp.ones((1, w_kv_all.shape[1]), F32), w_kv_all,
                       tile=batch * MEM_LEN // 2)
    memkv = memkv.reshape(batch, MEM_LEN, depth * 2 * MEM_W)

    pos_col = positions.reshape(t, 1).astype(jnp.int32)
    rot_tab = _rotary_table()
    lru_col_scale = jnp.concatenate([jnp.ones((1, 2 * LRU_WIDTH), F32),
                                     jnp.full((1, MEM_W), MEM_HEAD_DIM ** -0.5, F32)], axis=1)

    for l in range(depth):
        kind, j = l % 2, l // 2
        g_mix = mix_norm[l].reshape(1, d)
        if kind == 0:
            w = attn_w_in[j]
            w_in = jnp.concatenate([w[:, :Q_W], _dup_heads(w[:, Q_W:Q_W + KV_W]),
                                    _dup_heads(w[:, Q_W + KV_W:Q_W + 2 * KV_W]),
                                    w[:, Q_W + 2 * KV_W:]], axis=1).astype(BF16)
            p = _attn_in(xf, g_mix, pos_col, rot_tab, w_in, ROW_TILE)
            y, mo = _attention(p, memkv, attn_sinks[j].astype(F32), l, batch, seq)
        else:
            p = _norm_proj(xf, g_mix, lru_col_scale, lru_w_in[j].astype(BF16), ROW_TILE)
            p3 = p.reshape(batch, seq, LRU_IN_COLS)
            cw, cb = lru_conv_w[j], lru_conv_b[j].reshape(1, LRU_WIDTH)

            def dir_weights(dirn):
                return (cw, cb, _gate_weights(lru_wa[j, dirn], lru_wx[j, dirn]),
                        lru_ba[j, dirn].reshape(1, LRU_WIDTH), lru_bx[j, dirn].reshape(1, LRU_WIDTH),
                        lru_lambda[j, dirn].reshape(1, LRU_WIDTH))

            hf, mo = _lru_forward(p3, memkv, l, dir_weights(0), batch, seq, LRU_TILE)
            y = _lru_reverse(p3, hf, dir_weights(1), batch, seq, LRU_TILE)
            y = y.reshape(t, LRU_WIDTH)
            mo = mo.reshape(t, MEM_W)
        final = l == depth - 1
        xf = _out_mlp(xf, y, mo, w_out[l].astype(BF16), mlp_norm[l].reshape(1, d),
                      w_up[l].astype(BF16), w_down[l].astype(BF16),
                      final_norm.reshape(1, d) if final else ones_d, final, ROW_TILE)
    return xf.reshape(batch, seq, d)
```

```python
import functools

import jax
import jax.numpy as jnp
from jax import lax
from jax.experimental import pallas as pl
from jax.experimental.pallas import tpu as pltpu

F32 = jnp.float32
BF16 = jnp.bfloat16

D_MODEL = 1024
EPS = 1e-6
ATTN_HEADS = 16
ATTN_KV_HEADS = 4
ATTN_HEAD_DIM = 64
ATTN_GROUP = 4
WINDOW = 128
BLOCK = 128
ROPE_THETA = 500000.0
ROPE_DIM = 16
Q_W = 1024
KV_W = 256
MEM_LEN = 256
MEM_HEADS = 4
MEM_HEAD_DIM = 128
MEM_W = 512
LRU_WIDTH = 1024
LRU_BLOCKS = 8
LRU_BLOCK_DIM = 128
LRU_C = 8.0
CONV_WIDTH = 4
D_FF = 4096
NEG = -1e30

LANES = 128
SUBLANES = 8
BF16_ROWS = 16
KV_DUP_W = 2 * KV_W
ATTN_IN_COLS = Q_W + 2 * KV_DUP_W + MEM_W
LRU_IN_COLS = 2 * LRU_WIDTH + MEM_W
VMEM_LIMIT = 56 * 1024 * 1024

ROW_TILE = 512
LRU_TILE = 256
FF_CHUNK = 1024
ATTN_TILE = 512
LOG2E = 1.4426950408889634


def _const_spec(shape):
    nd = len(shape)
    return pl.BlockSpec(shape, lambda *_: (0,) * nd, pipeline_mode=pl.Buffered(1))


def _params(n_axes):
    return pltpu.CompilerParams(dimension_semantics=("arbitrary",) * n_axes,
                                vmem_limit_bytes=VMEM_LIMIT)


def _rms(x, g):
    ms = jnp.mean(x * x, axis=-1, keepdims=True)
    return x * lax.rsqrt(ms + EPS) * g


def _norm_proj_kernel(x_ref, g_ref, cs_ref, w_ref, o_ref, *, chunk):
    xn = _rms(x_ref[...], g_ref[...]).astype(BF16)
    for c in range(0, o_ref.shape[-1], chunk):
        acc = jnp.dot(xn, w_ref[:, c:c + chunk], preferred_element_type=F32)
        o_ref[:, c:c + chunk] = (acc * cs_ref[:, c:c + chunk]).astype(BF16)


def _norm_proj(x, g, col_scale, w, tile):
    t, d = x.shape
    n = w.shape[1]
    return pl.pallas_call(
        functools.partial(_norm_proj_kernel, chunk=512),
        grid=(t // tile,),
        in_specs=[pl.BlockSpec((tile, d), lambda i: (i, 0)),
                  _const_spec((1, d)), _const_spec((1, n)), _const_spec((d, n))],
        out_specs=pl.BlockSpec((tile, n), lambda i: (i, 0)),
        out_shape=jax.ShapeDtypeStruct((t, n), BF16),
        compiler_params=_params(1),
        name="norm_proj",
    )(x, g, col_scale, w)


def _attn_in_kernel(x_ref, g_ref, pos_ref, tab_ref, w_ref, o_ref):
    xn = _rms(x_ref[...], g_ref[...]).astype(BF16)
    ang = pos_ref[...].astype(F32) * tab_ref[0:1, :]
    cos = jnp.cos(ang)
    sin = jnp.sin(ang)
    sin_lo = sin * tab_ref[1:2, :]
    sin_hi = sin * tab_ref[2:3, :]
    q_scale = LOG2E * ATTN_HEAD_DIM ** -0.5
    mem_scale = LOG2E * MEM_HEAD_DIM ** -0.5
    half = ROPE_DIM // 2
    rot_cols = Q_W + KV_DUP_W
    for c in range(0, ATTN_IN_COLS, 2 * LANES):
        acc = jnp.dot(xn, w_ref[:, c:c + 2 * LANES], preferred_element_type=F32)
        for s in range(2):
            col = c + s * LANES
            slab = acc[:, s * LANES:(s + 1) * LANES]
            if col < rot_cols:
                slab = (slab * cos + pltpu.roll(slab, LANES - half, 1) * sin_lo
                        + pltpu.roll(slab, half, 1) * sin_hi)
                if col < Q_W:
                    slab = slab * q_scale
            elif col >= Q_W + 2 * KV_DUP_W:
                slab = slab * mem_scale
            o_ref[:, col:col + LANES] = slab.astype(BF16)


def _attn_in(x, g, pos, tab, w, tile):
    t, d = x.shape
    n = w.shape[1]
    return pl.pallas_call(
        _attn_in_kernel,
        grid=(t // tile,),
        in_specs=[pl.BlockSpec((tile, d), lambda i: (i, 0)),
                  _const_spec((1, d)),
                  pl.BlockSpec((tile, 1), lambda i: (i, 0)),
                  _const_spec((SUBLANES, LANES)), _const_spec((d, n))],
        out_specs=pl.BlockSpec((tile, n), lambda i: (i, 0)),
        out_shape=jax.ShapeDtypeStruct((t, n), BF16),
        compiler_params=_params(1),
        name="attn_in",
    )(x, g, pos, tab, w)


def _mem_attention(mq_ref, mk_ref, mv_ref, mo_ref):
    rows = mq_ref.shape[0]
    for r0 in range(0, rows, BLOCK):
        for h in range(MEM_HEADS):
            sl = slice(h * MEM_HEAD_DIM, (h + 1) * MEM_HEAD_DIM)
            s = lax.dot_general(mq_ref[r0:r0 + BLOCK, sl], mk_ref[:, sl], (((1,), (1,)), ((), ())),
                                preferred_element_type=F32)
            m = jnp.max(s, axis=-1, keepdims=True)
            p = jnp.exp2(s - m)
            den = jnp.sum(p, axis=-1, keepdims=True)
            o = jnp.dot(p.astype(BF16), mv_ref[:, sl], preferred_element_type=F32)
            mo_ref[r0:r0 + BLOCK, sl] = (o / den).astype(BF16)


def _attn_kernel(sink_ref, q_ref, kp_ref, kc_ref, kn_ref, vp_ref, vc_ref, vn_ref,
                 mq_ref, mk_ref, mv_ref, o_ref, mo_ref):
    i = pl.program_id(1)
    nt = pl.num_programs(1)
    nq = q_ref.shape[0] // BLOCK
    row = lax.broadcasted_iota(jnp.int32, (BLOCK, BLOCK), 0)
    col = lax.broadcasted_iota(jnp.int32, (BLOCK, BLOCK), 1)
    band_prev = col >= row
    band_next = col <= row
    lo = lax.broadcasted_iota(jnp.int32, (BLOCK, LANES), 1) < ATTN_HEAD_DIM
    k_all = jnp.concatenate([kp_ref[...], kc_ref[...], kn_ref[...]], axis=0)
    v_all = jnp.concatenate([vp_ref[...], vc_ref[...], vn_ref[...]], axis=0)
    zero = jnp.zeros((), BF16)
    for j in range(nq):
        m_prev = band_prev if j > 0 else band_prev & (i > 0)
        m_next = band_next if j < nq - 1 else band_next & (i < nt - 1)
        r0 = j * BLOCK
        for h in range(ATTN_KV_HEADS):
            kh = k_all[r0:r0 + 3 * BLOCK, h * LANES:(h + 1) * LANES]
            vh = v_all[r0:r0 + 3 * BLOCK, h * LANES:(h + 1) * LANES]
            parts = []
            for g in range(ATTN_GROUP):
                slab = q_ref[r0:r0 + BLOCK, (2 * h + g // 2) * LANES:(2 * h + g // 2 + 1) * LANES]
                parts.append(jnp.where(lo if g % 2 == 0 else ~lo, slab, zero))
            qs = jnp.concatenate(parts, axis=0)
            s = lax.dot_general(qs, kh, (((1,), (1,)), ((), ())),
                                preferred_element_type=F32)
            probs, dens = [], []
            for g in range(ATTN_GROUP):
                sg = s[g * BLOCK:(g + 1) * BLOCK]
                s0 = jnp.where(m_prev, sg[:, 0:BLOCK], NEG)
                s1 = sg[:, BLOCK:2 * BLOCK]
                s2 = jnp.where(m_next, sg[:, 2 * BLOCK:3 * BLOCK], NEG)
                sink = sink_ref[ATTN_GROUP * h + g] * LOG2E
                m = jnp.max(jnp.maximum(jnp.maximum(s0, s1), s2), axis=-1, keepdims=True)
                m = jnp.maximum(m, sink)
                p0, p1, p2 = jnp.exp2(s0 - m), jnp.exp2(s1 - m), jnp.exp2(s2 - m)
                dens.append(jnp.sum(p0 + p1 + p2, axis=-1, keepdims=True) + jnp.exp2(sink - m))
                probs.append(jnp.concatenate([p0, p1, p2], axis=1).astype(BF16))
            o = jnp.dot(jnp.concatenate(probs, axis=0), vh, preferred_element_type=F32)
            outs = [o[g * BLOCK:(g + 1) * BLOCK] / dens[g] for g in range(ATTN_GROUP)]
            for pair in range(2):
                slab = jnp.where(lo, outs[2 * pair], outs[2 * pair + 1])
                o_ref[r0:r0 + BLOCK, (2 * h + pair) * LANES:(2 * h + pair + 1) * LANES] = slab.astype(BF16)
    _mem_attention(mq_ref, mk_ref, mv_ref, mo_ref)


def _attention(p, memkv, sinks, layer, batch, seq, tq):
    nt = seq // tq
    per = tq // BLOCK
    nb = seq // BLOCK
    p3 = p.reshape(batch, seq, ATTN_IN_COLS)
    kcol = Q_W // KV_DUP_W
    vcol = kcol + 1
    mqcol = (Q_W + 2 * KV_DUP_W) // MEM_W

    def kv_specs(colblk):
        prev = pl.BlockSpec((None, BLOCK, KV_DUP_W),
                            lambda b, i: (b, jnp.maximum(i * per - 1, 0), colblk))
        cur = pl.BlockSpec((None, tq, KV_DUP_W), lambda b, i: (b, i, colblk))
        nxt = pl.BlockSpec((None, BLOCK, KV_DUP_W),
                           lambda b, i: (b, jnp.minimum((i + 1) * per, nb - 1), colblk))
        return [prev, cur, nxt]

    out, mo = pl.pallas_call(
        _attn_kernel,
        grid=(batch, nt),
        in_specs=[pl.BlockSpec(memory_space=pltpu.SMEM),
                  pl.BlockSpec((None, tq, Q_W), lambda b, i: (b, i, 0))]
        + kv_specs(kcol) + kv_specs(vcol)
        + [pl.BlockSpec((None, tq, MEM_W), lambda b, i: (b, i, mqcol)),
           pl.BlockSpec((None, MEM_LEN, MEM_W), lambda b, i: (b, 0, 2 * layer)),
           pl.BlockSpec((None, MEM_LEN, MEM_W), lambda b, i: (b, 0, 2 * layer + 1))],
        out_specs=[pl.BlockSpec((None, tq, Q_W), lambda b, i: (b, i, 0)),
                   pl.BlockSpec((None, tq, MEM_W), lambda b, i: (b, i, 0))],
        out_shape=[jax.ShapeDtypeStruct((batch, seq, Q_W), BF16),
                   jax.ShapeDtypeStruct((batch, seq, MEM_W), BF16)],
        compiler_params=_params(2),
        name="attention",
    )(sinks, p3, p3, p3, p3, p3, p3, p3, p3, memkv, memkv)
    return out.reshape(batch * seq, Q_W), mo.reshape(batch * seq, MEM_W)


def _out_mlp_kernel(x_ref, y_ref, mo_ref, wo_ref, g_ref, wup_ref, wdn_ref, fg_ref, o_ref,
                    acc_ref, xn_ref, *, final):
    mix_w = y_ref.shape[-1]
    x1 = (x_ref[...]
          + jnp.dot(y_ref[...], wo_ref[0:mix_w, :], preferred_element_type=F32)
          + jnp.dot(mo_ref[...], wo_ref[mix_w:, :], preferred_element_type=F32))
    acc_ref[...] = x1
    xn_ref[...] = _rms(x1, g_ref[...]).astype(BF16)

    def ff_chunk(c, carry):
        c0 = pl.multiple_of(c * FF_CHUNK, FF_CHUNK)
        hid = jnp.dot(xn_ref[...], wup_ref[:, pl.ds(c0, FF_CHUNK)], preferred_element_type=F32)
        hid = jnp.maximum(hid, 0.0)
        hid = (hid * hid).astype(BF16)
        acc_ref[...] += jnp.dot(hid, wdn_ref[pl.ds(c0, FF_CHUNK), :], preferred_element_type=F32)
        return carry

    lax.fori_loop(0, D_FF // FF_CHUNK, ff_chunk, 0)
    acc = acc_ref[...]
    if final:
        acc = _rms(acc, fg_ref[...])
    o_ref[...] = acc


def _out_mlp(x, y, mo, wo, g, wup, wdn, fg, final, tile):
    t, d = x.shape
    return pl.pallas_call(
        functools.partial(_out_mlp_kernel, final=final),
        grid=(t // tile,),
        in_specs=[pl.BlockSpec((tile, d), lambda i: (i, 0)),
                  pl.BlockSpec((tile, y.shape[1]), lambda i: (i, 0)),
                  pl.BlockSpec((tile, mo.shape[1]), lambda i: (i, 0)),
                  _const_spec(wo.shape), _const_spec((1, d)),
                  _const_spec(wup.shape), _const_spec(wdn.shape), _const_spec((1, d))],
        out_specs=pl.BlockSpec((tile, d), lambda i: (i, 0)),
        out_shape=jax.ShapeDtypeStruct((t, d), F32),
        scratch_shapes=[pltpu.VMEM((tile, d), F32), pltpu.VMEM((tile, d), BF16)],
        compiler_params=_params(1),
        name="out_mlp",
    )(x, y, mo, wo, g, wup, wdn, fg)


def _permuted_conv(prev_ref, cur_ref, next_ref, perm_ref, cw_ref, cb_ref, ext_ref, first, last):
    ts, w = cur_ref.shape
    n = ts // SUBLANES
    xp = jnp.dot(perm_ref[...], cur_ref[...], preferred_element_type=F32).reshape(n, SUBLANES, w)
    ext_ref[1:n + 1] = xp
    sub = lax.broadcasted_iota(jnp.int32, (SUBLANES, w), 0)
    before = jnp.where(first, 0.0, prev_ref[BF16_ROWS - 1:BF16_ROWS, :].astype(F32))
    after = jnp.where(last, 0.0, next_ref[0:2, :].astype(F32))
    ext_ref[0] = jnp.where(sub == 0, before, pltpu.roll(xp[n - 1], 1, 0))
    ext_ref[n + 1] = jnp.where(sub == SUBLANES - 1, after[0:1], pltpu.roll(xp[0], SUBLANES - 1, 0))
    ext_ref[n + 2] = jnp.where(sub == SUBLANES - 1, after[1:2], pltpu.roll(xp[1], SUBLANES - 1, 0))
    xc = cb_ref[...][None]
    for tap in range(CONV_WIDTH):
        xc = xc + ext_ref[tap:tap + n] * cw_ref[tap:tap + 1, :][None]
    return xc.reshape(ts, w)


def _sigmoid(x):
    return 0.5 * jnp.tanh(0.5 * x) + 0.5


def _lru_tile(xc, wg_ref, ba_ref, bx_ref, lam_ref, a_ref, u_ref, h_ref, carry_ref, reverse):
    ts, w = xc.shape
    n = ts // SUBLANES
    xcb = xc.astype(BF16)
    lam = lam_ref[...]
    softplus_neg_lam = jnp.maximum(-lam, 0.0) + jnp.log1p(jnp.exp(-jnp.abs(lam)))
    for blk in range(LRU_BLOCKS):
        sl = slice(blk * LRU_BLOCK_DIM, (blk + 1) * LRU_BLOCK_DIM)
        gates = jnp.dot(xcb[:, sl], wg_ref[blk], preferred_element_type=F32)
        r = _sigmoid(gates[:, :LRU_BLOCK_DIM] + ba_ref[:, sl])
        i_gate = _sigmoid(gates[:, LRU_BLOCK_DIM:] + bx_ref[:, sl])
        a = jnp.exp((-LRU_C) * r * softplus_neg_lam[:, sl])
        a_ref[:, sl] = a
        z = 1.0 - a * a
        root = jnp.where(z > 0.0, z * lax.rsqrt(z), 0.0)
        u_ref[:, sl] = root * (i_gate * xc[:, sl])

    order = range(n - 1, -1, -1) if reverse else range(n)

    def rows(ref, j):
        return ref[j * SUBLANES:(j + 1) * SUBLANES, :]

    local = jnp.zeros((SUBLANES, w), F32)
    prod = jnp.ones((SUBLANES, w), F32)
    for j in order:
        a = rows(a_ref, j)
        local = a * local + rows(u_ref, j)
        prod = a * prod
    c = carry_ref[...]
    inflow = [None] * SUBLANES
    for s in (range(SUBLANES - 1, -1, -1) if reverse else range(SUBLANES)):
        inflow[s] = c
        c = prod[s:s + 1] * c + local[s:s + 1]
    carry_ref[...] = c
    h = jnp.concatenate(inflow, axis=0)
    for j in order:
        h = rows(a_ref, j) * h + rows(u_ref, j)
        h_ref[j * SUBLANES:(j + 1) * SUBLANES, :] = h


def _lru_fwd_kernel(xp_ref, xc_ref, xn_ref, mq_ref, mk_ref, mv_ref, perm_ref, cw_ref, cb_ref, wg_ref,
                    ba_ref, bx_ref, lam_ref, hf_ref, mo_ref, ext_ref, a_ref, u_ref, h_ref, carry_ref):
    i = pl.program_id(1)
    nt = pl.num_programs(1)

    @pl.when(i == 0)
    def _():
        carry_ref[...] = jnp.zeros_like(carry_ref)

    xc = _permuted_conv(xp_ref, xc_ref, xn_ref, perm_ref, cw_ref, cb_ref, ext_ref, i == 0, i == nt - 1)
    _lru_tile(xc, wg_ref, ba_ref, bx_ref, lam_ref, a_ref, u_ref, h_ref, carry_ref, reverse=False)
    hf_ref[...] = h_ref[...].astype(BF16)
    _mem_attention(mq_ref, mk_ref, mv_ref, mo_ref)


def _lru_rev_kernel(xp_ref, xc_ref, xn_ref, gate_ref, hf_ref, perm_ref, unperm_ref, cw_ref, cb_ref,
                    wg_ref, ba_ref, bx_ref, lam_ref, y_ref, ext_ref, a_ref, u_ref, h_ref, carry_ref):
    i = pl.program_id(1)
    nt = pl.num_programs(1)

    @pl.when(i == 0)
    def _():
        carry_ref[...] = jnp.zeros_like(carry_ref)

    xc = _permuted_conv(xp_ref, xc_ref, xn_ref, perm_ref, cw_ref, cb_ref, ext_ref, i == nt - 1, i == 0)
    _lru_tile(xc, wg_ref, ba_ref, bx_ref, lam_ref, a_ref, u_ref, h_ref, carry_ref, reverse=True)
    gate = jnp.dot(perm_ref[...], gate_ref[...], preferred_element_type=F32)
    y = (hf_ref[...].astype(F32) + h_ref[...]) * jax.nn.gelu(gate)
    y_ref[...] = jnp.dot(unperm_ref[...], y.astype(BF16), preferred_element_type=F32).astype(BF16)


def _lru_specs(ts, nt, seq, reverse):
    per = ts // BF16_ROWS
    last_halo = seq // BF16_ROWS - 1

    def t_idx(i):
        return (nt - 1 - i) if reverse else i

    prev = pl.BlockSpec((None, BF16_ROWS, LRU_WIDTH),
                        lambda b, i: (b, jnp.maximum(t_idx(i) * per - 1, 0), 0))
    cur = pl.BlockSpec((None, ts, LRU_WIDTH), lambda b, i: (b, t_idx(i), 0))
    nxt = pl.BlockSpec((None, BF16_ROWS, LRU_WIDTH),
                       lambda b, i: (b, jnp.minimum((t_idx(i) + 1) * per, last_halo), 0))
    return t_idx, prev, cur, nxt


def _lru_weight_specs():
    return [_const_spec((CONV_WIDTH, LRU_WIDTH)), _const_spec((1, LRU_WIDTH)),
            _const_spec((LRU_BLOCKS, LRU_BLOCK_DIM, 2 * LRU_BLOCK_DIM)),
            _const_spec((1, LRU_WIDTH)), _const_spec((1, LRU_WIDTH)), _const_spec((1, LRU_WIDTH))]


def _lru_scratch(ts):
    return [pltpu.VMEM((ts // SUBLANES + CONV_WIDTH - 1, SUBLANES, LRU_WIDTH), F32),
            pltpu.VMEM((ts, LRU_WIDTH), F32),
            pltpu.VMEM((ts, LRU_WIDTH), F32),
            pltpu.VMEM((ts, LRU_WIDTH), F32),
            pltpu.VMEM((1, LRU_WIDTH), F32)]


def _chunk_permutation(ts):
    n = ts // SUBLANES
    r = jnp.arange(ts)
    src = (r % SUBLANES) * n + r // SUBLANES
    return (src[:, None] == jnp.arange(ts)[None, :]).astype(BF16)


def _lru_forward(p3, memkv, layer, perm, weights, batch, seq, ts):
    nt = seq // ts
    t_idx, prev, cur, nxt = _lru_specs(ts, nt, seq, reverse=False)
    mqcol = (2 * LRU_WIDTH) // MEM_W
    return pl.pallas_call(
        _lru_fwd_kernel,
        grid=(batch, nt),
        in_specs=[prev, cur, nxt,
                  pl.BlockSpec((None, ts, MEM_W), lambda b, i: (b, i, mqcol)),
                  pl.BlockSpec((None, MEM_LEN, MEM_W), lambda b, i: (b, 0, 2 * layer)),
                  pl.BlockSpec((None, MEM_LEN, MEM_W), lambda b, i: (b, 0, 2 * layer + 1)),
                  _const_spec((ts, ts))]
        + _lru_weight_specs(),
        out_specs=[pl.BlockSpec((None, ts, LRU_WIDTH), lambda b, i: (b, i, 0)),
                   pl.BlockSpec((None, ts, MEM_W), lambda b, i: (b, i, 0))],
        out_shape=[jax.ShapeDtypeStruct((batch, seq, LRU_WIDTH), BF16),
                   jax.ShapeDtypeStruct((batch, seq, MEM_W), BF16)],
        scratch_shapes=_lru_scratch(ts),
        compiler_params=_params(2),
        name="lru_forward",
    )(p3, p3, p3, p3, memkv, memkv, perm, *weights)


def _lru_reverse(p3, hf, perm, weights, batch, seq, ts):
    nt = seq // ts
    t_idx, prev, cur, nxt = _lru_specs(ts, nt, seq, reverse=True)
    return pl.pallas_call(
        _lru_rev_kernel,
        grid=(batch, nt),
        in_specs=[prev, cur, nxt,
                  pl.BlockSpec((None, ts, LRU_WIDTH), lambda b, i: (b, t_idx(i), 1)),
                  pl.BlockSpec((None, ts, LRU_WIDTH), lambda b, i: (b, t_idx(i), 0)),
                  _const_spec((ts, ts)), _const_spec((ts, ts))]
        + _lru_weight_specs(),
        out_specs=pl.BlockSpec((None, ts, LRU_WIDTH), lambda b, i: (b, t_idx(i), 0)),
        out_shape=jax.ShapeDtypeStruct((batch, seq, LRU_WIDTH), BF16),
        scratch_shapes=_lru_scratch(ts),
        compiler_params=_params(2),
        name="lru_reverse",
    )(p3, p3, p3, p3, hf, perm, perm.T, *weights)


def _dup_heads(w):
    d = w.shape[0]
    w4 = w.reshape(d, ATTN_KV_HEADS, 1, ATTN_HEAD_DIM)
    return jnp.broadcast_to(w4, (d, ATTN_KV_HEADS, 2, ATTN_HEAD_DIM)).reshape(d, KV_DUP_W)


def _rotary_table():
    half = ROPE_DIM // 2
    inv_freq = ROPE_THETA ** (-2.0 * jnp.arange(half, dtype=F32) / ROPE_DIM)
    d = jnp.arange(LANES) % ATTN_HEAD_DIM
    freq = jnp.where(d < ROPE_DIM, inv_freq[d % half], 0.0)
    lo = jnp.where(d < half, -1.0, 0.0)
    hi = jnp.where((d >= half) & (d < ROPE_DIM), 1.0, 0.0)
    tab = jnp.zeros((SUBLANES, LANES), F32)
    return tab.at[0].set(freq).at[1].set(lo).at[2].set(hi)


def _gate_weights(wa, wx):
    return jnp.concatenate([wa, wx], axis=-1).astype(BF16)


def kernel(x, mem, positions, mix_norm, mlp_norm, mem_norm, final_norm, w_mem_kv, w_out, w_up, w_down,
           attn_w_in, attn_sinks, lru_w_in, lru_conv_w, lru_conv_b, lru_wa, lru_ba, lru_wx, lru_bx,
           lru_lambda):
    batch, seq, d = x.shape
    t = batch * seq
    depth = mix_norm.shape[0]
    xf = x.reshape(t, d)
    ones_d = jnp.ones((1, d), F32)

    w_kv_all = jnp.concatenate([w_mem_kv[l] for l in range(depth)], axis=1).astype(BF16)
    memkv = _norm_proj(mem.reshape(batch * MEM_LEN, d), mem_norm.reshape(1, d),
                       jnp.ones((1, w_kv_all.shape[1]), F32), w_kv_all,
                       tile=batch * MEM_LEN // 2)
    memkv = memkv.reshape(batch, MEM_LEN, depth * 2 * MEM_W)

    pos_col = positions.reshape(t, 1).astype(jnp.int32)
    rot_tab = _rotary_table()
    lru_col_scale = jnp.concatenate([jnp.ones((1, 2 * LRU_WIDTH), F32),
                                     jnp.full((1, MEM_W), LOG2E * MEM_HEAD_DIM ** -0.5, F32)], axis=1)

    for l in range(depth):
        kind, j = l % 2, l // 2
        g_mix = mix_norm[l].reshape(1, d)
        if kind == 0:
            w = attn_w_in[j]
            w_in = jnp.concatenate([w[:, :Q_W], _dup_heads(w[:, Q_W:Q_W + KV_W]),
                                    _dup_heads(w[:, Q_W + KV_W:Q_W + 2 * KV_W]),
                                    w[:, Q_W + 2 * KV_W:]], axis=1).astype(BF16)
            p = _attn_in(xf, g_mix, pos_col, rot_tab, w_in, ROW_TILE)
            y, mo = _attention(p, memkv, attn_sinks[j].astype(F32), l, batch, seq, ATTN_TILE)
        else:
            p = _norm_proj(xf, g_mix, lru_col_scale, lru_w_in[j].astype(BF16), ROW_TILE)
            p3 = p.reshape(batch, seq, LRU_IN_COLS)
            cw, cb = lru_conv_w[j], lru_conv_b[j].reshape(1, LRU_WIDTH)

            def dir_weights(dirn):
                return (cw, cb, _gate_weights(lru_wa[j, dirn], lru_wx[j, dirn]),
                        lru_ba[j, dirn].reshape(1, LRU_WIDTH), lru_bx[j, dirn].reshape(1, LRU_WIDTH),
                        lru_lambda[j, dirn].reshape(1, LRU_WIDTH))

            perm = _chunk_permutation(LRU_TILE)
            hf, mo = _lru_forward(p3, memkv, l, perm, dir_weights(0), batch, seq, LRU_TILE)
            y = _lru_reverse(p3, hf, perm, dir_weights(1), batch, seq, LRU_TILE)
            y = y.reshape(t, LRU_WIDTH)
            mo = mo.reshape(t, MEM_W)
        final = l == depth - 1
        xf = _out_mlp(xf, y, mo, w_out[l].astype(BF16), mlp_norm[l].reshape(1, d),
                      w_up[l].astype(BF16), w_down[l].astype(BF16),
                      final_norm.reshape(1, d) if final else ones_d, final, ROW_TILE)
    return xf.reshape(batch, seq, d)
```

```python
import functools

import jax
import jax.numpy as jnp
from jax import lax
from jax.experimental import pallas as pl
from jax.experimental.pallas import tpu as pltpu

F32 = jnp.float32
BF16 = jnp.bfloat16

D_MODEL = 1024
EPS = 1e-6
ATTN_HEADS = 16
ATTN_KV_HEADS = 4
ATTN_HEAD_DIM = 64
ATTN_GROUP = 4
WINDOW = 128
BLOCK = 128
ROPE_THETA = 500000.0
ROPE_DIM = 16
Q_W = 1024
KV_W = 256
MEM_LEN = 256
MEM_HEADS = 4
MEM_HEAD_DIM = 128
MEM_W = 512
LRU_WIDTH = 1024
LRU_BLOCKS = 8
LRU_BLOCK_DIM = 128
LRU_C = 8.0
CONV_WIDTH = 4
D_FF = 4096
NEG = -1e30

LANES = 128
SUBLANES = 8
BF16_ROWS = 16
KV_DUP_W = 2 * KV_W
ATTN_IN_COLS = Q_W + 2 * KV_DUP_W + MEM_W
LRU_IN_COLS = 2 * LRU_WIDTH + MEM_W
VMEM_LIMIT = 56 * 1024 * 1024

ROW_TILE = 512
LRU_TILE = 256
FF_CHUNK = 1024
ATTN_TILE = 512
LOG2E = 1.4426950408889634


def _const_spec(shape):
    nd = len(shape)
    return pl.BlockSpec(shape, lambda *_: (0,) * nd, pipeline_mode=pl.Buffered(1))


def _params(n_axes):
    return pltpu.CompilerParams(dimension_semantics=("arbitrary",) * n_axes,
                                vmem_limit_bytes=VMEM_LIMIT)


def _rms(x, g):
    ms = jnp.mean(x * x, axis=-1, keepdims=True)
    return x * lax.rsqrt(ms + EPS) * g


def _norm_proj_kernel(x_ref, g_ref, cs_ref, w_ref, o_ref, *, chunk):
    xn = _rms(x_ref[...], g_ref[...]).astype(BF16)
    for c in range(0, o_ref.shape[-1], chunk):
        acc = jnp.dot(xn, w_ref[:, c:c + chunk], preferred_element_type=F32)
        o_ref[:, c:c + chunk] = (acc * cs_ref[:, c:c + chunk]).astype(BF16)


def _norm_proj(x, g, col_scale, w, tile):
    t, d = x.shape
    n = w.shape[1]
    return pl.pallas_call(
        functools.partial(_norm_proj_kernel, chunk=512),
        grid=(t // tile,),
        in_specs=[pl.BlockSpec((tile, d), lambda i: (i, 0)),
                  _const_spec((1, d)), _const_spec((1, n)), _const_spec((d, n))],
        out_specs=pl.BlockSpec((tile, n), lambda i: (i, 0)),
        out_shape=jax.ShapeDtypeStruct((t, n), BF16),
        compiler_params=_params(1),
        name="norm_proj",
    )(x, g, col_scale, w)


def _attn_in_kernel(x_ref, g_ref, pos_ref, tab_ref, w_ref, o_ref):
    xn = _rms(x_ref[...], g_ref[...]).astype(BF16)
    ang = pos_ref[...].astype(F32) * tab_ref[0:1, :]
    cos = jnp.cos(ang)
    sin = jnp.sin(ang)
    sin_lo = sin * tab_ref[1:2, :]
    sin_hi = sin * tab_ref[2:3, :]
    q_scale = LOG2E * ATTN_HEAD_DIM ** -0.5
    mem_scale = LOG2E * MEM_HEAD_DIM ** -0.5
    half = ROPE_DIM // 2
    rot_cols = Q_W + KV_DUP_W
    for c in range(0, ATTN_IN_COLS, 2 * LANES):
        acc = jnp.dot(xn, w_ref[:, c:c + 2 * LANES], preferred_element_type=F32)
        for s in range(2):
            col = c + s * LANES
            slab = acc[:, s * LANES:(s + 1) * LANES]
            if col < rot_cols:
                slab = (slab * cos + pltpu.roll(slab, LANES - half, 1) * sin_lo
                        + pltpu.roll(slab, half, 1) * sin_hi)
                if col < Q_W:
                    slab = slab * q_scale
            elif col >= Q_W + 2 * KV_DUP_W:
                slab = slab * mem_scale
            o_ref[:, col:col + LANES] = slab.astype(BF16)


def _attn_in(x, g, pos, tab, w, tile):
    t, d = x.shape
    n = w.shape[1]
    return pl.pallas_call(
        _attn_in_kernel,
        grid=(t // tile,),
        in_specs=[pl.BlockSpec((tile, d), lambda i: (i, 0)),
                  _const_spec((1, d)),
                  pl.BlockSpec((tile, 1), lambda i: (i, 0)),
                  _const_spec((SUBLANES, LANES)), _const_spec((d, n))],
        out_specs=pl.BlockSpec((tile, n), lambda i: (i, 0)),
        out_shape=jax.ShapeDtypeStruct((t, n), BF16),
        compiler_params=_params(1),
        name="attn_in",
    )(x, g, pos, tab, w)


def _mem_attention(mq_ref, mk_ref, mv_ref, mo_ref):
    rows = mq_ref.shape[0]
    for r0 in range(0, rows, BLOCK):
        for h in range(MEM_HEADS):
            sl = slice(h * MEM_HEAD_DIM, (h + 1) * MEM_HEAD_DIM)
            s = lax.dot_general(mq_ref[r0:r0 + BLOCK, sl], mk_ref[:, sl], (((1,), (1,)), ((), ())),
                                preferred_element_type=F32)
            m = jnp.max(s, axis=-1, keepdims=True)
            p = jnp.exp2(s - m)
            den = jnp.sum(p, axis=-1, keepdims=True)
            o = jnp.dot(p.astype(BF16), mv_ref[:, sl], preferred_element_type=F32)
            mo_ref[r0:r0 + BLOCK, sl] = (o / den).astype(BF16)


def _attn_kernel(sink_ref, q_ref, kp_ref, kc_ref, kn_ref, vp_ref, vc_ref, vn_ref,
                 mq_ref, mk_ref, mv_ref, o_ref, mo_ref):
    i = pl.program_id(1)
    nt = pl.num_programs(1)
    nq = q_ref.shape[0] // BLOCK
    row = lax.broadcasted_iota(jnp.int32, (BLOCK, BLOCK), 0)
    col = lax.broadcasted_iota(jnp.int32, (BLOCK, BLOCK), 1)
    band_prev = col >= row
    band_next = col <= row
    lo = lax.broadcasted_iota(jnp.int32, (BLOCK, LANES), 1) < ATTN_HEAD_DIM
    k_all = jnp.concatenate([kp_ref[...], kc_ref[...], kn_ref[...]], axis=0)
    v_all = jnp.concatenate([vp_ref[...], vc_ref[...], vn_ref[...]], axis=0)
    zero = jnp.zeros((), BF16)
    for j in range(nq):
        m_prev = band_prev if j > 0 else band_prev & (i > 0)
        m_next = band_next if j < nq - 1 else band_next & (i < nt - 1)
        r0 = j * BLOCK
        for h in range(ATTN_KV_HEADS):
            kh = k_all[r0:r0 + 3 * BLOCK, h * LANES:(h + 1) * LANES]
            vh = v_all[r0:r0 + 3 * BLOCK, h * LANES:(h + 1) * LANES]
            parts = []
            for g in range(ATTN_GROUP):
                slab = q_ref[r0:r0 + BLOCK, (2 * h + g // 2) * LANES:(2 * h + g // 2 + 1) * LANES]
                parts.append(jnp.where(lo if g % 2 == 0 else ~lo, slab, zero))
            qs = jnp.concatenate(parts, axis=0)
            s = lax.dot_general(qs, kh, (((1,), (1,)), ((), ())),
                                preferred_element_type=F32)
            probs, dens = [], []
            for g in range(ATTN_GROUP):
                sg = s[g * BLOCK:(g + 1) * BLOCK]
                s0 = jnp.where(m_prev, sg[:, 0:BLOCK], NEG)
                s1 = sg[:, BLOCK:2 * BLOCK]
                s2 = jnp.where(m_next, sg[:, 2 * BLOCK:3 * BLOCK], NEG)
                sink = sink_ref[ATTN_GROUP * h + g] * LOG2E
                m = jnp.max(jnp.maximum(jnp.maximum(s0, s1), s2), axis=-1, keepdims=True)
                m = jnp.maximum(m, sink)
                p0, p1, p2 = jnp.exp2(s0 - m), jnp.exp2(s1 - m), jnp.exp2(s2 - m)
                dens.append(jnp.sum(p0 + p1 + p2, axis=-1, keepdims=True) + jnp.exp2(sink - m))
                probs.append(jnp.concatenate([p0, p1, p2], axis=1).astype(BF16))
            o = jnp.dot(jnp.concatenate(probs, axis=0), vh, preferred_element_type=F32)
            outs = [o[g * BLOCK:(g + 1) * BLOCK] / dens[g] for g in range(ATTN_GROUP)]
            for pair in range(2):
                slab = jnp.where(lo, outs[2 * pair], outs[2 * pair + 1])
                o_ref[r0:r0 + BLOCK, (2 * h + pair) * LANES:(2 * h + pair + 1) * LANES] = slab.astype(BF16)
    _mem_attention(mq_ref, mk_ref, mv_ref, mo_ref)


def _attention(p, memkv, sinks, layer, batch, seq, tq):
    nt = seq // tq
    per = tq // BLOCK
    nb = seq // BLOCK
    p3 = p.reshape(batch, seq, ATTN_IN_COLS)
    kcol = Q_W // KV_DUP_W
    vcol = kcol + 1
    mqcol = (Q_W + 2 * KV_DUP_W) // MEM_W

    def kv_specs(colblk):
        prev = pl.BlockSpec((None, BLOCK, KV_DUP_W),
                            lambda b, i: (b, jnp.maximum(i * per - 1, 0), colblk))
        cur = pl.BlockSpec((None, tq, KV_DUP_W), lambda b, i: (b, i, colblk))
        nxt = pl.BlockSpec((None, BLOCK, KV_DUP_W),
                           lambda b, i: (b, jnp.minimum((i + 1) * per, nb - 1), colblk))
        return [prev, cur, nxt]

    out, mo = pl.pallas_call(
        _attn_kernel,
        grid=(batch, nt),
        in_specs=[pl.BlockSpec(memory_space=pltpu.SMEM),
                  pl.BlockSpec((None, tq, Q_W), lambda b, i: (b, i, 0))]
        + kv_specs(kcol) + kv_specs(vcol)
        + [pl.BlockSpec((None, tq, MEM_W), lambda b, i: (b, i, mqcol)),
           pl.BlockSpec((None, MEM_LEN, MEM_W), lambda b, i: (b, 0, 2 * layer)),
           pl.BlockSpec((None, MEM_LEN, MEM_W), lambda b, i: (b, 0, 2 * layer + 1))],
        out_specs=[pl.BlockSpec((None, tq, Q_W), lambda b, i: (b, i, 0)),
                   pl.BlockSpec((None, tq, MEM_W), lambda b, i: (b, i, 0))],
        out_shape=[jax.ShapeDtypeStruct((batch, seq, Q_W), BF16),
                   jax.ShapeDtypeStruct((batch, seq, MEM_W), BF16)],
        compiler_params=_params(2),
        name="attention",
    )(sinks, p3, p3, p3, p3, p3, p3, p3, p3, memkv, memkv)
    return out.reshape(batch * seq, Q_W), mo.reshape(batch * seq, MEM_W)


def _mlp_stages(x_ref, y_ref, mo_ref, wo_ref, g_ref, wup_ref, wdn_ref, result):
    mix_w = y_ref.shape[-1]
    x1 = (x_ref[...] + jnp.dot(y_ref[...], wo_ref[0:mix_w, :], preferred_element_type=F32)
          + jnp.dot(mo_ref[...], wo_ref[mix_w:, :], preferred_element_type=F32))
    yield
    xn = _rms(x1, g_ref[...]).astype(BF16)
    hids = []
    for c in range(0, D_FF, FF_CHUNK):
        hid = jnp.dot(xn, wup_ref[:, c:c + FF_CHUNK], preferred_element_type=F32)
        hid = jnp.maximum(hid, 0.0)
        hids.append((hid * hid).astype(BF16))
        yield
    hid = jnp.concatenate(hids, axis=1)
    half = D_MODEL // 2
    lo = x1[:, :half] + jnp.dot(hid, wdn_ref[:, :half], preferred_element_type=F32)
    yield
    hi = x1[:, half:] + jnp.dot(hid, wdn_ref[:, half:], preferred_element_type=F32)
    result.append(jnp.concatenate([lo, hi], axis=1))
    yield


def _out_mlp_kernel(x_ref, y_ref, mo_ref, wo_ref, g_ref, wup_ref, wdn_ref, fg_ref, o_ref, *, final):
    result = []
    for _ in _mlp_stages(x_ref, y_ref, mo_ref, wo_ref, g_ref, wup_ref, wdn_ref, result):
        pass
    acc = result[0]
    if final:
        acc = _rms(acc, fg_ref[...])
    o_ref[...] = acc


def _out_mlp(x, y, mo, wo, g, wup, wdn, fg, final, tile):
    t, d = x.shape
    return pl.pallas_call(
        functools.partial(_out_mlp_kernel, final=final),
        grid=(t // tile,),
        in_specs=[pl.BlockSpec((tile, d), lambda i: (i, 0)),
                  pl.BlockSpec((tile, y.shape[1]), lambda i: (i, 0)),
                  pl.BlockSpec((tile, mo.shape[1]), lambda i: (i, 0)),
                  _const_spec(wo.shape), _const_spec((1, d)),
                  _const_spec(wup.shape), _const_spec(wdn.shape), _const_spec((1, d))],
        out_specs=pl.BlockSpec((tile, d), lambda i: (i, 0)),
        out_shape=jax.ShapeDtypeStruct((t, d), F32),
        compiler_params=_params(1),
        name="out_mlp",
    )(x, y, mo, wo, g, wup, wdn, fg)


def _permuted_conv(xp, before, after, cw_ref, cb_ref, ext_ref):
    ts, w = xp.shape
    n = ts // SUBLANES
    xp = xp.reshape(n, SUBLANES, w)
    ext_ref[1:n + 1] = xp
    sub = lax.broadcasted_iota(jnp.int32, (SUBLANES, w), 0)
    ext_ref[0] = jnp.where(sub == 0, before, pltpu.roll(xp[n - 1], 1, 0))
    ext_ref[n + 1] = jnp.where(sub == SUBLANES - 1, after[0:1], pltpu.roll(xp[0], SUBLANES - 1, 0))
    ext_ref[n + 2] = jnp.where(sub == SUBLANES - 1, after[1:2], pltpu.roll(xp[1], SUBLANES - 1, 0))
    xc = cb_ref[...][None]
    for tap in range(CONV_WIDTH):
        xc = xc + ext_ref[tap:tap + n] * cw_ref[tap:tap + 1, :][None]
    return xc.reshape(ts, w)


def _sigmoid(x):
    return 0.5 * jnp.tanh(0.5 * x) + 0.5


def _lru_scan(xc, gates, ba_ref, bx_ref, lam_ref, a_ref, u_ref, h_ref, carry_ref, reverse):
    ts, w = xc.shape
    n = ts // SUBLANES
    lam = lam_ref[...]
    softplus_neg_lam = jnp.maximum(-lam, 0.0) + jnp.log1p(jnp.exp(-jnp.abs(lam)))
    for blk in range(LRU_BLOCKS):
        sl = slice(blk * LRU_BLOCK_DIM, (blk + 1) * LRU_BLOCK_DIM)
        r = _sigmoid(gates[blk][:, :LRU_BLOCK_DIM] + ba_ref[:, sl])
        i_gate = _sigmoid(gates[blk][:, LRU_BLOCK_DIM:] + bx_ref[:, sl])
        a = jnp.exp((-LRU_C) * r * softplus_neg_lam[:, sl])
        a_ref[:, sl] = a
        z = 1.0 - a * a
        root = jnp.where(z > 0.0, z * lax.rsqrt(z), 0.0)
        u_ref[:, sl] = root * (i_gate * xc[:, sl])

    order = range(n - 1, -1, -1) if reverse else range(n)

    def rows(ref, j):
        return ref[j * SUBLANES:(j + 1) * SUBLANES, :]

    local = jnp.zeros((SUBLANES, w), F32)
    prod = jnp.ones((SUBLANES, w), F32)
    for j in order:
        a = rows(a_ref, j)
        local = a * local + rows(u_ref, j)
        prod = a * prod
    c = carry_ref[...]
    inflow = [None] * SUBLANES
    for s in (range(SUBLANES - 1, -1, -1) if reverse else range(SUBLANES)):
        inflow[s] = c
        c = prod[s:s + 1] * c + local[s:s + 1]
    carry_ref[...] = c
    h = jnp.concatenate(inflow, axis=0)
    for j in order:
        h = rows(a_ref, j) * h + rows(u_ref, j)
        h_ref[j * SUBLANES:(j + 1) * SUBLANES, :] = h


def _lru_stages(prev_ref, cur_ref, next_ref, first, last, lru_refs, scratch, reverse, pre, emit):
    perm_ref, cw_ref, cb_ref, wg_ref, ba_ref, bx_ref, lam_ref = lru_refs
    ext_ref, a_ref, u_ref, h_ref, carry_ref = scratch
    rows = cur_ref.shape[0]
    starts = list(range(0, rows, LRU_TILE))
    starts = starts[::-1] if reverse else starts
    permuted = [(jnp.dot(perm_ref[...], cur_ref[r0:r0 + LRU_TILE, :], preferred_element_type=F32), pre(r0))
                for r0 in starts]
    yield
    for r0, (xp, aux) in zip(starts, permuted):
        if r0 == 0:
            before = jnp.where(first, 0.0, prev_ref[BF16_ROWS - 1:BF16_ROWS, :].astype(F32))
        else:
            before = cur_ref[r0 - 1:r0, :].astype(F32)
        if r0 + LRU_TILE == rows:
            after = jnp.where(last, 0.0, next_ref[0:2, :].astype(F32))
        else:
            after = cur_ref[r0 + LRU_TILE:r0 + LRU_TILE + 2, :].astype(F32)
        xc = _permuted_conv(xp, before, after, cw_ref, cb_ref, ext_ref)
        xcb = xc.astype(BF16)
        gates = [jnp.dot(xcb[:, blk * LRU_BLOCK_DIM:(blk + 1) * LRU_BLOCK_DIM], wg_ref[blk],
                         preferred_element_type=F32) for blk in range(LRU_BLOCKS)]
        yield
        _lru_scan(xc, gates, ba_ref, bx_ref, lam_ref, a_ref, u_ref, h_ref, carry_ref, reverse)
        emit(r0, aux)
        yield


def _lru_fwd_kernel(xp_ref, xc_ref, xn_ref, mq_ref, mk_ref, mv_ref, perm_ref, cw_ref, cb_ref, wg_ref,
                    ba_ref, bx_ref, lam_ref, hf_ref, mo_ref, ext_ref, a_ref, u_ref, h_ref, carry_ref):
    i = pl.program_id(1)
    nt = pl.num_programs(1)

    @pl.when(i == 0)
    def _():
        carry_ref[...] = jnp.zeros_like(carry_ref)

    def emit(r0, aux):
        hf_ref[r0:r0 + LRU_TILE, :] = h_ref[...].astype(BF16)

    for _ in _lru_stages(xp_ref, xc_ref, xn_ref, i == 0, i == nt - 1,
                         (perm_ref, cw_ref, cb_ref, wg_ref, ba_ref, bx_ref, lam_ref),
                         (ext_ref, a_ref, u_ref, h_ref, carry_ref), False, lambda r0: None, emit):
        pass
    _mem_attention(mq_ref, mk_ref, mv_ref, mo_ref)


_FUSED_ORDER = "LMLMMMLLMMML"


def _lru_rev_mlp_kernel(x_ref, mo_ref, xp_ref, xc_ref, xn_ref, gate_ref, hf_ref,
                        perm_ref, unperm_ref, cw_ref, cb_ref, wg_ref, ba_ref, bx_ref, lam_ref,
                        wo_ref, g_ref, wup_ref, wdn_ref, fg_ref, o_ref,
                        ext_ref, a_ref, u_ref, h_ref, carry_ref, y_ref, *, blocks_per_row, final):
    k = pl.program_id(0)
    kk = jnp.minimum(k, pl.num_programs(0) - 2)
    t_blk = blocks_per_row - 1 - kk % blocks_per_row

    @pl.when(k == 0)
    def _():
        y_ref[...] = jnp.zeros_like(y_ref)

    @pl.when(kk % blocks_per_row == 0)
    def _():
        carry_ref[...] = jnp.zeros_like(carry_ref)

    def pre(r0):
        return jnp.dot(perm_ref[...], gate_ref[r0:r0 + LRU_TILE, :], preferred_element_type=F32)

    def emit(r0, gate):
        y = (hf_ref[r0:r0 + LRU_TILE, :].astype(F32) + h_ref[...]) * jax.nn.gelu(gate)
        y_ref[r0:r0 + LRU_TILE, :] = jnp.dot(unperm_ref[...], y.astype(BF16),
                                              preferred_element_type=F32).astype(BF16)

    result = []
    stages = {
        "M": _mlp_stages(x_ref, y_ref, mo_ref, wo_ref, g_ref, wup_ref, wdn_ref, result),
        "L": _lru_stages(xp_ref, xc_ref, xn_ref, t_blk == 0, t_blk == blocks_per_row - 1,
                         (perm_ref, cw_ref, cb_ref, wg_ref, ba_ref, bx_ref, lam_ref),
                         (ext_ref, a_ref, u_ref, h_ref, carry_ref), True, pre, emit),
    }
    for which in _FUSED_ORDER:
        next(stages[which])
    done = object()
    assert all(next(gen, done) is done for gen in stages.values())
    acc = result[0]
    if final:
        acc = _rms(acc, fg_ref[...])
    o_ref[...] = acc


def _lru_weight_specs():
    return [_const_spec((CONV_WIDTH, LRU_WIDTH)), _const_spec((1, LRU_WIDTH)),
            _const_spec((LRU_BLOCKS, LRU_BLOCK_DIM, 2 * LRU_BLOCK_DIM)),
            _const_spec((1, LRU_WIDTH)), _const_spec((1, LRU_WIDTH)), _const_spec((1, LRU_WIDTH))]


def _lru_scratch():
    ts = LRU_TILE
    return [pltpu.VMEM((ts // SUBLANES + CONV_WIDTH - 1, SUBLANES, LRU_WIDTH), F32),
            pltpu.VMEM((ts, LRU_WIDTH), F32),
            pltpu.VMEM((ts, LRU_WIDTH), F32),
            pltpu.VMEM((ts, LRU_WIDTH), F32),
            pltpu.VMEM((1, LRU_WIDTH), F32)]


def _chunk_permutation(ts):
    n = ts // SUBLANES
    r = jnp.arange(ts)
    src = (r % SUBLANES) * n + r // SUBLANES
    return (src[:, None] == jnp.arange(ts)[None, :]).astype(BF16)


def _lru_forward(p3, memkv, layer, perm, weights, batch, seq, blk):
    nt = seq // blk
    per = blk // BF16_ROWS
    last_halo = seq // BF16_ROWS - 1
    mqcol = (2 * LRU_WIDTH) // MEM_W
    return pl.pallas_call(
        _lru_fwd_kernel,
        grid=(batch, nt),
        in_specs=[pl.BlockSpec((None, BF16_ROWS, LRU_WIDTH),
                               lambda b, i: (b, jnp.maximum(i * per - 1, 0), 0)),
                  pl.BlockSpec((None, blk, LRU_WIDTH), lambda b, i: (b, i, 0)),
                  pl.BlockSpec((None, BF16_ROWS, LRU_WIDTH),
                               lambda b, i: (b, jnp.minimum((i + 1) * per, last_halo), 0)),
                  pl.BlockSpec((None, blk, MEM_W), lambda b, i: (b, i, mqcol)),
                  pl.BlockSpec((None, MEM_LEN, MEM_W), lambda b, i: (b, 0, 2 * layer)),
                  pl.BlockSpec((None, MEM_LEN, MEM_W), lambda b, i: (b, 0, 2 * layer + 1)),
                  _const_spec((LRU_TILE, LRU_TILE))]
        + _lru_weight_specs(),
        out_specs=[pl.BlockSpec((None, blk, LRU_WIDTH), lambda b, i: (b, i, 0)),
                   pl.BlockSpec((None, blk, MEM_W), lambda b, i: (b, i, 0))],
        out_shape=[jax.ShapeDtypeStruct((batch, seq, LRU_WIDTH), BF16),
                   jax.ShapeDtypeStruct((batch, seq, MEM_W), BF16)],
        scratch_shapes=_lru_scratch(),
        compiler_params=_params(2),
        name="lru_forward",
    )(p3, p3, p3, p3, memkv, memkv, perm, *weights)


def _lru_reverse_mlp(x3, mo3, p3, hf, perm, weights, wo, g, wup, wdn, fg, final, batch, seq, blk):
    nt = seq // blk
    n_blocks = batch * nt
    per = blk // BF16_ROWS
    last_halo = seq // BF16_ROWS - 1
    d = x3.shape[-1]

    def lru_bt(k):
        kk = jnp.minimum(k, n_blocks - 1)
        return kk // nt, nt - 1 - kk % nt

    def mlp_bt(k):
        return lru_bt(jnp.maximum(k - 1, 0))

    def at_mlp(col):
        return lambda k: (*mlp_bt(k), col)

    def at_lru(col):
        return lambda k: (*lru_bt(k), col)

    def prev_idx(k):
        b, t = lru_bt(k)
        return b, jnp.maximum(t * per - 1, 0), 0

    def next_idx(k):
        b, t = lru_bt(k)
        return b, jnp.minimum((t + 1) * per, last_halo), 0

    return pl.pallas_call(
        functools.partial(_lru_rev_mlp_kernel, blocks_per_row=nt, final=final),
        grid=(n_blocks + 1,),
        in_specs=[pl.BlockSpec((None, blk, d), at_mlp(0)),
                  pl.BlockSpec((None, blk, MEM_W), at_mlp(0)),
                  pl.BlockSpec((None, BF16_ROWS, LRU_WIDTH), prev_idx),
                  pl.BlockSpec((None, blk, LRU_WIDTH), at_lru(0)),
                  pl.BlockSpec((None, BF16_ROWS, LRU_WIDTH), next_idx),
                  pl.BlockSpec((None, blk, LRU_WIDTH), at_lru(1)),
                  pl.BlockSpec((None, blk, LRU_WIDTH), at_lru(0)),
                  _const_spec((LRU_TILE, LRU_TILE)), _const_spec((LRU_TILE, LRU_TILE))]
        + _lru_weight_specs()
        + [_const_spec(wo.shape), _const_spec((1, d)), _const_spec(wup.shape), _const_spec(wdn.shape),
           _const_spec((1, d))],
        out_specs=pl.BlockSpec((None, blk, d), at_mlp(0)),
        out_shape=jax.ShapeDtypeStruct(x3.shape, F32),
        scratch_shapes=_lru_scratch() + [pltpu.VMEM((blk, LRU_WIDTH), BF16)],
        compiler_params=_params(1),
        name="lru_reverse_mlp",
    )(x3, mo3, p3, p3, p3, p3, hf, perm, perm.T, *weights, wo, g, wup, wdn, fg)


def _dup_heads(w):
    d = w.shape[0]
    w4 = w.reshape(d, ATTN_KV_HEADS, 1, ATTN_HEAD_DIM)
    return jnp.broadcast_to(w4, (d, ATTN_KV_HEADS, 2, ATTN_HEAD_DIM)).reshape(d, KV_DUP_W)


def _rotary_table():
    half = ROPE_DIM // 2
    inv_freq = ROPE_THETA ** (-2.0 * jnp.arange(half, dtype=F32) / ROPE_DIM)
    d = jnp.arange(LANES) % ATTN_HEAD_DIM
    freq = jnp.where(d < ROPE_DIM, inv_freq[d % half], 0.0)
    lo = jnp.where(d < half, -1.0, 0.0)
    hi = jnp.where((d >= half) & (d < ROPE_DIM), 1.0, 0.0)
    tab = jnp.zeros((SUBLANES, LANES), F32)
    return tab.at[0].set(freq).at[1].set(lo).at[2].set(hi)


def _gate_weights(wa, wx):
    return jnp.concatenate([wa, wx], axis=-1).astype(BF16)


def kernel(x, mem, positions, mix_norm, mlp_norm, mem_norm, final_norm, w_mem_kv, w_out, w_up, w_down,
           attn_w_in, attn_sinks, lru_w_in, lru_conv_w, lru_conv_b, lru_wa, lru_ba, lru_wx, lru_bx,
           lru_lambda):
    batch, seq, d = x.shape
    t = batch * seq
    depth = mix_norm.shape[0]
    xf = x.reshape(t, d)
    ones_d = jnp.ones((1, d), F32)

    w_kv_all = jnp.concatenate([w_mem_kv[l] for l in range(depth)], axis=1).astype(BF16)
    memkv = _norm_proj(mem.reshape(batch * MEM_LEN, d), mem_norm.reshape(1, d),
                       jnp.ones((1, w_kv_all.shape[1]), F32), w_kv_all,
                       tile=batch * MEM_LEN // 2)
    memkv = memkv.reshape(batch, MEM_LEN, depth * 2 * MEM_W)

    pos_col = positions.reshape(t, 1).astype(jnp.int32)
    rot_tab = _rotary_table()
    lru_col_scale = jnp.concatenate([jnp.ones((1, 2 * LRU_WIDTH), F32),
                                     jnp.full((1, MEM_W), LOG2E * MEM_HEAD_DIM ** -0.5, F32)], axis=1)

    for l in range(depth):
        kind, j = l % 2, l // 2
        g_mix = mix_norm[l].reshape(1, d)
        final = l == depth - 1
        mlp_weights = (w_out[l].astype(BF16), mlp_norm[l].reshape(1, d), w_up[l].astype(BF16),
                       w_down[l].astype(BF16), final_norm.reshape(1, d) if final else ones_d)
        if kind == 0:
            w = attn_w_in[j]
            w_in = jnp.concatenate([w[:, :Q_W], _dup_heads(w[:, Q_W:Q_W + KV_W]),
                                    _dup_heads(w[:, Q_W + KV_W:Q_W + 2 * KV_W]),
                                    w[:, Q_W + 2 * KV_W:]], axis=1).astype(BF16)
            p = _attn_in(xf, g_mix, pos_col, rot_tab, w_in, ROW_TILE)
            y, mo = _attention(p, memkv, attn_sinks[j].astype(F32), l, batch, seq, ATTN_TILE)
            xf = _out_mlp(xf, y, mo, *mlp_weights, final, ROW_TILE)
        else:
            p = _norm_proj(xf, g_mix, lru_col_scale, lru_w_in[j].astype(BF16), ROW_TILE)
            p3 = p.reshape(batch, seq, LRU_IN_COLS)
            cw, cb = lru_conv_w[j], lru_conv_b[j].reshape(1, LRU_WIDTH)

            def dir_weights(dirn):
                return (cw, cb, _gate_weights(lru_wa[j, dirn], lru_wx[j, dirn]),
                        lru_ba[j, dirn].reshape(1, LRU_WIDTH), lru_bx[j, dirn].reshape(1, LRU_WIDTH),
                        lru_lambda[j, dirn].reshape(1, LRU_WIDTH))

            perm = _chunk_permutation(LRU_TILE)
            hf, mo = _lru_forward(p3, memkv, l, perm, dir_weights(0), batch, seq, ROW_TILE)
            xf = _lru_reverse_mlp(xf.reshape(batch, seq, d), mo, p3, hf, perm, dir_weights(1),
                                  *mlp_weights, final, batch, seq, ROW_TILE).reshape(t, d)
    return xf.reshape(batch, seq, d)
```

```python
import functools

import jax
import jax.numpy as jnp
from jax import lax
from jax.experimental import pallas as pl
from jax.experimental.pallas import tpu as pltpu

F32 = jnp.float32
BF16 = jnp.bfloat16

D_MODEL = 1024
EPS = 1e-6
ATTN_HEADS = 16
ATTN_KV_HEADS = 4
ATTN_HEAD_DIM = 64
ATTN_GROUP = 4
WINDOW = 128
BLOCK = 128
ROPE_THETA = 500000.0
ROPE_DIM = 16
Q_W = 1024
KV_W = 256
MEM_LEN = 256
MEM_HEADS = 4
MEM_HEAD_DIM = 128
MEM_W = 512
LRU_WIDTH = 1024
LRU_BLOCKS = 8
LRU_BLOCK_DIM = 128
LRU_C = 8.0
CONV_WIDTH = 4
D_FF = 4096
NEG = -1e30

LANES = 128
SUBLANES = 8
BF16_ROWS = 16
KV_DUP_W = 2 * KV_W
ATTN_IN_COLS = Q_W + 2 * KV_DUP_W + MEM_W
LRU_IN_COLS = 2 * LRU_WIDTH + MEM_W
VMEM_LIMIT = 56 * 1024 * 1024

ROW_TILE = 512
LRU_TILE = 256
FF_CHUNK = 1024
PROJ_CHUNK = 512
ATTN_TILE = 512
LOG2E = 1.4426950408889634
GELU_C0 = 0.7978845608028654
GELU_C1 = GELU_C0 * 0.044715


def _const_spec(shape):
    nd = len(shape)
    return pl.BlockSpec(shape, lambda *_: (0,) * nd, pipeline_mode=pl.Buffered(1))


def _params(n_axes):
    return pltpu.CompilerParams(dimension_semantics=("arbitrary",) * n_axes,
                                vmem_limit_bytes=VMEM_LIMIT)


def _rms(x, g):
    ms = jnp.mean(x * x, axis=-1, keepdims=True)
    return x * lax.rsqrt(ms + EPS) * g


def _norm_proj_kernel(x_ref, g_ref, cs_ref, w_ref, o_ref, *, chunk):
    xn = _rms(x_ref[...], g_ref[...]).astype(BF16)
    for c in range(0, o_ref.shape[-1], chunk):
        acc = jnp.dot(xn, w_ref[:, c:c + chunk], preferred_element_type=F32)
        o_ref[:, c:c + chunk] = (acc * cs_ref[:, c:c + chunk]).astype(BF16)


def _norm_proj(x, g, col_scale, w, tile):
    t, d = x.shape
    n = w.shape[1]
    return pl.pallas_call(
        functools.partial(_norm_proj_kernel, chunk=512),
        grid=(t // tile,),
        in_specs=[pl.BlockSpec((tile, d), lambda i: (i, 0)),
                  _const_spec((1, d)), _const_spec((1, n)), _const_spec((d, n))],
        out_specs=pl.BlockSpec((tile, n), lambda i: (i, 0)),
        out_shape=jax.ShapeDtypeStruct((t, n), BF16),
        compiler_params=_params(1),
        name="norm_proj",
    )(x, g, col_scale, w)


def _attn_in_kernel(x_ref, g_ref, pos_ref, tab_ref, w_ref, o_ref):
    xn = _rms(x_ref[...], g_ref[...]).astype(BF16)
    ang = pos_ref[...].astype(F32) * tab_ref[0:1, :]
    cos = jnp.cos(ang)
    sin = jnp.sin(ang)
    sin_lo = sin * tab_ref[1:2, :]
    sin_hi = sin * tab_ref[2:3, :]
    q_scale = LOG2E * ATTN_HEAD_DIM ** -0.5
    mem_scale = LOG2E * MEM_HEAD_DIM ** -0.5
    half = ROPE_DIM // 2
    rot_cols = Q_W + KV_DUP_W
    for c in range(0, ATTN_IN_COLS, 2 * LANES):
        acc = jnp.dot(xn, w_ref[:, c:c + 2 * LANES], preferred_element_type=F32)
        for s in range(2):
            col = c + s * LANES
            slab = acc[:, s * LANES:(s + 1) * LANES]
            if col < rot_cols:
                slab = (slab * cos + pltpu.roll(slab, LANES - half, 1) * sin_lo
                        + pltpu.roll(slab, half, 1) * sin_hi)
                if col < Q_W:
                    slab = slab * q_scale
            elif col >= Q_W + 2 * KV_DUP_W:
                slab = slab * mem_scale
            o_ref[:, col:col + LANES] = slab.astype(BF16)


def _attn_in(x, g, pos, tab, w, tile):
    t, d = x.shape
    n = w.shape[1]
    return pl.pallas_call(
        _attn_in_kernel,
        grid=(t // tile,),
        in_specs=[pl.BlockSpec((tile, d), lambda i: (i, 0)),
                  _const_spec((1, d)),
                  pl.BlockSpec((tile, 1), lambda i: (i, 0)),
                  _const_spec((SUBLANES, LANES)), _const_spec((d, n))],
        out_specs=pl.BlockSpec((tile, n), lambda i: (i, 0)),
        out_shape=jax.ShapeDtypeStruct((t, n), BF16),
        compiler_params=_params(1),
        name="attn_in",
    )(x, g, pos, tab, w)


def _score_ahead(items, scores, finish):
    pending = scores(items[0])
    for n, item in enumerate(items):
        current = pending
        if n + 1 < len(items):
            pending = scores(items[n + 1])
        finish(item, current)


def _mem_attention(mq_ref, mk_ref, mv_ref, mo_ref):
    rows = mq_ref.shape[0]

    def scores(item):
        r0, h = item
        sl = slice(h * MEM_HEAD_DIM, (h + 1) * MEM_HEAD_DIM)
        return lax.dot_general(mq_ref[r0:r0 + BLOCK, sl], mk_ref[:, sl], (((1,), (1,)), ((), ())),
                               preferred_element_type=F32)

    def finish(item, s):
        r0, h = item
        sl = slice(h * MEM_HEAD_DIM, (h + 1) * MEM_HEAD_DIM)
        m = jnp.max(s, axis=-1, keepdims=True)
        p = jnp.exp2(s - m)
        den = jnp.sum(p, axis=-1, keepdims=True)
        o = jnp.dot(p.astype(BF16), mv_ref[:, sl], preferred_element_type=F32)
        mo_ref[r0:r0 + BLOCK, sl] = (o / den).astype(BF16)

    _score_ahead([(r0, h) for r0 in range(0, rows, BLOCK) for h in range(MEM_HEADS)], scores, finish)


def _attn_kernel(sink_ref, q_ref, kp_ref, kc_ref, kn_ref, vp_ref, vc_ref, vn_ref,
                 mq_ref, mk_ref, mv_ref, o_ref, mo_ref):
    i = pl.program_id(1)
    nt = pl.num_programs(1)
    nq = q_ref.shape[0] // BLOCK
    key = lax.broadcasted_iota(jnp.int32, (BLOCK, ATTN_GROUP * BLOCK), 0)
    qry = lax.broadcasted_iota(jnp.int32, (BLOCK, ATTN_GROUP * BLOCK), 1) % BLOCK
    band_prev = key >= qry
    band_next = key <= qry
    lo = lax.broadcasted_iota(jnp.int32, (BLOCK, LANES), 1) < ATTN_HEAD_DIM
    top = lax.broadcasted_iota(jnp.int32, (LANES, BLOCK), 0) < ATTN_HEAD_DIM
    k_all = jnp.concatenate([kp_ref[...], kc_ref[...], kn_ref[...]], axis=0)
    v_all = jnp.concatenate([vp_ref[...], vc_ref[...], vn_ref[...]], axis=0)
    zero = jnp.zeros((), BF16)
    def scores(item):
        j, h = item
        r0 = j * BLOCK
        kh = k_all[r0:r0 + 3 * BLOCK, h * LANES:(h + 1) * LANES]
        parts = []
        for g in range(ATTN_GROUP):
            slab = q_ref[r0:r0 + BLOCK, (2 * h + g // 2) * LANES:(2 * h + g // 2 + 1) * LANES]
            parts.append(jnp.where(lo if g % 2 == 0 else ~lo, slab, zero))
        qs = jnp.concatenate(parts, axis=0)
        return lax.dot_general(kh, qs, (((1,), (1,)), ((), ())),
                               preferred_element_type=F32)

    def finish(item, st):
        j, h = item
        r0 = j * BLOCK
        m_prev = band_prev if j > 0 else band_prev & (i > 0)
        m_next = band_next if j < nq - 1 else band_next & (i < nt - 1)
        vh = v_all[r0:r0 + 3 * BLOCK, h * LANES:(h + 1) * LANES]
        s0 = jnp.where(m_prev, st[0:BLOCK], NEG)
        s1 = st[BLOCK:2 * BLOCK]
        s2 = jnp.where(m_next, st[2 * BLOCK:3 * BLOCK], NEG)
        sink = jnp.concatenate(
            [jnp.full((1, BLOCK), sink_ref[ATTN_GROUP * h + g] * LOG2E, F32) for g in range(ATTN_GROUP)],
            axis=1)
        m = jnp.max(jnp.maximum(jnp.maximum(s0, s1), s2), axis=0, keepdims=True)
        m = jnp.maximum(m, sink)
        p0, p1, p2 = jnp.exp2(s0 - m), jnp.exp2(s1 - m), jnp.exp2(s2 - m)
        den = jnp.sum(p0 + p1 + p2, axis=0, keepdims=True) + jnp.exp2(sink - m)
        pt = jnp.concatenate([p0, p1, p2], axis=0).astype(BF16)
        ot = lax.dot_general(vh, pt, (((0,), (0,)), ((), ())),
                             preferred_element_type=F32)
        ot = ot / den
        for pair in range(2):
            even = ot[:, (2 * pair) * BLOCK:(2 * pair + 1) * BLOCK]
            odd = ot[:, (2 * pair + 1) * BLOCK:(2 * pair + 2) * BLOCK]
            slab = jnp.where(top, even, odd).T
            o_ref[r0:r0 + BLOCK, (2 * h + pair) * LANES:(2 * h + pair + 1) * LANES] = slab.astype(BF16)

    _score_ahead([(j, h) for j in range(nq) for h in range(ATTN_KV_HEADS)], scores, finish)
    _mem_attention(mq_ref, mk_ref, mv_ref, mo_ref)


def _attention(p, memkv, sinks, layer, batch, seq, tq):
    nt = seq // tq
    per = tq // BLOCK
    nb = seq // BLOCK
    p3 = p.reshape(batch, seq, ATTN_IN_COLS)
    kcol = Q_W // KV_DUP_W
    vcol = kcol + 1
    mqcol = (Q_W + 2 * KV_DUP_W) // MEM_W

    def kv_specs(colblk):
        prev = pl.BlockSpec((None, BLOCK, KV_DUP_W),
                            lambda b, i: (b, jnp.maximum(i * per - 1, 0), colblk))
        cur = pl.BlockSpec((None, tq, KV_DUP_W), lambda b, i: (b, i, colblk))
        nxt = pl.BlockSpec((None, BLOCK, KV_DUP_W),
                           lambda b, i: (b, jnp.minimum((i + 1) * per, nb - 1), colblk))
        return [prev, cur, nxt]

    out, mo = pl.pallas_call(
        _attn_kernel,
        grid=(batch, nt),
        in_specs=[pl.BlockSpec(memory_space=pltpu.SMEM),
                  pl.BlockSpec((None, tq, Q_W), lambda b, i: (b, i, 0))]
        + kv_specs(kcol) + kv_specs(vcol)
        + [pl.BlockSpec((None, tq, MEM_W), lambda b, i: (b, i, mqcol)),
           pl.BlockSpec((None, MEM_LEN, MEM_W), lambda b, i: (b, 0, 2 * layer)),
           pl.BlockSpec((None, MEM_LEN, MEM_W), lambda b, i: (b, 0, 2 * layer + 1))],
        out_specs=[pl.BlockSpec((None, tq, Q_W), lambda b, i: (b, i, 0)),
                   pl.BlockSpec((None, tq, MEM_W), lambda b, i: (b, i, 0))],
        out_shape=[jax.ShapeDtypeStruct((batch, seq, Q_W), BF16),
                   jax.ShapeDtypeStruct((batch, seq, MEM_W), BF16)],
        compiler_params=_params(2),
        name="attention",
    )(sinks, p3, p3, p3, p3, p3, p3, p3, p3, memkv, memkv)
    return out.reshape(batch * seq, Q_W), mo.reshape(batch * seq, MEM_W)


def _mlp_stages(x_ref, y_ref, mo_ref, wo_ref, g_ref, wup_ref, wdn_ref, result):
    mix_w = y_ref.shape[-1]
    x1 = (x_ref[...] + jnp.dot(y_ref[...], wo_ref[0:mix_w, :], preferred_element_type=F32)
          + jnp.dot(mo_ref[...], wo_ref[mix_w:, :], preferred_element_type=F32))
    yield
    xn = _rms(x1, g_ref[...]).astype(BF16)
    hids = []
    for c in range(0, D_FF, FF_CHUNK):
        hid = jnp.dot(xn, wup_ref[:, c:c + FF_CHUNK], preferred_element_type=F32)
        hid = jnp.maximum(hid, 0.0)
        hids.append((hid * hid).astype(BF16))
        yield
    hid = jnp.concatenate(hids, axis=1)
    half = D_MODEL // 2
    lo = x1[:, :half] + jnp.dot(hid, wdn_ref[:, :half], preferred_element_type=F32)
    yield
    hi = x1[:, half:] + jnp.dot(hid, wdn_ref[:, half:], preferred_element_type=F32)
    result.append(jnp.concatenate([lo, hi], axis=1))
    yield


def _out_mlp_kernel(x_ref, y_ref, mo_ref, wo_ref, g_ref, wup_ref, wdn_ref, fg_ref, o_ref, *, final):
    result = []
    for _ in _mlp_stages(x_ref, y_ref, mo_ref, wo_ref, g_ref, wup_ref, wdn_ref, result):
        pass
    acc = result[0]
    if final:
        acc = _rms(acc, fg_ref[...])
    o_ref[...] = acc


def _out_mlp(x, y, mo, wo, g, wup, wdn, fg, final, tile):
    t, d = x.shape
    return pl.pallas_call(
        functools.partial(_out_mlp_kernel, final=final),
        grid=(t // tile,),
        in_specs=[pl.BlockSpec((tile, d), lambda i: (i, 0)),
                  pl.BlockSpec((tile, y.shape[1]), lambda i: (i, 0)),
                  pl.BlockSpec((tile, mo.shape[1]), lambda i: (i, 0)),
                  _const_spec(wo.shape), _const_spec((1, d)),
                  _const_spec(wup.shape), _const_spec(wdn.shape), _const_spec((1, d))],
        out_specs=pl.BlockSpec((tile, d), lambda i: (i, 0)),
        out_shape=jax.ShapeDtypeStruct((t, d), F32),
        compiler_params=_params(1),
        name="out_mlp",
    )(x, y, mo, wo, g, wup, wdn, fg)


def _permuted_conv(xp, before, after, cw_ref, cb_ref, ext_ref):
    ts, w = xp.shape
    n = ts // SUBLANES
    xp = xp.reshape(n, SUBLANES, w)
    ext_ref[1:n + 1] = xp
    sub = lax.broadcasted_iota(jnp.int32, (SUBLANES, w), 0)
    ext_ref[0] = jnp.where(sub == 0, before, pltpu.roll(xp[n - 1], 1, 0))
    ext_ref[n + 1] = jnp.where(sub == SUBLANES - 1, after[0:1], pltpu.roll(xp[0], SUBLANES - 1, 0))
    ext_ref[n + 2] = jnp.where(sub == SUBLANES - 1, after[1:2], pltpu.roll(xp[1], SUBLANES - 1, 0))
    xc = cb_ref[...][None]
    for tap in range(CONV_WIDTH):
        xc = xc + ext_ref[tap:tap + n] * cw_ref[tap:tap + 1, :][None]
    return xc.reshape(ts, w)


def _lru_scan(xc, gates, ba_ref, bx_ref, lam_ref, a_ref, u_ref, h_ref, carry_ref, reverse):
    ts, w = xc.shape
    n = ts // SUBLANES
    lam = lam_ref[...]
    softplus_neg_lam = jnp.maximum(-lam, 0.0) + jnp.log1p(jnp.exp(-jnp.abs(lam)))
    rate = (-0.5 * LRU_C * LOG2E) * softplus_neg_lam
    for blk in range(LRU_BLOCKS):
        sl = slice(blk * LRU_BLOCK_DIM, (blk + 1) * LRU_BLOCK_DIM)
        t_rec = jnp.tanh(gates[blk][:, :LRU_BLOCK_DIM] + ba_ref[:, sl])
        t_in = jnp.tanh(gates[blk][:, LRU_BLOCK_DIM:] + bx_ref[:, sl])
        a = jnp.exp2(rate[:, sl] * t_rec + rate[:, sl])
        a_ref[:, sl] = a
        z = 1.0 - a * a
        root = jnp.where(z > 0.0, z * lax.rsqrt(z), 0.0)
        u_ref[:, sl] = root * ((0.5 * t_in + 0.5) * xc[:, sl])

    order = range(n - 1, -1, -1) if reverse else range(n)

    def rows(ref, j):
        return ref[j * SUBLANES:(j + 1) * SUBLANES, :]

    local = jnp.zeros((SUBLANES, w), F32)
    prod = jnp.ones((SUBLANES, w), F32)
    for j in order:
        a = rows(a_ref, j)
        local = a * local + rows(u_ref, j)
        prod = a * prod
    c = carry_ref[...]
    inflow = [None] * SUBLANES
    for s in (range(SUBLANES - 1, -1, -1) if reverse else range(SUBLANES)):
        inflow[s] = c
        c = prod[s:s + 1] * c + local[s:s + 1]
    carry_ref[...] = c
    h = jnp.concatenate(inflow, axis=0)
    for j in order:
        h = rows(a_ref, j) * h + rows(u_ref, j)
        h_ref[j * SUBLANES:(j + 1) * SUBLANES, :] = h


def _lru_stages(prev_ref, cur_ref, next_ref, first, last, lru_refs, scratch, reverse, pre, emit):
    perm_ref, cw_ref, cb_ref, wg_ref, ba_ref, bx_ref, lam_ref = lru_refs
    ext_ref, a_ref, u_ref, h_ref, carry_ref = scratch
    rows = cur_ref.shape[0]
    starts = list(range(0, rows, LRU_TILE))
    starts = starts[::-1] if reverse else starts
    permuted = [(jnp.dot(perm_ref[...], cur_ref[r0:r0 + LRU_TILE, :], preferred_element_type=F32), pre(r0))
                for r0 in starts]
    yield
    for r0, (xp, aux) in zip(starts, permuted):
        if r0 == 0:
            before = jnp.where(first, 0.0, prev_ref[BF16_ROWS - 1:BF16_ROWS, :].astype(F32))
        else:
            before = cur_ref[r0 - 1:r0, :].astype(F32)
        if r0 + LRU_TILE == rows:
            after = jnp.where(last, 0.0, next_ref[0:2, :].astype(F32))
        else:
            after = cur_ref[r0 + LRU_TILE:r0 + LRU_TILE + 2, :].astype(F32)
        xc = _permuted_conv(xp, before, after, cw_ref, cb_ref, ext_ref)
        xcb = xc.astype(BF16)
        gates = [jnp.dot(xcb[:, blk * LRU_BLOCK_DIM:(blk + 1) * LRU_BLOCK_DIM], wg_ref[blk],
                         preferred_element_type=F32) for blk in range(LRU_BLOCKS)]
        yield
        _lru_scan(xc, gates, ba_ref, bx_ref, lam_ref, a_ref, u_ref, h_ref, carry_ref, reverse)
        emit(r0, aux)
        yield


def _lru_fwd_kernel(xp_ref, xc_ref, xn_ref, mq_ref, mk_ref, mv_ref, perm_ref, cw_ref, cb_ref, wg_ref,
                    ba_ref, bx_ref, lam_ref, hf_ref, mo_ref, ext_ref, a_ref, u_ref, h_ref, carry_ref):
    i = pl.program_id(1)
    nt = pl.num_programs(1)

    @pl.when(i == 0)
    def _():
        carry_ref[...] = jnp.zeros_like(carry_ref)

    def emit(r0, aux):
        hf_ref[r0:r0 + LRU_TILE, :] = h_ref[...].astype(BF16)

    for _ in _lru_stages(xp_ref, xc_ref, xn_ref, i == 0, i == nt - 1,
                         (perm_ref, cw_ref, cb_ref, wg_ref, ba_ref, bx_ref, lam_ref),
                         (ext_ref, a_ref, u_ref, h_ref, carry_ref), False, lambda r0: None, emit):
        pass
    _mem_attention(mq_ref, mk_ref, mv_ref, mo_ref)


_FUSED_ORDER = "LMLMMMLLMMML"


def _trace_interleaved(order, stages):
    for which in order:
        next(stages[which])
    done = object()
    assert all(next(gen, done) is done for gen in stages.values())


def _lru_rev_mlp_kernel(x_ref, mo_ref, xp_ref, xc_ref, xn_ref, gate_ref, hf_ref,
                        perm_ref, unperm_ref, cw_ref, cb_ref, wg_ref, ba_ref, bx_ref, lam_ref,
                        wo_ref, g_ref, wup_ref, wdn_ref, fg_ref, o_ref,
                        ext_ref, a_ref, u_ref, h_ref, carry_ref, y_ref, *, blocks_per_row, final):
    k = pl.program_id(0)
    kk = jnp.minimum(k, pl.num_programs(0) - 2)
    t_blk = blocks_per_row - 1 - kk % blocks_per_row

    @pl.when(k == 0)
    def _():
        y_ref[...] = jnp.zeros_like(y_ref)

    @pl.when(kk % blocks_per_row == 0)
    def _():
        carry_ref[...] = jnp.zeros_like(carry_ref)

    def pre(r0):
        return jnp.dot(perm_ref[...], gate_ref[r0:r0 + LRU_TILE, :], preferred_element_type=F32)

    def emit(r0, gate):
        inner = gate * (GELU_C0 + GELU_C1 * (gate * gate))
        hsum = hf_ref[r0:r0 + LRU_TILE, :].astype(F32) + h_ref[...]
        y = (hsum * gate) * (0.5 * jnp.tanh(inner) + 0.5)
        y_ref[r0:r0 + LRU_TILE, :] = jnp.dot(unperm_ref[...], y.astype(BF16),
                                              preferred_element_type=F32).astype(BF16)

    result = []
    _trace_interleaved(_FUSED_ORDER, {
        "M": _mlp_stages(x_ref, y_ref, mo_ref, wo_ref, g_ref, wup_ref, wdn_ref, result),
        "L": _lru_stages(xp_ref, xc_ref, xn_ref, t_blk == 0, t_blk == blocks_per_row - 1,
                         (perm_ref, cw_ref, cb_ref, wg_ref, ba_ref, bx_ref, lam_ref),
                         (ext_ref, a_ref, u_ref, h_ref, carry_ref), True, pre, emit),
    })
    acc = result[0]
    if final:
        acc = _rms(acc, fg_ref[...])
    o_ref[...] = acc


def _lru_weight_specs():
    return [_const_spec((CONV_WIDTH, LRU_WIDTH)), _const_spec((1, LRU_WIDTH)),
            _const_spec((LRU_BLOCKS, LRU_BLOCK_DIM, 2 * LRU_BLOCK_DIM)),
            _const_spec((1, LRU_WIDTH)), _const_spec((1, LRU_WIDTH)), _const_spec((1, LRU_WIDTH))]


def _lru_scratch():
    ts = LRU_TILE
    return [pltpu.VMEM((ts // SUBLANES + CONV_WIDTH - 1, SUBLANES, LRU_WIDTH), F32),
            pltpu.VMEM((ts, LRU_WIDTH), F32),
            pltpu.VMEM((ts, LRU_WIDTH), F32),
            pltpu.VMEM((ts, LRU_WIDTH), F32),
            pltpu.VMEM((1, LRU_WIDTH), F32)]


def _chunk_permutation(ts):
    n = ts // SUBLANES
    r = jnp.arange(ts)
    src = (r % SUBLANES) * n + r // SUBLANES
    return (src[:, None] == jnp.arange(ts)[None, :]).astype(BF16)


def _lru_forward(p3, memkv, layer, perm, weights, batch, seq, blk):
    nt = seq // blk
    per = blk // BF16_ROWS
    last_halo = seq // BF16_ROWS - 1
    mqcol = (2 * LRU_WIDTH) // MEM_W
    return pl.pallas_call(
        _lru_fwd_kernel,
        grid=(batch, nt),
        in_specs=[pl.BlockSpec((None, BF16_ROWS, LRU_WIDTH),
                               lambda b, i: (b, jnp.maximum(i * per - 1, 0), 0)),
                  pl.BlockSpec((None, blk, LRU_WIDTH), lambda b, i: (b, i, 0)),
                  pl.BlockSpec((None, BF16_ROWS, LRU_WIDTH),
                               lambda b, i: (b, jnp.minimum((i + 1) * per, last_halo), 0)),
                  pl.BlockSpec((None, blk, MEM_W), lambda b, i: (b, i, mqcol)),
                  pl.BlockSpec((None, MEM_LEN, MEM_W), lambda b, i: (b, 0, 2 * layer)),
                  pl.BlockSpec((None, MEM_LEN, MEM_W), lambda b, i: (b, 0, 2 * layer + 1)),
                  _const_spec((LRU_TILE, LRU_TILE))]
        + _lru_weight_specs(),
        out_specs=[pl.BlockSpec((None, blk, LRU_WIDTH), lambda b, i: (b, i, 0)),
                   pl.BlockSpec((None, blk, MEM_W), lambda b, i: (b, i, 0))],
        out_shape=[jax.ShapeDtypeStruct((batch, seq, LRU_WIDTH), BF16),
                   jax.ShapeDtypeStruct((batch, seq, MEM_W), BF16)],
        scratch_shapes=_lru_scratch(),
        compiler_params=_params(2),
        name="lru_forward",
    )(p3, p3, p3, p3, memkv, memkv, perm, *weights)


def _lru_reverse_mlp(x3, mo3, p3, hf, perm, weights, wo, g, wup, wdn, fg, final, batch, seq, blk):
    nt = seq // blk
    n_blocks = batch * nt
    per = blk // BF16_ROWS
    last_halo = seq // BF16_ROWS - 1
    d = x3.shape[-1]

    def lru_bt(k):
        kk = jnp.minimum(k, n_blocks - 1)
        return kk // nt, nt - 1 - kk % nt

    def mlp_bt(k):
        return lru_bt(jnp.maximum(k - 1, 0))

    def at_mlp(col):
        return lambda k: (*mlp_bt(k), col)

    def at_lru(col):
        return lambda k: (*lru_bt(k), col)

    def prev_idx(k):
        b, t = lru_bt(k)
        return b, jnp.maximum(t * per - 1, 0), 0

    def next_idx(k):
        b, t = lru_bt(k)
        return b, jnp.minimum((t + 1) * per, last_halo), 0

    return pl.pallas_call(
        functools.partial(_lru_rev_mlp_kernel, blocks_per_row=nt, final=final),
        grid=(n_blocks + 1,),
        in_specs=[pl.BlockSpec((None, blk, d), at_mlp(0)),
                  pl.BlockSpec((None, blk, MEM_W), at_mlp(0)),
                  pl.BlockSpec((None, BF16_ROWS, LRU_WIDTH), prev_idx),
                  pl.BlockSpec((None, blk, LRU_WIDTH), at_lru(0)),
                  pl.BlockSpec((None, BF16_ROWS, LRU_WIDTH), next_idx),
                  pl.BlockSpec((None, blk, LRU_WIDTH), at_lru(1)),
                  pl.BlockSpec((None, blk, LRU_WIDTH), at_lru(0)),
                  _const_spec((LRU_TILE, LRU_TILE)), _const_spec((LRU_TILE, LRU_TILE))]
        + _lru_weight_specs()
        + [_const_spec(wo.shape), _const_spec((1, d)), _const_spec(wup.shape), _const_spec(wdn.shape),
           _const_spec((1, d))],
        out_specs=pl.BlockSpec((None, blk, d), at_mlp(0)),
        out_shape=jax.ShapeDtypeStruct(x3.shape, F32),
        scratch_shapes=_lru_scratch() + [pltpu.VMEM((blk, LRU_WIDTH), BF16)],
        compiler_params=_params(1),
        name="lru_reverse_mlp",
    )(x3, mo3, p3, p3, p3, p3, hf, perm, perm.T, *weights, wo, g, wup, wdn, fg)


def _dup_heads(w):
    d = w.shape[0]
    w4 = w.reshape(d, ATTN_KV_HEADS, 1, ATTN_HEAD_DIM)
    return jnp.broadcast_to(w4, (d, ATTN_KV_HEADS, 2, ATTN_HEAD_DIM)).reshape(d, KV_DUP_W)


def _rotary_table():
    half = ROPE_DIM // 2
    inv_freq = ROPE_THETA ** (-2.0 * jnp.arange(half, dtype=F32) / ROPE_DIM)
    d = jnp.arange(LANES) % ATTN_HEAD_DIM
    freq = jnp.where(d < ROPE_DIM, inv_freq[d % half], 0.0)
    lo = jnp.where(d < half, -1.0, 0.0)
    hi = jnp.where((d >= half) & (d < ROPE_DIM), 1.0, 0.0)
    tab = jnp.zeros((SUBLANES, LANES), F32)
    return tab.at[0].set(freq).at[1].set(lo).at[2].set(hi)


def _gate_weights(wa, wx):
    return (0.5 * jnp.concatenate([wa, wx], axis=-1)).astype(BF16)


def kernel(x, mem, positions, mix_norm, mlp_norm, mem_norm, final_norm, w_mem_kv, w_out, w_up, w_down,
           attn_w_in, attn_sinks, lru_w_in, lru_conv_w, lru_conv_b, lru_wa, lru_ba, lru_wx, lru_bx,
           lru_lambda):
    batch, seq, d = x.shape
    t = batch * seq
    depth = mix_norm.shape[0]
    xf = x.reshape(t, d)
    ones_d = jnp.ones((1, d), F32)

    w_kv_all = jnp.concatenate([w_mem_kv[l] for l in range(depth)], axis=1).astype(BF16)
    memkv = _norm_proj(mem.reshape(batch * MEM_LEN, d), mem_norm.reshape(1, d),
                       jnp.ones((1, w_kv_all.shape[1]), F32), w_kv_all,
                       tile=batch * MEM_LEN // 2)
    memkv = memkv.reshape(batch, MEM_LEN, depth * 2 * MEM_W)

    pos_col = positions.reshape(t, 1).astype(jnp.int32)
    rot_tab = _rotary_table()
    lru_col_scale = jnp.concatenate([jnp.ones((1, 2 * LRU_WIDTH), F32),
                                     jnp.full((1, MEM_W), LOG2E * MEM_HEAD_DIM ** -0.5, F32)], axis=1)

    for l in range(depth):
        kind, j = l % 2, l // 2
        g_mix = mix_norm[l].reshape(1, d)
        final = l == depth - 1
        mlp_weights = (w_out[l].astype(BF16), mlp_norm[l].reshape(1, d), w_up[l].astype(BF16),
                       w_down[l].astype(BF16), final_norm.reshape(1, d) if final else ones_d)
        if kind == 0:
            w = attn_w_in[j]
            w_in = jnp.concatenate([w[:, :Q_W], _dup_heads(w[:, Q_W:Q_W + KV_W]),
                                    _dup_heads(w[:, Q_W + KV_W:Q_W + 2 * KV_W]),
                                    w[:, Q_W + 2 * KV_W:]], axis=1).astype(BF16)
            p = _attn_in(xf, g_mix, pos_col, rot_tab, w_in, ROW_TILE)
            y, mo = _attention(p, memkv, attn_sinks[j].astype(F32), l, batch, seq, ATTN_TILE)
            xf = _out_mlp(xf, y, mo, *mlp_weights, final, ROW_TILE)
        else:
            p = _norm_proj(xf, g_mix, lru_col_scale, lru_w_in[j].astype(BF16), ROW_TILE)
            p3 = p.reshape(batch, seq, LRU_IN_COLS)
            cw, cb = lru_conv_w[j], lru_conv_b[j].reshape(1, LRU_WIDTH)

            def dir_weights(dirn):
                return (cw, cb, _gate_weights(lru_wa[j, dirn], lru_wx[j, dirn]),
                        0.5 * lru_ba[j, dirn].reshape(1, LRU_WIDTH),
                        0.5 * lru_bx[j, dirn].reshape(1, LRU_WIDTH),
                        lru_lambda[j, dirn].reshape(1, LRU_WIDTH))

            perm = _chunk_permutation(LRU_TILE)
            hf, mo = _lru_forward(p3, memkv, l, perm, dir_weights(0), batch, seq, ROW_TILE)
            xf = _lru_reverse_mlp(xf.reshape(batch, seq, d), mo, p3, hf, perm, dir_weights(1),
                                  *mlp_weights, final, batch, seq, ROW_TILE).reshape(t, d)
    return xf.reshape(batch, seq, d)
```

```python
import functools

import jax
import jax.numpy as jnp
from jax import lax
from jax.experimental import pallas as pl
from jax.experimental.pallas import tpu as pltpu

F32 = jnp.float32
BF16 = jnp.bfloat16

D_MODEL = 1024
EPS = 1e-6
ATTN_HEADS = 16
ATTN_KV_HEADS = 4
ATTN_HEAD_DIM = 64
ATTN_GROUP = 4
WINDOW = 128
BLOCK = 128
ROPE_THETA = 500000.0
ROPE_DIM = 16
Q_W = 1024
KV_W = 256
MEM_LEN = 256
MEM_HEADS = 4
MEM_HEAD_DIM = 128
MEM_W = 512
LRU_WIDTH = 1024
LRU_BLOCKS = 8
LRU_BLOCK_DIM = 128
LRU_C = 8.0
CONV_WIDTH = 4
D_FF = 4096
NEG = -1e30

LANES = 128
SUBLANES = 8
BF16_ROWS = 16
KV_DUP_W = 2 * KV_W
ATTN_IN_COLS = Q_W + 2 * KV_DUP_W + MEM_W
LRU_IN_COLS = 2 * LRU_WIDTH + MEM_W
VMEM_LIMIT = 56 * 1024 * 1024

ROW_TILE = 512
LRU_TILE = 256
FF_CHUNK = 1024
PROJ_CHUNK = 512
ATTN_TILE = 512
LOG2E = 1.4426950408889634
GELU_C0 = 0.7978845608028654
GELU_C1 = GELU_C0 * 0.044715


def _const_spec(shape):
    nd = len(shape)
    return pl.BlockSpec(shape, lambda *_: (0,) * nd, pipeline_mode=pl.Buffered(1))


def _params(n_axes):
    return pltpu.CompilerParams(dimension_semantics=("arbitrary",) * n_axes,
                                vmem_limit_bytes=VMEM_LIMIT)


def _rms(x, g):
    ms = jnp.mean(x * x, axis=-1, keepdims=True)
    return x * lax.rsqrt(ms + EPS) * g


def _norm_proj_kernel(x_ref, g_ref, cs_ref, w_ref, o_ref, *, chunk):
    xn = _rms(x_ref[...], g_ref[...]).astype(BF16)
    for c in range(0, o_ref.shape[-1], chunk):
        acc = jnp.dot(xn, w_ref[:, c:c + chunk], preferred_element_type=F32)
        o_ref[:, c:c + chunk] = (acc * cs_ref[:, c:c + chunk]).astype(BF16)


def _norm_proj(x, g, col_scale, w, tile):
    t, d = x.shape
    n = w.shape[1]
    return pl.pallas_call(
        functools.partial(_norm_proj_kernel, chunk=512),
        grid=(t // tile,),
        in_specs=[pl.BlockSpec((tile, d), lambda i: (i, 0)),
                  _const_spec((1, d)), _const_spec((1, n)), _const_spec((d, n))],
        out_specs=pl.BlockSpec((tile, n), lambda i: (i, 0)),
        out_shape=jax.ShapeDtypeStruct((t, n), BF16),
        compiler_params=_params(1),
        name="norm_proj",
    )(x, g, col_scale, w)


def _attn_in_kernel(x_ref, g_ref, pos_ref, tab_ref, w_ref, o_ref):
    xn = _rms(x_ref[...], g_ref[...]).astype(BF16)
    ang = pos_ref[...].astype(F32) * tab_ref[0:1, :]
    cos = jnp.cos(ang)
    sin = jnp.sin(ang)
    sin_lo = sin * tab_ref[1:2, :]
    sin_hi = sin * tab_ref[2:3, :]
    q_scale = LOG2E * ATTN_HEAD_DIM ** -0.5
    mem_scale = LOG2E * MEM_HEAD_DIM ** -0.5
    half = ROPE_DIM // 2
    rot_cols = Q_W + KV_DUP_W
    for c in range(0, ATTN_IN_COLS, 2 * LANES):
        acc = jnp.dot(xn, w_ref[:, c:c + 2 * LANES], preferred_element_type=F32)
        for s in range(2):
            col = c + s * LANES
            slab = acc[:, s * LANES:(s + 1) * LANES]
            if col < rot_cols:
                slab = (slab * cos + pltpu.roll(slab, LANES - half, 1) * sin_lo
                        + pltpu.roll(slab, half, 1) * sin_hi)
                if col < Q_W:
                    slab = slab * q_scale
            elif col >= Q_W + 2 * KV_DUP_W:
                slab = slab * mem_scale
            o_ref[:, col:col + LANES] = slab.astype(BF16)


def _attn_in(x, g, pos, tab, w, tile):
    t, d = x.shape
    n = w.shape[1]
    return pl.pallas_call(
        _attn_in_kernel,
        grid=(t // tile,),
        in_specs=[pl.BlockSpec((tile, d), lambda i: (i, 0)),
                  _const_spec((1, d)),
                  pl.BlockSpec((tile, 1), lambda i: (i, 0)),
                  _const_spec((SUBLANES, LANES)), _const_spec((d, n))],
        out_specs=pl.BlockSpec((tile, n), lambda i: (i, 0)),
        out_shape=jax.ShapeDtypeStruct((t, n), BF16),
        compiler_params=_params(1),
        name="attn_in",
    )(x, g, pos, tab, w)


def _score_ahead(items, scores, finish):
    pending = scores(items[0])
    yield "S"
    for n, item in enumerate(items):
        current = pending
        if n + 1 < len(items):
            pending = scores(items[n + 1])
            yield "S"
        finish(item, current)
        yield "F"


def _mem_attention_stages(mq_ref, mk_ref, mv_ref, mo_ref):
    rows = mq_ref.shape[0]

    def scores(item):
        r0, h = item
        sl = slice(h * MEM_HEAD_DIM, (h + 1) * MEM_HEAD_DIM)
        return lax.dot_general(mq_ref[r0:r0 + BLOCK, sl], mk_ref[:, sl], (((1,), (1,)), ((), ())),
                               preferred_element_type=F32)

    def finish(item, s):
        r0, h = item
        sl = slice(h * MEM_HEAD_DIM, (h + 1) * MEM_HEAD_DIM)
        m = jnp.max(s, axis=-1, keepdims=True)
        p = jnp.exp2(s - m)
        den = jnp.sum(p, axis=-1, keepdims=True)
        o = jnp.dot(p.astype(BF16), mv_ref[:, sl], preferred_element_type=F32)
        mo_ref[r0:r0 + BLOCK, sl] = (o / den).astype(BF16)

    return _score_ahead([(r0, h) for r0 in range(0, rows, BLOCK) for h in range(MEM_HEADS)], scores, finish)


def _mem_attention(mq_ref, mk_ref, mv_ref, mo_ref):
    for _ in _mem_attention_stages(mq_ref, mk_ref, mv_ref, mo_ref):
        pass


def _window_attention_stages(sink_ref, q_ref, k_refs, v_refs, o_ref, has_prev, has_next):
    nq = q_ref.shape[0] // BLOCK
    key = lax.broadcasted_iota(jnp.int32, (BLOCK, ATTN_GROUP * BLOCK), 0)
    qry = lax.broadcasted_iota(jnp.int32, (BLOCK, ATTN_GROUP * BLOCK), 1) % BLOCK
    band_prev = key >= qry
    band_next = key <= qry
    lo = lax.broadcasted_iota(jnp.int32, (BLOCK, LANES), 1) < ATTN_HEAD_DIM
    top = lax.broadcasted_iota(jnp.int32, (LANES, BLOCK), 0) < ATTN_HEAD_DIM
    k_all = jnp.concatenate([r[...] for r in k_refs], axis=0)
    v_all = jnp.concatenate([r[...] for r in v_refs], axis=0)
    zero = jnp.zeros((), BF16)
    def scores(item):
        j, h = item
        r0 = j * BLOCK
        kh = k_all[r0:r0 + 3 * BLOCK, h * LANES:(h + 1) * LANES]
        parts = []
        for g in range(ATTN_GROUP):
            slab = q_ref[r0:r0 + BLOCK, (2 * h + g // 2) * LANES:(2 * h + g // 2 + 1) * LANES]
            parts.append(jnp.where(lo if g % 2 == 0 else ~lo, slab, zero))
        qs = jnp.concatenate(parts, axis=0)
        return lax.dot_general(kh, qs, (((1,), (1,)), ((), ())),
                               preferred_element_type=F32)

    def finish(item, st):
        j, h = item
        r0 = j * BLOCK
        m_prev = band_prev if j > 0 else band_prev & has_prev
        m_next = band_next if j < nq - 1 else band_next & has_next
        vh = v_all[r0:r0 + 3 * BLOCK, h * LANES:(h + 1) * LANES]
        s0 = jnp.where(m_prev, st[0:BLOCK], NEG)
        s1 = st[BLOCK:2 * BLOCK]
        s2 = jnp.where(m_next, st[2 * BLOCK:3 * BLOCK], NEG)
        sink = jnp.concatenate(
            [jnp.full((1, BLOCK), sink_ref[ATTN_GROUP * h + g] * LOG2E, F32) for g in range(ATTN_GROUP)],
            axis=1)
        m = jnp.max(jnp.maximum(jnp.maximum(s0, s1), s2), axis=0, keepdims=True)
        m = jnp.maximum(m, sink)
        p0, p1, p2 = jnp.exp2(s0 - m), jnp.exp2(s1 - m), jnp.exp2(s2 - m)
        den = jnp.sum(p0 + p1 + p2, axis=0, keepdims=True) + jnp.exp2(sink - m)
        pt = jnp.concatenate([p0, p1, p2], axis=0).astype(BF16)
        ot = lax.dot_general(vh, pt, (((0,), (0,)), ((), ())),
                             preferred_element_type=F32)
        ot = ot / den
        for pair in range(2):
            even = ot[:, (2 * pair) * BLOCK:(2 * pair + 1) * BLOCK]
            odd = ot[:, (2 * pair + 1) * BLOCK:(2 * pair + 2) * BLOCK]
            slab = jnp.where(top, even, odd).T
            o_ref[r0:r0 + BLOCK, (2 * h + pair) * LANES:(2 * h + pair + 1) * LANES] = slab.astype(BF16)

    return _score_ahead([(j, h) for j in range(nq) for h in range(ATTN_KV_HEADS)], scores, finish)


MLP_COARSE = (D_MODEL, FF_CHUNK, D_MODEL // 2)
MLP_FINE = (2 * LANES, 2 * LANES, 2 * LANES)


def _mlp_stages(x_ref, y_ref, mo_ref, wo_ref, g_ref, wup_ref, wdn_ref, result, widths=MLP_COARSE):
    proj_w, up_w, down_w = widths
    mix_w = y_ref.shape[-1]
    pieces = []
    for c in range(0, D_MODEL, proj_w):
        pieces.append(x_ref[:, c:c + proj_w]
                      + jnp.dot(y_ref[...], wo_ref[0:mix_w, c:c + proj_w], preferred_element_type=F32)
                      + jnp.dot(mo_ref[...], wo_ref[mix_w:, c:c + proj_w], preferred_element_type=F32))
        yield
    x1 = jnp.concatenate(pieces, axis=1)
    xn = _rms(x1, g_ref[...]).astype(BF16)
    hids = []
    for c in range(0, D_FF, up_w):
        hid = jnp.dot(xn, wup_ref[:, c:c + up_w], preferred_element_type=F32)
        hid = jnp.maximum(hid, 0.0)
        hids.append((hid * hid).astype(BF16))
        yield
    hid = jnp.concatenate(hids, axis=1)
    pieces = []
    for c in range(0, D_MODEL, down_w):
        pieces.append(x1[:, c:c + down_w]
                      + jnp.dot(hid, wdn_ref[:, c:c + down_w], preferred_element_type=F32))
        if c + down_w == D_MODEL:
            result.append(jnp.concatenate(pieces, axis=1))
        yield


def _attn_mlp_kernel(sink_ref, x_ref, q_ref, kp_ref, kc_ref, kn_ref, vp_ref, vc_ref, vn_ref,
                     mq_ref, mk_ref, mv_ref, wo_ref, g_ref, wup_ref, wdn_ref, fg_ref, o_ref,
                     y_ref, mo_ref, *, tiles_per_row, final):
    k = pl.program_id(0)
    kk = jnp.minimum(k, pl.num_programs(0) - 2)
    t_idx = kk % tiles_per_row

    @pl.when(k == 0)
    def _():
        y_ref[...] = jnp.zeros_like(y_ref)
        mo_ref[...] = jnp.zeros_like(mo_ref)

    def attention():
        yield from _window_attention_stages(sink_ref, q_ref, (kp_ref, kc_ref, kn_ref),
                                            (vp_ref, vc_ref, vn_ref), y_ref,
                                            t_idx > 0, t_idx < tiles_per_row - 1)
        yield from _mem_attention_stages(mq_ref, mk_ref, mv_ref, mo_ref)

    result = []
    attn = attention()
    mlp = _mlp_stages(x_ref, y_ref, mo_ref, wo_ref, g_ref, wup_ref, wdn_ref, result, MLP_FINE)
    next(attn)
    next(mlp)
    next(attn)
    for _ in range(D_MODEL // MLP_FINE[0] - 1):
        next(mlp)
    done = object()
    for tag in attn:
        if tag == "S":
            next(mlp, done)
    for _ in mlp:
        pass
    acc = result[0]
    if final:
        acc = _rms(acc, fg_ref[...])
    o_ref[...] = acc


def _attention_mlp(x3, p, memkv, sinks, layer, wo, g, wup, wdn, fg, final, batch, seq, tq):
    nt = seq // tq
    n_tiles = batch * nt
    per = tq // BLOCK
    nb = seq // BLOCK
    d = x3.shape[-1]
    p3 = p.reshape(batch, seq, ATTN_IN_COLS)
    kcol = Q_W // KV_DUP_W
    vcol = kcol + 1
    mqcol = (Q_W + 2 * KV_DUP_W) // MEM_W

    def attn_bt(k):
        kk = jnp.minimum(k, n_tiles - 1)
        return kk // nt, kk % nt

    def mlp_bt(k):
        return attn_bt(jnp.maximum(k - 1, 0))

    def kv_specs(colblk):
        def prev(k):
            b, i = attn_bt(k)
            return b, jnp.maximum(i * per - 1, 0), colblk

        def nxt(k):
            b, i = attn_bt(k)
            return b, jnp.minimum((i + 1) * per, nb - 1), colblk

        return [pl.BlockSpec((None, BLOCK, KV_DUP_W), prev),
                pl.BlockSpec((None, tq, KV_DUP_W), lambda k: (*attn_bt(k), colblk)),
                pl.BlockSpec((None, BLOCK, KV_DUP_W), nxt)]

    return pl.pallas_call(
        functools.partial(_attn_mlp_kernel, tiles_per_row=nt, final=final),
        grid=(n_tiles + 1,),
        in_specs=[pl.BlockSpec(memory_space=pltpu.SMEM),
                  pl.BlockSpec((None, tq, d), lambda k: (*mlp_bt(k), 0)),
                  pl.BlockSpec((None, tq, Q_W), lambda k: (*attn_bt(k), 0))]
        + kv_specs(kcol) + kv_specs(vcol)
        + [pl.BlockSpec((None, tq, MEM_W), lambda k: (*attn_bt(k), mqcol)),
           pl.BlockSpec((None, MEM_LEN, MEM_W), lambda k: (attn_bt(k)[0], 0, 2 * layer)),
           pl.BlockSpec((None, MEM_LEN, MEM_W), lambda k: (attn_bt(k)[0], 0, 2 * layer + 1)),
           _const_spec(wo.shape), _const_spec((1, d)), _const_spec(wup.shape), _const_spec(wdn.shape),
           _const_spec((1, d))],
        out_specs=pl.BlockSpec((None, tq, d), lambda k: (*mlp_bt(k), 0)),
        out_shape=jax.ShapeDtypeStruct(x3.shape, F32),
        scratch_shapes=[pltpu.VMEM((tq, Q_W), BF16), pltpu.VMEM((tq, MEM_W), BF16)],
        compiler_params=_params(1),
        name="attention_mlp",
    )(sinks, x3, p3, p3, p3, p3, p3, p3, p3, p3, memkv, memkv, wo, g, wup, wdn, fg)


def _permuted_conv(xp, before, after, cw_ref, cb_ref, ext_ref):
    ts, w = xp.shape
    n = ts // SUBLANES
    xp = xp.reshape(n, SUBLANES, w)
    ext_ref[1:n + 1] = xp
    sub = lax.broadcasted_iota(jnp.int32, (SUBLANES, w), 0)
    ext_ref[0] = jnp.where(sub == 0, before, pltpu.roll(xp[n - 1], 1, 0))
    ext_ref[n + 1] = jnp.where(sub == SUBLANES - 1, after[0:1], pltpu.roll(xp[0], SUBLANES - 1, 0))
    ext_ref[n + 2] = jnp.where(sub == SUBLANES - 1, after[1:2], pltpu.roll(xp[1], SUBLANES - 1, 0))
    xc = cb_ref[...][None]
    for tap in range(CONV_WIDTH):
        xc = xc + ext_ref[tap:tap + n] * cw_ref[tap:tap + 1, :][None]
    return xc.reshape(ts, w)


def _lru_scan(xc, gates, ba_ref, bx_ref, lam_ref, a_ref, u_ref, h_ref, carry_ref, reverse):
    ts, w = xc.shape
    n = ts // SUBLANES
    lam = lam_ref[...]
    softplus_neg_lam = jnp.maximum(-lam, 0.0) + jnp.log1p(jnp.exp(-jnp.abs(lam)))
    rate = (-0.5 * LRU_C * LOG2E) * softplus_neg_lam
    for blk in range(LRU_BLOCKS):
        sl = slice(blk * LRU_BLOCK_DIM, (blk + 1) * LRU_BLOCK_DIM)
        t_rec = jnp.tanh(gates[blk][:, :LRU_BLOCK_DIM] + ba_ref[:, sl])
        t_in = jnp.tanh(gates[blk][:, LRU_BLOCK_DIM:] + bx_ref[:, sl])
        a = jnp.exp2(rate[:, sl] * t_rec + rate[:, sl])
        a_ref[:, sl] = a
        z = 1.0 - a * a
        root = jnp.where(z > 0.0, z * lax.rsqrt(z), 0.0)
        u_ref[:, sl] = root * ((0.5 * t_in + 0.5) * xc[:, sl])

    order = range(n - 1, -1, -1) if reverse else range(n)

    def rows(ref, j):
        return ref[j * SUBLANES:(j + 1) * SUBLANES, :]

    local = jnp.zeros((SUBLANES, w), F32)
    prod = jnp.ones((SUBLANES, w), F32)
    for j in order:
        a = rows(a_ref, j)
        local = a * local + rows(u_ref, j)
        prod = a * prod
    c = carry_ref[...]
    inflow = [None] * SUBLANES
    for s in (range(SUBLANES - 1, -1, -1) if reverse else range(SUBLANES)):
        inflow[s] = c
        c = prod[s:s + 1] * c + local[s:s + 1]
    carry_ref[...] = c
    h = jnp.concatenate(inflow, axis=0)
    for j in order:
        h = rows(a_ref, j) * h + rows(u_ref, j)
        h_ref[j * SUBLANES:(j + 1) * SUBLANES, :] = h


def _lru_stages(prev_ref, cur_ref, next_ref, first, last, lru_refs, scratch, reverse, pre, emit):
    perm_ref, cw_ref, cb_ref, wg_ref, ba_ref, bx_ref, lam_ref = lru_refs
    ext_ref, a_ref, u_ref, h_ref, carry_ref = scratch
    rows = cur_ref.shape[0]
    starts = list(range(0, rows, LRU_TILE))
    starts = starts[::-1] if reverse else starts
    permuted = [(jnp.dot(perm_ref[...], cur_ref[r0:r0 + LRU_TILE, :], preferred_element_type=F32), pre(r0))
                for r0 in starts]
    yield
    for r0, (xp, aux) in zip(starts, permuted):
        if r0 == 0:
            before = jnp.where(first, 0.0, prev_ref[BF16_ROWS - 1:BF16_ROWS, :].astype(F32))
        else:
            before = cur_ref[r0 - 1:r0, :].astype(F32)
        if r0 + LRU_TILE == rows:
            after = jnp.where(last, 0.0, next_ref[0:2, :].astype(F32))
        else:
            after = cur_ref[r0 + LRU_TILE:r0 + LRU_TILE + 2, :].astype(F32)
        xc = _permuted_conv(xp, before, after, cw_ref, cb_ref, ext_ref)
        xcb = xc.astype(BF16)
        gates = [jnp.dot(xcb[:, blk * LRU_BLOCK_DIM:(blk + 1) * LRU_BLOCK_DIM], wg_ref[blk],
                         preferred_element_type=F32) for blk in range(LRU_BLOCKS)]
        yield
        _lru_scan(xc, gates, ba_ref, bx_ref, lam_ref, a_ref, u_ref, h_ref, carry_ref, reverse)
        emit(r0, aux)
        yield


def _lru_fwd_kernel(xp_ref, xc_ref, xn_ref, mq_ref, mk_ref, mv_ref, perm_ref, cw_ref, cb_ref, wg_ref,
                    ba_ref, bx_ref, lam_ref, hf_ref, mo_ref, ext_ref, a_ref, u_ref, h_ref, carry_ref):
    i = pl.program_id(1)
    nt = pl.num_programs(1)

    @pl.when(i == 0)
    def _():
        carry_ref[...] = jnp.zeros_like(carry_ref)

    def emit(r0, aux):
        hf_ref[r0:r0 + LRU_TILE, :] = h_ref[...].astype(BF16)

    for _ in _lru_stages(xp_ref, xc_ref, xn_ref, i == 0, i == nt - 1,
                         (perm_ref, cw_ref, cb_ref, wg_ref, ba_ref, bx_ref, lam_ref),
                         (ext_ref, a_ref, u_ref, h_ref, carry_ref), False, lambda r0: None, emit):
        pass
    _mem_attention(mq_ref, mk_ref, mv_ref, mo_ref)


_FUSED_ORDER = "LMLMMMLLMMML"


def _trace_interleaved(order, stages):
    for which in order:
        next(stages[which])
    done = object()
    assert all(next(gen, done) is done for gen in stages.values())


def _lru_rev_mlp_kernel(x_ref, mo_ref, xp_ref, xc_ref, xn_ref, gate_ref, hf_ref,
                        perm_ref, unperm_ref, cw_ref, cb_ref, wg_ref, ba_ref, bx_ref, lam_ref,
                        wo_ref, g_ref, wup_ref, wdn_ref, fg_ref, o_ref,
                        ext_ref, a_ref, u_ref, h_ref, carry_ref, y_ref, *, blocks_per_row, final):
    k = pl.program_id(0)
    kk = jnp.minimum(k, pl.num_programs(0) - 2)
    t_blk = blocks_per_row - 1 - kk % blocks_per_row

    @pl.when(k == 0)
    def _():
        y_ref[...] = jnp.zeros_like(y_ref)

    @pl.when(kk % blocks_per_row == 0)
    def _():
        carry_ref[...] = jnp.zeros_like(carry_ref)

    def pre(r0):
        return jnp.dot(perm_ref[...], gate_ref[r0:r0 + LRU_TILE, :], preferred_element_type=F32)

    def emit(r0, gate):
        inner = gate * (GELU_C0 + GELU_C1 * (gate * gate))
        hsum = hf_ref[r0:r0 + LRU_TILE, :].astype(F32) + h_ref[...]
        y = (hsum * gate) * (0.5 * jnp.tanh(inner) + 0.5)
        y_ref[r0:r0 + LRU_TILE, :] = jnp.dot(unperm_ref[...], y.astype(BF16),
                                              preferred_element_type=F32).astype(BF16)

    result = []
    _trace_interleaved(_FUSED_ORDER, {
        "M": _mlp_stages(x_ref, y_ref, mo_ref, wo_ref, g_ref, wup_ref, wdn_ref, result),
        "L": _lru_stages(xp_ref, xc_ref, xn_ref, t_blk == 0, t_blk == blocks_per_row - 1,
                         (perm_ref, cw_ref, cb_ref, wg_ref, ba_ref, bx_ref, lam_ref),
                         (ext_ref, a_ref, u_ref, h_ref, carry_ref), True, pre, emit),
    })
    acc = result[0]
    if final:
        acc = _rms(acc, fg_ref[...])
    o_ref[...] = acc


def _lru_weight_specs():
    return [_const_spec((CONV_WIDTH, LRU_WIDTH)), _const_spec((1, LRU_WIDTH)),
            _const_spec((LRU_BLOCKS, LRU_BLOCK_DIM, 2 * LRU_BLOCK_DIM)),
            _const_spec((1, LRU_WIDTH)), _const_spec((1, LRU_WIDTH)), _const_spec((1, LRU_WIDTH))]


def _lru_scratch():
    ts = LRU_TILE
    return [pltpu.VMEM((ts // SUBLANES + CONV_WIDTH - 1, SUBLANES, LRU_WIDTH), F32),
            pltpu.VMEM((ts, LRU_WIDTH), F32),
            pltpu.VMEM((ts, LRU_WIDTH), F32),
            pltpu.VMEM((ts, LRU_WIDTH), F32),
            pltpu.VMEM((1, LRU_WIDTH), F32)]


def _chunk_permutation(ts):
    n = ts // SUBLANES
    r = jnp.arange(ts)
    src = (r % SUBLANES) * n + r // SUBLANES
    return (src[:, None] == jnp.arange(ts)[None, :]).astype(BF16)


def _lru_forward(p3, memkv, layer, perm, weights, batch, seq, blk):
    nt = seq // blk
    per = blk // BF16_ROWS
    last_halo = seq // BF16_ROWS - 1
    mqcol = (2 * LRU_WIDTH) // MEM_W
    return pl.pallas_call(
        _lru_fwd_kernel,
        grid=(batch, nt),
        in_specs=[pl.BlockSpec((None, BF16_ROWS, LRU_WIDTH),
                               lambda b, i: (b, jnp.maximum(i * per - 1, 0), 0)),
                  pl.BlockSpec((None, blk, LRU_WIDTH), lambda b, i: (b, i, 0)),
                  pl.BlockSpec((None, BF16_ROWS, LRU_WIDTH),
                               lambda b, i: (b, jnp.minimum((i + 1) * per, last_halo), 0)),
                  pl.BlockSpec((None, blk, MEM_W), lambda b, i: (b, i, mqcol)),
                  pl.BlockSpec((None, MEM_LEN, MEM_W), lambda b, i: (b, 0, 2 * layer)),
                  pl.BlockSpec((None, MEM_LEN, MEM_W), lambda b, i: (b, 0, 2 * layer + 1)),
                  _const_spec((LRU_TILE, LRU_TILE))]
        + _lru_weight_specs(),
        out_specs=[pl.BlockSpec((None, blk, LRU_WIDTH), lambda b, i: (b, i, 0)),
                   pl.BlockSpec((None, blk, MEM_W), lambda b, i: (b, i, 0))],
        out_shape=[jax.ShapeDtypeStruct((batch, seq, LRU_WIDTH), BF16),
                   jax.ShapeDtypeStruct((batch, seq, MEM_W), BF16)],
        scratch_shapes=_lru_scratch(),
        compiler_params=_params(2),
        name="lru_forward",
    )(p3, p3, p3, p3, memkv, memkv, perm, *weights)


def _lru_reverse_mlp(x3, mo3, p3, hf, perm, weights, wo, g, wup, wdn, fg, final, batch, seq, blk):
    nt = seq // blk
    n_blocks = batch * nt
    per = blk // BF16_ROWS
    last_halo = seq // BF16_ROWS - 1
    d = x3.shape[-1]

    def lru_bt(k):
        kk = jnp.minimum(k, n_blocks - 1)
        return kk // nt, nt - 1 - kk % nt

    def mlp_bt(k):
        return lru_bt(jnp.maximum(k - 1, 0))

    def at_mlp(col):
        return lambda k: (*mlp_bt(k), col)

    def at_lru(col):
        return lambda k: (*lru_bt(k), col)

    def prev_idx(k):
        b, t = lru_bt(k)
        return b, jnp.maximum(t * per - 1, 0), 0

    def next_idx(k):
        b, t = lru_bt(k)
        return b, jnp.minimum((t + 1) * per, last_halo), 0

    return pl.pallas_call(
        functools.partial(_lru_rev_mlp_kernel, blocks_per_row=nt, final=final),
        grid=(n_blocks + 1,),
        in_specs=[pl.BlockSpec((None, blk, d), at_mlp(0)),
                  pl.BlockSpec((None, blk, MEM_W), at_mlp(0)),
                  pl.BlockSpec((None, BF16_ROWS, LRU_WIDTH), prev_idx),
                  pl.BlockSpec((None, blk, LRU_WIDTH), at_lru(0)),
                  pl.BlockSpec((None, BF16_ROWS, LRU_WIDTH), next_idx),
                  pl.BlockSpec((None, blk, LRU_WIDTH), at_lru(1)),
                  pl.BlockSpec((None, blk, LRU_WIDTH), at_lru(0)),
                  _const_spec((LRU_TILE, LRU_TILE)), _const_spec((LRU_TILE, LRU_TILE))]
        + _lru_weight_specs()
        + [_const_spec(wo.shape), _const_spec((1, d)), _const_spec(wup.shape), _const_spec(wdn.shape),
           _const_spec((1, d))],
        out_specs=pl.BlockSpec((None, blk, d), at_mlp(0)),
        out_shape=jax.ShapeDtypeStruct(x3.shape, F32),
        scratch_shapes=_lru_scratch() + [pltpu.VMEM((blk, LRU_WIDTH), BF16)],
        compiler_params=_params(1),
        name="lru_reverse_mlp",
    )(x3, mo3, p3, p3, p3, p3, hf, perm, perm.T, *weights, wo, g, wup, wdn, fg)


def _dup_heads(w):
    d = w.shape[0]
    w4 = w.reshape(d, ATTN_KV_HEADS, 1, ATTN_HEAD_DIM)
    return jnp.broadcast_to(w4, (d, ATTN_KV_HEADS, 2, ATTN_HEAD_DIM)).reshape(d, KV_DUP_W)


def _rotary_table():
    half = ROPE_DIM // 2
    inv_freq = ROPE_THETA ** (-2.0 * jnp.arange(half, dtype=F32) / ROPE_DIM)
    d = jnp.arange(LANES) % ATTN_HEAD_DIM
    freq = jnp.where(d < ROPE_DIM, inv_freq[d % half], 0.0)
    lo = jnp.where(d < half, -1.0, 0.0)
    hi = jnp.where((d >= half) & (d < ROPE_DIM), 1.0, 0.0)
    tab = jnp.zeros((SUBLANES, LANES), F32)
    return tab.at[0].set(freq).at[1].set(lo).at[2].set(hi)


def _gate_weights(wa, wx):
    return (0.5 * jnp.concatenate([wa, wx], axis=-1)).astype(BF16)


def kernel(x, mem, positions, mix_norm, mlp_norm, mem_norm, final_norm, w_mem_kv, w_out, w_up, w_down,
           attn_w_in, attn_sinks, lru_w_in, lru_conv_w, lru_conv_b, lru_wa, lru_ba, lru_wx, lru_bx,
           lru_lambda):
    batch, seq, d = x.shape
    t = batch * seq
    depth = mix_norm.shape[0]
    xf = x.reshape(t, d)
    ones_d = jnp.ones((1, d), F32)

    w_kv_all = jnp.concatenate([w_mem_kv[l] for l in range(depth)], axis=1).astype(BF16)
    memkv = _norm_proj(mem.reshape(batch * MEM_LEN, d), mem_norm.reshape(1, d),
                       jnp.ones((1, w_kv_all.shape[1]), F32), w_kv_all,
                       tile=batch * MEM_LEN // 2)
    memkv = memkv.reshape(batch, MEM_LEN, depth * 2 * MEM_W)

    pos_col = positions.reshape(t, 1).astype(jnp.int32)
    rot_tab = _rotary_table()
    lru_col_scale = jnp.concatenate([jnp.ones((1, 2 * LRU_WIDTH), F32),
                                     jnp.full((1, MEM_W), LOG2E * MEM_HEAD_DIM ** -0.5, F32)], axis=1)

    for l in range(depth):
        kind, j = l % 2, l // 2
        g_mix = mix_norm[l].reshape(1, d)
        final = l == depth - 1
        mlp_weights = (w_out[l].astype(BF16), mlp_norm[l].reshape(1, d), w_up[l].astype(BF16),
                       w_down[l].astype(BF16), final_norm.reshape(1, d) if final else ones_d)
        if kind == 0:
            w = attn_w_in[j]
            w_in = jnp.concatenate([w[:, :Q_W], _dup_heads(w[:, Q_W:Q_W + KV_W]),
                                    _dup_heads(w[:, Q_W + KV_W:Q_W + 2 * KV_W]),
                                    w[:, Q_W + 2 * KV_W:]], axis=1).astype(BF16)
            p = _attn_in(xf, g_mix, pos_col, rot_tab, w_in, ROW_TILE)
            xf = _attention_mlp(xf.reshape(batch, seq, d), p, memkv, attn_sinks[j].astype(F32), l,
                                *mlp_weights, final, batch, seq, ATTN_TILE).reshape(t, d)
        else:
            p = _norm_proj(xf, g_mix, lru_col_scale, lru_w_in[j].astype(BF16), ROW_TILE)
            p3 = p.reshape(batch, seq, LRU_IN_COLS)
            cw, cb = lru_conv_w[j], lru_conv_b[j].reshape(1, LRU_WIDTH)

            def dir_weights(dirn):
                return (cw, cb, _gate_weights(lru_wa[j, dirn], lru_wx[j, dirn]),
                        0.5 * lru_ba[j, dirn].reshape(1, LRU_WIDTH),
                        0.5 * lru_bx[j, dirn].reshape(1, LRU_WIDTH),
                        lru_lambda[j, dirn].reshape(1, LRU_WIDTH))

            perm = _chunk_permutation(LRU_TILE)
            hf, mo = _lru_forward(p3, memkv, l, perm, dir_weights(0), batch, seq, ROW_TILE)
            xf = _lru_reverse_mlp(xf.reshape(batch, seq, d), mo, p3, hf, perm, dir_weights(1),
                                  *mlp_weights, final, batch, seq, ROW_TILE).reshape(t, d)
    return xf.reshape(batch, seq, d)
```

```python
import functools

import jax
import jax.numpy as jnp
from jax import lax
from jax.experimental import pallas as pl
from jax.experimental.pallas import tpu as pltpu

F32 = jnp.float32
BF16 = jnp.bfloat16

D_MODEL = 1024
EPS = 1e-6
ATTN_HEADS = 16
ATTN_KV_HEADS = 4
ATTN_HEAD_DIM = 64
ATTN_GROUP = 4
WINDOW = 128
BLOCK = 128
ROPE_THETA = 500000.0
ROPE_DIM = 16
Q_W = 1024
KV_W = 256
MEM_LEN = 256
MEM_HEADS = 4
MEM_HEAD_DIM = 128
MEM_W = 512
LRU_WIDTH = 1024
LRU_BLOCKS = 8
LRU_BLOCK_DIM = 128
LRU_C = 8.0
CONV_WIDTH = 4
D_FF = 4096
NEG = -1e30

LANES = 128
SUBLANES = 8
BF16_ROWS = 16
KV_DUP_W = 2 * KV_W
ATTN_IN_COLS = Q_W + 2 * KV_DUP_W + MEM_W
LRU_IN_COLS = 2 * LRU_WIDTH + MEM_W
VMEM_LIMIT = 56 * 1024 * 1024

ROW_TILE = 512
LRU_TILE = 256
FF_CHUNK = 1024
PROJ_CHUNK = 512
PROJ_TILE = 1024
ATTN_TILE = 512
LOG2E = 1.4426950408889634
GELU_C0 = 0.7978845608028654
GELU_C1 = GELU_C0 * 0.044715


def _const_spec(shape):
    nd = len(shape)
    return pl.BlockSpec(shape, lambda *_: (0,) * nd, pipeline_mode=pl.Buffered(1))


def _params(n_axes):
    return pltpu.CompilerParams(dimension_semantics=("arbitrary",) * n_axes,
                                vmem_limit_bytes=VMEM_LIMIT)


def _rms(x, g):
    ms = jnp.mean(x * x, axis=-1, keepdims=True)
    return x * lax.rsqrt(ms + EPS) * g


def _norm_proj_kernel(x_ref, g_ref, cs_ref, w_ref, o_ref, *, chunk):
    xn = _rms(x_ref[...], g_ref[...]).astype(BF16)
    for c in range(0, o_ref.shape[-1], chunk):
        acc = jnp.dot(xn, w_ref[:, c:c + chunk], preferred_element_type=F32)
        o_ref[:, c:c + chunk] = (acc * cs_ref[:, c:c + chunk]).astype(BF16)


def _norm_proj(x, g, col_scale, w, tile):
    t, d = x.shape
    n = w.shape[1]
    return pl.pallas_call(
        functools.partial(_norm_proj_kernel, chunk=PROJ_CHUNK),
        grid=(t // tile,),
        in_specs=[pl.BlockSpec((tile, d), lambda i: (i, 0)),
                  _const_spec((1, d)), _const_spec((1, n)), _const_spec((d, n))],
        out_specs=pl.BlockSpec((tile, n), lambda i: (i, 0)),
        out_shape=jax.ShapeDtypeStruct((t, n), BF16),
        compiler_params=_params(1),
        name="norm_proj",
    )(x, g, col_scale, w)


def _split_bf16(x):
    hi = x.astype(BF16).astype(F32)
    return hi, x - hi


def _attn_in_kernel(x_ref, g_ref, pos_ref, freq_ref, expand_ref, w_ref, o_ref):
    xn = _rms(x_ref[...], g_ref[...]).astype(BF16)
    ang = freq_ref[...] * pos_ref[...].astype(F32)
    cos_hi, cos_lo = _split_bf16(jnp.cos(ang))
    sin_hi_, sin_lo_ = _split_bf16(jnp.sin(ang))
    one = jnp.where(lax.broadcasted_iota(jnp.int32, (BF16_ROWS, ang.shape[1]), 0) == 0, 1.0, 0.0)
    trig = jnp.concatenate([cos_hi, cos_lo, sin_hi_, sin_lo_, one], axis=0).astype(BF16)
    tables = lax.dot_general(trig, expand_ref[...], (((0,), (0,)), ((), ())),
                             preferred_element_type=F32)
    cos = tables[:, 0:LANES]
    sin_lo = tables[:, LANES:2 * LANES]
    sin_hi = tables[:, 2 * LANES:3 * LANES]
    q_scale = LOG2E * ATTN_HEAD_DIM ** -0.5
    mem_scale = LOG2E * MEM_HEAD_DIM ** -0.5
    half = ROPE_DIM // 2
    rot_cols = Q_W + KV_DUP_W
    for c in range(0, ATTN_IN_COLS, 2 * LANES):
        acc = jnp.dot(xn, w_ref[:, c:c + 2 * LANES], preferred_element_type=F32)
        for s in range(2):
            col = c + s * LANES
            slab = acc[:, s * LANES:(s + 1) * LANES]
            if col < rot_cols:
                slab = (slab * cos + pltpu.roll(slab, LANES - half, 1) * sin_lo
                        + pltpu.roll(slab, half, 1) * sin_hi)
                if col < Q_W:
                    slab = slab * q_scale
            elif col >= Q_W + 2 * KV_DUP_W:
                slab = slab * mem_scale
            o_ref[:, col:col + LANES] = slab.astype(BF16)


def _attn_in(x, g, pos, freq, expand, w, tile):
    t, d = x.shape
    n = w.shape[1]
    return pl.pallas_call(
        _attn_in_kernel,
        grid=(t // tile,),
        in_specs=[pl.BlockSpec((tile, d), lambda i: (i, 0)),
                  _const_spec((1, d)),
                  pl.BlockSpec((None, 1, tile), lambda i: (i, 0, 0)),
                  _const_spec(freq.shape), _const_spec(expand.shape), _const_spec((d, n))],
        out_specs=pl.BlockSpec((tile, n), lambda i: (i, 0)),
        out_shape=jax.ShapeDtypeStruct((t, n), BF16),
        compiler_params=_params(1),
        name="attn_in",
    )(x, g, pos, freq, expand, w)


def _score_ahead(items, scores, finish):
    pending = scores(items[0])
    yield "S"
    for n, item in enumerate(items):
        current = pending
        if n + 1 < len(items):
            pending = scores(items[n + 1])
            yield "S"
        finish(item, current)
        yield "F"


def _mem_attention_stages(mq_ref, mk_ref, mv_ref, mo_ref):
    rows = mq_ref.shape[0]

    def scores(item):
        r0, h = item
        sl = slice(h * MEM_HEAD_DIM, (h + 1) * MEM_HEAD_DIM)
        return lax.dot_general(mq_ref[r0:r0 + BLOCK, sl], mk_ref[:, sl], (((1,), (1,)), ((), ())),
                               preferred_element_type=F32)

    def finish(item, s):
        r0, h = item
        sl = slice(h * MEM_HEAD_DIM, (h + 1) * MEM_HEAD_DIM)
        m = jnp.max(s, axis=-1, keepdims=True)
        p = jnp.exp2(s - m)
        den = jnp.sum(p, axis=-1, keepdims=True)
        o = jnp.dot(p.astype(BF16), mv_ref[:, sl], preferred_element_type=F32)
        mo_ref[r0:r0 + BLOCK, sl] = (o / den).astype(BF16)

    return _score_ahead([(r0, h) for r0 in range(0, rows, BLOCK) for h in range(MEM_HEADS)], scores, finish)


def _mem_attention(mq_ref, mk_ref, mv_ref, mo_ref):
    for _ in _mem_attention_stages(mq_ref, mk_ref, mv_ref, mo_ref):
        pass


def _window_attention_stages(sink_ref, q_ref, k_refs, v_refs, o_ref, has_prev, has_next):
    nq = q_ref.shape[0] // BLOCK
    key = lax.broadcasted_iota(jnp.int32, (BLOCK, ATTN_GROUP * BLOCK), 0)
    qry = lax.broadcasted_iota(jnp.int32, (BLOCK, ATTN_GROUP * BLOCK), 1) % BLOCK
    band_prev = key >= qry
    band_next = key <= qry
    lo = lax.broadcasted_iota(jnp.int32, (BLOCK, LANES), 1) < ATTN_HEAD_DIM
    top = lax.broadcasted_iota(jnp.int32, (LANES, BLOCK), 0) < ATTN_HEAD_DIM
    k_all = jnp.concatenate([r[...] for r in k_refs], axis=0)
    v_all = jnp.concatenate([r[...] for r in v_refs], axis=0)
    zero = jnp.zeros((), BF16)
    def scores(item):
        j, h = item
        r0 = j * BLOCK
        kh = k_all[r0:r0 + 3 * BLOCK, h * LANES:(h + 1) * LANES]
        parts = []
        for g in range(ATTN_GROUP):
            slab = q_ref[r0:r0 + BLOCK, (2 * h + g // 2) * LANES:(2 * h + g // 2 + 1) * LANES]
            parts.append(jnp.where(lo if g % 2 == 0 else ~lo, slab, zero))
        qs = jnp.concatenate(parts, axis=0)
        return lax.dot_general(kh, qs, (((1,), (1,)), ((), ())),
                               preferred_element_type=F32)

    def finish(item, st):
        j, h = item
        r0 = j * BLOCK
        m_prev = band_prev if j > 0 else band_prev & has_prev
        m_next = band_next if j < nq - 1 else band_next & has_next
        vh = v_all[r0:r0 + 3 * BLOCK, h * LANES:(h + 1) * LANES]
        s0 = jnp.where(m_prev, st[0:BLOCK], NEG)
        s1 = st[BLOCK:2 * BLOCK]
        s2 = jnp.where(m_next, st[2 * BLOCK:3 * BLOCK], NEG)
        sink = jnp.concatenate(
            [jnp.full((1, BLOCK), sink_ref[ATTN_GROUP * h + g] * LOG2E, F32) for g in range(ATTN_GROUP)],
            axis=1)
        m = jnp.max(jnp.maximum(jnp.maximum(s0, s1), s2), axis=0, keepdims=True)
        m = jnp.maximum(m, sink)
        p0, p1, p2 = jnp.exp2(s0 - m), jnp.exp2(s1 - m), jnp.exp2(s2 - m)
        den = jnp.sum(p0 + p1 + p2, axis=0, keepdims=True) + jnp.exp2(sink - m)
        pt = jnp.concatenate([p0, p1, p2], axis=0).astype(BF16)
        ot = lax.dot_general(vh, pt, (((0,), (0,)), ((), ())),
                             preferred_element_type=F32)
        ot = ot / den
        for pair in range(2):
            even = ot[:, (2 * pair) * BLOCK:(2 * pair + 1) * BLOCK]
            odd = ot[:, (2 * pair + 1) * BLOCK:(2 * pair + 2) * BLOCK]
            slab = jnp.where(top, even, odd).T
            o_ref[r0:r0 + BLOCK, (2 * h + pair) * LANES:(2 * h + pair + 1) * LANES] = slab.astype(BF16)

    return _score_ahead([(j, h) for j in range(nq) for h in range(ATTN_KV_HEADS)], scores, finish)


MLP_COARSE = (D_MODEL, FF_CHUNK, D_MODEL // 2)
MLP_FINE = (2 * LANES, 2 * LANES, 2 * LANES)


def _mlp_stages(x_ref, y_ref, mo_ref, wo_ref, g_ref, wup_ref, wdn_ref, result, widths=MLP_COARSE):
    proj_w, up_w, down_w = widths
    mix_w = y_ref.shape[-1]
    pieces = []
    for c in range(0, D_MODEL, proj_w):
        pieces.append(x_ref[:, c:c + proj_w]
                      + jnp.dot(y_ref[...], wo_ref[0:mix_w, c:c + proj_w], preferred_element_type=F32)
                      + jnp.dot(mo_ref[...], wo_ref[mix_w:, c:c + proj_w], preferred_element_type=F32))
        yield
    x1 = jnp.concatenate(pieces, axis=1)
    xn = _rms(x1, g_ref[...]).astype(BF16)
    hids = []
    for c in range(0, D_FF, up_w):
        hid = jnp.dot(xn, wup_ref[:, c:c + up_w], preferred_element_type=F32)
        hid = jnp.maximum(hid, 0.0)
        hids.append((hid * hid).astype(BF16))
        yield
    hid = jnp.concatenate(hids, axis=1)
    pieces = []
    for c in range(0, D_MODEL, down_w):
        pieces.append(x1[:, c:c + down_w]
                      + jnp.dot(hid, wdn_ref[:, c:c + down_w], preferred_element_type=F32))
        if c + down_w == D_MODEL:
            result.append(jnp.concatenate(pieces, axis=1))
        yield


def _attn_mlp_kernel(sink_ref, x_ref, q_ref, kp_ref, kc_ref, kn_ref, vp_ref, vc_ref, vn_ref,
                     mq_ref, mk_ref, mv_ref, wo_ref, g_ref, wup_ref, wdn_ref, fg_ref, o_ref,
                     y_ref, mo_ref, *, tiles_per_row, final):
    k = pl.program_id(0)
    kk = jnp.minimum(k, pl.num_programs(0) - 2)
    t_idx = kk % tiles_per_row

    @pl.when(k == 0)
    def _():
        y_ref[...] = jnp.zeros_like(y_ref)
        mo_ref[...] = jnp.zeros_like(mo_ref)

    def attention():
        yield from _window_attention_stages(sink_ref, q_ref, (kp_ref, kc_ref, kn_ref),
                                            (vp_ref, vc_ref, vn_ref), y_ref,
                                            t_idx > 0, t_idx < tiles_per_row - 1)
        yield from _mem_attention_stages(mq_ref, mk_ref, mv_ref, mo_ref)

    result = []
    attn = attention()
    mlp = _mlp_stages(x_ref, y_ref, mo_ref, wo_ref, g_ref, wup_ref, wdn_ref, result, MLP_FINE)
    next(attn)
    next(mlp)
    next(attn)
    for _ in range(D_MODEL // MLP_FINE[0] - 1):
        next(mlp)
    done = object()
    for tag in attn:
        if tag == "S":
            next(mlp, done)
    for _ in mlp:
        pass
    acc = result[0]
    if final:
        acc = _rms(acc, fg_ref[...])
    o_ref[...] = acc


def _attention_mlp(x3, p, memkv, sinks, layer, wo, g, wup, wdn, fg, final, batch, seq, tq):
    nt = seq // tq
    n_tiles = batch * nt
    per = tq // BLOCK
    nb = seq // BLOCK
    d = x3.shape[-1]
    p3 = p.reshape(batch, seq, ATTN_IN_COLS)
    kcol = Q_W // KV_DUP_W
    vcol = kcol + 1
    mqcol = (Q_W + 2 * KV_DUP_W) // MEM_W

    def attn_bt(k):
        kk = jnp.minimum(k, n_tiles - 1)
        return kk // nt, kk % nt

    def mlp_bt(k):
        return attn_bt(jnp.maximum(k - 1, 0))

    def kv_specs(colblk):
        def prev(k):
            b, i = attn_bt(k)
            return b, jnp.maximum(i * per - 1, 0), colblk

        def nxt(k):
            b, i = attn_bt(k)
            return b, jnp.minimum((i + 1) * per, nb - 1), colblk

        return [pl.BlockSpec((None, BLOCK, KV_DUP_W), prev),
                pl.BlockSpec((None, tq, KV_DUP_W), lambda k: (*attn_bt(k), colblk)),
                pl.BlockSpec((None, BLOCK, KV_DUP_W), nxt)]

    return pl.pallas_call(
        functools.partial(_attn_mlp_kernel, tiles_per_row=nt, final=final),
        grid=(n_tiles + 1,),
        in_specs=[pl.BlockSpec(memory_space=pltpu.SMEM),
                  pl.BlockSpec((None, tq, d), lambda k: (*mlp_bt(k), 0)),
                  pl.BlockSpec((None, tq, Q_W), lambda k: (*attn_bt(k), 0))]
        + kv_specs(kcol) + kv_specs(vcol)
        + [pl.BlockSpec((None, tq, MEM_W), lambda k: (*attn_bt(k), mqcol)),
           pl.BlockSpec((None, MEM_LEN, MEM_W), lambda k: (attn_bt(k)[0], 0, 2 * layer)),
           pl.BlockSpec((None, MEM_LEN, MEM_W), lambda k: (attn_bt(k)[0], 0, 2 * layer + 1)),
           _const_spec(wo.shape), _const_spec((1, d)), _const_spec(wup.shape), _const_spec(wdn.shape),
           _const_spec((1, d))],
        out_specs=pl.BlockSpec((None, tq, d), lambda k: (*mlp_bt(k), 0)),
        out_shape=jax.ShapeDtypeStruct(x3.shape, F32),
        scratch_shapes=[pltpu.VMEM((tq, Q_W), BF16), pltpu.VMEM((tq, MEM_W), BF16)],
        compiler_params=_params(1),
        name="attention_mlp",
    )(sinks, x3, p3, p3, p3, p3, p3, p3, p3, p3, memkv, memkv, wo, g, wup, wdn, fg)


def _permuted_conv(xp, before, after, cw_ref, cb_ref, ext_ref):
    ts, w = xp.shape
    n = ts // SUBLANES
    xp = xp.reshape(n, SUBLANES, w)
    ext_ref[1:n + 1] = xp
    sub = lax.broadcasted_iota(jnp.int32, (SUBLANES, w), 0)
    ext_ref[0] = jnp.where(sub == 0, before, pltpu.roll(xp[n - 1], 1, 0))
    ext_ref[n + 1] = jnp.where(sub == SUBLANES - 1, after[0:1], pltpu.roll(xp[0], SUBLANES - 1, 0))
    ext_ref[n + 2] = jnp.where(sub == SUBLANES - 1, after[1:2], pltpu.roll(xp[1], SUBLANES - 1, 0))
    xc = cb_ref[...][None]
    for tap in range(CONV_WIDTH):
        xc = xc + ext_ref[tap:tap + n] * cw_ref[tap:tap + 1, :][None]
    return xc.reshape(ts, w)


def _lru_gate_block(blk, xc, xcb, wg_ref, ba_ref, bx_ref, lam_ref, a_ref, u_ref):
    sl = slice(blk * LRU_BLOCK_DIM, (blk + 1) * LRU_BLOCK_DIM)
    gates = jnp.dot(xcb[:, sl], wg_ref[blk], preferred_element_type=F32)
    lam = lam_ref[:, sl]
    softplus_neg_lam = jnp.maximum(-lam, 0.0) + jnp.log1p(jnp.exp(-jnp.abs(lam)))
    rate = (-0.5 * LRU_C * LOG2E) * softplus_neg_lam
    t_rec = jnp.tanh(gates[:, :LRU_BLOCK_DIM] + ba_ref[:, sl])
    t_in = jnp.tanh(gates[:, LRU_BLOCK_DIM:] + bx_ref[:, sl])
    a = jnp.exp2(rate * t_rec + rate)
    a_ref[:, sl] = a
    z = 1.0 - a * a
    root = jnp.where(z > 0.0, z * lax.rsqrt(z), 0.0)
    u_ref[:, sl] = root * ((0.5 * t_in + 0.5) * xc[:, sl])


def _lru_scan(a_ref, u_ref, h_ref, carry_ref, reverse):
    ts, w = a_ref.shape
    n = ts // SUBLANES
    order = range(n - 1, -1, -1) if reverse else range(n)

    def rows(ref, j):
        return ref[j * SUBLANES:(j + 1) * SUBLANES, :]

    local = jnp.zeros((SUBLANES, w), F32)
    prod = jnp.ones((SUBLANES, w), F32)
    for j in order:
        a = rows(a_ref, j)
        local = a * local + rows(u_ref, j)
        prod = a * prod
    c = carry_ref[...]
    inflow = [None] * SUBLANES
    for s in (range(SUBLANES - 1, -1, -1) if reverse else range(SUBLANES)):
        inflow[s] = c
        c = prod[s:s + 1] * c + local[s:s + 1]
    carry_ref[...] = c
    h = jnp.concatenate(inflow, axis=0)
    for j in order:
        h = rows(a_ref, j) * h + rows(u_ref, j)
        h_ref[j * SUBLANES:(j + 1) * SUBLANES, :] = h


def _lru_stages(prev_ref, cur_ref, next_ref, first, last, lru_refs, scratch, reverse, pre, emit):
    perm_ref, cw_ref, cb_ref, wg_ref, ba_ref, bx_ref, lam_ref = lru_refs
    ext_ref, a_ref, u_ref, h_ref, carry_ref = scratch
    rows = cur_ref.shape[0]
    starts = list(range(0, rows, LRU_TILE))
    starts = starts[::-1] if reverse else starts
    permuted = [(jnp.dot(perm_ref[...], cur_ref[r0:r0 + LRU_TILE, :], preferred_element_type=F32), pre(r0))
                for r0 in starts]
    yield
    for r0, (xp, aux) in zip(starts, permuted):
        if r0 == 0:
            before = jnp.where(first, 0.0, prev_ref[BF16_ROWS - 1:BF16_ROWS, :].astype(F32))
        else:
            before = cur_ref[r0 - 1:r0, :].astype(F32)
        if r0 + LRU_TILE == rows:
            after = jnp.where(last, 0.0, next_ref[0:2, :].astype(F32))
        else:
            after = cur_ref[r0 + LRU_TILE:r0 + LRU_TILE + 2, :].astype(F32)
        xc = _permuted_conv(xp, before, after, cw_ref, cb_ref, ext_ref)
        xcb = xc.astype(BF16)
        for blk in range(LRU_BLOCKS):
            _lru_gate_block(blk, xc, xcb, wg_ref, ba_ref, bx_ref, lam_ref, a_ref, u_ref)
            yield
        _lru_scan(a_ref, u_ref, h_ref, carry_ref, reverse)
        emit(r0, aux)
        yield


def _lru_in_stages(x_ref, g_ref, w_ref, mk_ref, mv_ref, xg_ref, mq_ref, mo_ref):
    xn = _rms(x_ref[...], g_ref[...]).astype(BF16)
    xg_w = xg_ref.shape[-1]
    piece = MLP_FINE[0]
    for c in range(0, LRU_IN_COLS, piece):
        acc = jnp.dot(xn, w_ref[:, c:c + piece], preferred_element_type=F32)
        if c < xg_w:
            xg_ref[:, c:c + piece] = acc.astype(BF16)
        else:
            mq_ref[:, c - xg_w:c - xg_w + piece] = (acc * (LOG2E * MEM_HEAD_DIM ** -0.5)).astype(BF16)
        yield
    yield from _mem_attention_stages(mq_ref, mk_ref, mv_ref, mo_ref)


def _lru_in_fwd_kernel(x_ref, g_ref, w_ref, mk_ref, mv_ref,
                       perm_ref, cw_ref, cb_ref, wg_ref, ba_ref, bx_ref, lam_ref,
                       xg_ref, mo_ref, hf_ref,
                       xprev_ref, tail_ref, mq_ref, ext_ref, a_ref, u_ref, h_ref, carry_ref,
                       *, blocks_per_row):
    k = pl.program_id(0)
    t_prev = jnp.maximum(k - 1, 0) % blocks_per_row

    @pl.when(k == 0)
    def _():
        xprev_ref[...] = jnp.zeros_like(xprev_ref)
        tail_ref[...] = jnp.zeros_like(tail_ref)

    @pl.when(t_prev == 0)
    def _():
        carry_ref[...] = jnp.zeros_like(carry_ref)

    def emit(r0, aux):
        hf_ref[r0:r0 + LRU_TILE, :] = h_ref[...].astype(BF16)

    proj = _lru_in_stages(x_ref, g_ref, w_ref, mk_ref, mv_ref, xg_ref, mq_ref, mo_ref)
    lru = _lru_stages(tail_ref, xprev_ref, xg_ref.at[:, 0:LRU_WIDTH],
                      t_prev == 0, t_prev == blocks_per_row - 1,
                      (perm_ref, cw_ref, cb_ref, wg_ref, ba_ref, bx_ref, lam_ref),
                      (ext_ref, a_ref, u_ref, h_ref, carry_ref), False, lambda r0: None, emit)
    done = object()
    next(lru)
    next(proj)
    for n, _ in enumerate(lru):
        if n % 2 == 1:
            next(proj, done)
    for _ in proj:
        pass
    rows = xprev_ref.shape[0]
    tail_ref[...] = xprev_ref[rows - BF16_ROWS:rows, :]
    xprev_ref[...] = xg_ref[:, 0:LRU_WIDTH]


def _lru_rev_mlp_kernel(x_ref, mo_ref, xp_ref, xc_ref, xn_ref, gate_ref, hf_ref,
                        perm_ref, unperm_ref, cw_ref, cb_ref, wg_ref, ba_ref, bx_ref, lam_ref,
                        wo_ref, g_ref, wup_ref, wdn_ref, fg_ref, o_ref,
                        ext_ref, a_ref, u_ref, h_ref, carry_ref, y_ref, *, blocks_per_row, final):
    k = pl.program_id(0)
    kk = jnp.minimum(k, pl.num_programs(0) - 2)
    t_blk = blocks_per_row - 1 - kk % blocks_per_row

    @pl.when(k == 0)
    def _():
        y_ref[...] = jnp.zeros_like(y_ref)

    @pl.when(kk % blocks_per_row == 0)
    def _():
        carry_ref[...] = jnp.zeros_like(carry_ref)

    def pre(r0):
        return jnp.dot(perm_ref[...], gate_ref[r0:r0 + LRU_TILE, :], preferred_element_type=F32)

    def emit(r0, gate):
        inner = gate * (GELU_C0 + GELU_C1 * (gate * gate))
        hsum = hf_ref[r0:r0 + LRU_TILE, :].astype(F32) + h_ref[...]
        y = (hsum * gate) * (0.5 * jnp.tanh(inner) + 0.5)
        y_ref[r0:r0 + LRU_TILE, :] = jnp.dot(unperm_ref[...], y.astype(BF16),
                                              preferred_element_type=F32).astype(BF16)

    result = []
    mlp = _mlp_stages(x_ref, y_ref, mo_ref, wo_ref, g_ref, wup_ref, wdn_ref, result, MLP_FINE)
    lru = _lru_stages(xp_ref, xc_ref, xn_ref, t_blk == 0, t_blk == blocks_per_row - 1,
                      (perm_ref, cw_ref, cb_ref, wg_ref, ba_ref, bx_ref, lam_ref),
                      (ext_ref, a_ref, u_ref, h_ref, carry_ref), True, pre, emit)
    next(lru)
    for _ in range(D_MODEL // MLP_FINE[0]):
        next(mlp)
    done = object()
    for _ in lru:
        next(mlp, done)
    for _ in mlp:
        pass
    acc = result[0]
    if final:
        acc = _rms(acc, fg_ref[...])
    o_ref[...] = acc


def _lru_weight_specs():
    return [_const_spec((CONV_WIDTH, LRU_WIDTH)), _const_spec((1, LRU_WIDTH)),
            _const_spec((LRU_BLOCKS, LRU_BLOCK_DIM, 2 * LRU_BLOCK_DIM)),
            _const_spec((1, LRU_WIDTH)), _const_spec((1, LRU_WIDTH)), _const_spec((1, LRU_WIDTH))]


def _lru_scratch():
    ts = LRU_TILE
    return [pltpu.VMEM((ts // SUBLANES + CONV_WIDTH - 1, SUBLANES, LRU_WIDTH), F32),
            pltpu.VMEM((ts, LRU_WIDTH), F32),
            pltpu.VMEM((ts, LRU_WIDTH), F32),
            pltpu.VMEM((ts, LRU_WIDTH), F32),
            pltpu.VMEM((1, LRU_WIDTH), F32)]


def _chunk_permutation(ts):
    n = ts // SUBLANES
    r = jnp.arange(ts)
    src = (r % SUBLANES) * n + r // SUBLANES
    return (src[:, None] == jnp.arange(ts)[None, :]).astype(BF16)


def _lru_in_forward(x3, g, w_in, memkv, layer, perm, weights, batch, seq, blk):
    nt = seq // blk
    n_blocks = batch * nt
    d = x3.shape[-1]

    def cur_bt(k):
        kk = jnp.minimum(k, n_blocks - 1)
        return kk // nt, kk % nt

    def prev_bt(k):
        return cur_bt(jnp.maximum(k - 1, 0))

    return pl.pallas_call(
        functools.partial(_lru_in_fwd_kernel, blocks_per_row=nt),
        grid=(n_blocks + 1,),
        in_specs=[pl.BlockSpec((None, blk, d), lambda k: (*cur_bt(k), 0)),
                  _const_spec((1, d)), _const_spec((d, LRU_IN_COLS)),
                  pl.BlockSpec((None, MEM_LEN, MEM_W), lambda k: (cur_bt(k)[0], 0, 2 * layer)),
                  pl.BlockSpec((None, MEM_LEN, MEM_W), lambda k: (cur_bt(k)[0], 0, 2 * layer + 1)),
                  _const_spec((LRU_TILE, LRU_TILE))]
        + _lru_weight_specs(),
        out_specs=[pl.BlockSpec((None, blk, 2 * LRU_WIDTH), lambda k: (*cur_bt(k), 0)),
                   pl.BlockSpec((None, blk, MEM_W), lambda k: (*cur_bt(k), 0)),
                   pl.BlockSpec((None, blk, LRU_WIDTH), lambda k: (*prev_bt(k), 0))],
        out_shape=[jax.ShapeDtypeStruct((batch, seq, 2 * LRU_WIDTH), BF16),
                   jax.ShapeDtypeStruct((batch, seq, MEM_W), BF16),
                   jax.ShapeDtypeStruct((batch, seq, LRU_WIDTH), BF16)],
        scratch_shapes=[pltpu.VMEM((blk, LRU_WIDTH), BF16),
                        pltpu.VMEM((BF16_ROWS, LRU_WIDTH), BF16),
                        pltpu.VMEM((blk, MEM_W), BF16)]
        + _lru_scratch(),
        compiler_params=_params(1),
        name="lru_in_forward",
    )(x3, g, w_in, memkv, memkv, perm, *weights)


def _lru_reverse_mlp(x3, mo3, p3, hf, perm, weights, wo, g, wup, wdn, fg, final, batch, seq, blk):
    nt = seq // blk
    n_blocks = batch * nt
    per = blk // BF16_ROWS
    last_halo = seq // BF16_ROWS - 1
    d = x3.shape[-1]

    def lru_bt(k):
        kk = jnp.minimum(k, n_blocks - 1)
        return kk // nt, nt - 1 - kk % nt

    def mlp_bt(k):
        return lru_bt(jnp.maximum(k - 1, 0))

    def at_mlp(col):
        return lambda k: (*mlp_bt(k), col)

    def at_lru(col):
        return lambda k: (*lru_bt(k), col)

    def prev_idx(k):
        b, t = lru_bt(k)
        return b, jnp.maximum(t * per - 1, 0), 0

    def next_idx(k):
        b, t = lru_bt(k)
        return b, jnp.minimum((t + 1) * per, last_halo), 0

    return pl.pallas_call(
        functools.partial(_lru_rev_mlp_kernel, blocks_per_row=nt, final=final),
        grid=(n_blocks + 1,),
        in_specs=[pl.BlockSpec((None, blk, d), at_mlp(0)),
                  pl.BlockSpec((None, blk, MEM_W), at_mlp(0)),
                  pl.BlockSpec((None, BF16_ROWS, LRU_WIDTH), prev_idx),
                  pl.BlockSpec((None, blk, LRU_WIDTH), at_lru(0)),
                  pl.BlockSpec((None, BF16_ROWS, LRU_WIDTH), next_idx),
                  pl.BlockSpec((None, blk, LRU_WIDTH), at_lru(1)),
                  pl.BlockSpec((None, blk, LRU_WIDTH), at_lru(0)),
                  _const_spec((LRU_TILE, LRU_TILE)), _const_spec((LRU_TILE, LRU_TILE))]
        + _lru_weight_specs()
        + [_const_spec(wo.shape), _const_spec((1, d)), _const_spec(wup.shape), _const_spec(wdn.shape),
           _const_spec((1, d))],
        out_specs=pl.BlockSpec((None, blk, d), at_mlp(0)),
        out_shape=jax.ShapeDtypeStruct(x3.shape, F32),
        scratch_shapes=_lru_scratch() + [pltpu.VMEM((blk, LRU_WIDTH), BF16)],
        compiler_params=_params(1),
        name="lru_reverse_mlp",
    )(x3, mo3, p3, p3, p3, p3, hf, perm, perm.T, *weights, wo, g, wup, wdn, fg)


def _dup_heads(w):
    d = w.shape[0]
    w4 = w.reshape(d, ATTN_KV_HEADS, 1, ATTN_HEAD_DIM)
    return jnp.broadcast_to(w4, (d, ATTN_KV_HEADS, 2, ATTN_HEAD_DIM)).reshape(d, KV_DUP_W)


def _rotary_tables(tile):
    half = ROPE_DIM // 2
    inv_freq = ROPE_THETA ** (-2.0 * jnp.arange(half, dtype=F32) / ROPE_DIM)
    freq = jnp.broadcast_to(inv_freq[:, None], (half, tile))
    d = jnp.arange(LANES) % ATTN_HEAD_DIM
    first = d < half
    second = (d >= half) & (d < ROPE_DIM)
    pick = (jnp.arange(half)[:, None] == (d % half)[None, :])
    zeros8 = jnp.zeros((half, LANES), F32)
    cos_rows = jnp.where(pick & (d < ROPE_DIM)[None, :], 1.0, 0.0)
    lo_rows = jnp.where(pick & first[None, :], -1.0, 0.0)
    hi_rows = jnp.where(pick & second[None, :], 1.0, 0.0)
    ones_rows = jnp.zeros((BF16_ROWS, LANES), F32).at[0].set(jnp.where(d < ROPE_DIM, 0.0, 1.0))
    zeros16 = jnp.zeros((BF16_ROWS, LANES), F32)
    cos_cols = jnp.concatenate([cos_rows, cos_rows, zeros8, zeros8, ones_rows], axis=0)
    lo_cols = jnp.concatenate([zeros8, zeros8, lo_rows, lo_rows, zeros16], axis=0)
    hi_cols = jnp.concatenate([zeros8, zeros8, hi_rows, hi_rows, zeros16], axis=0)
    return freq, jnp.concatenate([cos_cols, lo_cols, hi_cols], axis=1).astype(BF16)


def _gate_weights(wa, wx):
    return (0.5 * jnp.concatenate([wa, wx], axis=-1)).astype(BF16)


def kernel(x, mem, positions, mix_norm, mlp_norm, mem_norm, final_norm, w_mem_kv, w_out, w_up, w_down,
           attn_w_in, attn_sinks, lru_w_in, lru_conv_w, lru_conv_b, lru_wa, lru_ba, lru_wx, lru_bx,
           lru_lambda):
    batch, seq, d = x.shape
    t = batch * seq
    depth = mix_norm.shape[0]
    xf = x.reshape(t, d)
    ones_d = jnp.ones((1, d), F32)

    w_kv_all = jnp.concatenate([w_mem_kv[l] for l in range(depth)], axis=1).astype(BF16)
    memkv = _norm_proj(mem.reshape(batch * MEM_LEN, d), mem_norm.reshape(1, d),
                       jnp.ones((1, w_kv_all.shape[1]), F32), w_kv_all,
                       tile=batch * MEM_LEN // 2)
    memkv = memkv.reshape(batch, MEM_LEN, depth * 2 * MEM_W)

    pos_rows = positions.reshape(t // PROJ_TILE, 1, PROJ_TILE).astype(jnp.int32)
    rot_freq, rot_expand = _rotary_tables(PROJ_TILE)

    for l in range(depth):
        kind, j = l % 2, l // 2
        g_mix = mix_norm[l].reshape(1, d)
        final = l == depth - 1
        mlp_weights = (w_out[l].astype(BF16), mlp_norm[l].reshape(1, d), w_up[l].astype(BF16),
                       w_down[l].astype(BF16), final_norm.reshape(1, d) if final else ones_d)
        if kind == 0:
            w = attn_w_in[j]
            w_in = jnp.concatenate([w[:, :Q_W], _dup_heads(w[:, Q_W:Q_W + KV_W]),
                                    _dup_heads(w[:, Q_W + KV_W:Q_W + 2 * KV_W]),
                                    w[:, Q_W + 2 * KV_W:]], axis=1).astype(BF16)
            p = _attn_in(xf, g_mix, pos_rows, rot_freq, rot_expand, w_in, PROJ_TILE)
            xf = _attention_mlp(xf.reshape(batch, seq, d), p, memkv, attn_sinks[j].astype(F32), l,
                                *mlp_weights, final, batch, seq, ATTN_TILE).reshape(t, d)
        else:
            x3 = xf.reshape(batch, seq, d)
            cw, cb = lru_conv_w[j], lru_conv_b[j].reshape(1, LRU_WIDTH)

            def dir_weights(dirn):
                return (cw, cb, _gate_weights(lru_wa[j, dirn], lru_wx[j, dirn]),
                        0.5 * lru_ba[j, dirn].reshape(1, LRU_WIDTH),
                        0.5 * lru_bx[j, dirn].reshape(1, LRU_WIDTH),
                        lru_lambda[j, dirn].reshape(1, LRU_WIDTH))

            perm = _chunk_permutation(LRU_TILE)
            xg, mo, hf = _lru_in_forward(x3, g_mix, lru_w_in[j].astype(BF16), memkv, l, perm,
                                         dir_weights(0), batch, seq, ROW_TILE)
            xf = _lru_reverse_mlp(x3, mo, xg, hf, perm, dir_weights(1),
                                  *mlp_weights, final, batch, seq, ROW_TILE).reshape(t, d)
    return xf.reshape(batch, seq, d)
```

```python
import functools

import jax
import jax.numpy as jnp
import numpy as np
from jax import lax
from jax.experimental import pallas as pl
from jax.experimental.pallas import tpu as pltpu

F32 = jnp.float32
BF16 = jnp.bfloat16

D_MODEL = 1024
EPS = 1e-6
ATTN_HEADS = 16
ATTN_KV_HEADS = 4
ATTN_HEAD_DIM = 64
ATTN_GROUP = 4
WINDOW = 128
BLOCK = 128
ROPE_THETA = 500000.0
ROPE_DIM = 16
Q_W = 1024
KV_W = 256
MEM_LEN = 256
MEM_HEADS = 4
MEM_HEAD_DIM = 128
MEM_W = 512
LRU_WIDTH = 1024
LRU_BLOCKS = 8
LRU_BLOCK_DIM = 128
LRU_C = 8.0
CONV_WIDTH = 4
D_FF = 4096
NEG = -1e30

LANES = 128
SUBLANES = 8
BF16_ROWS = 16
KV_DUP_W = 2 * KV_W
ATTN_IN_COLS = Q_W + 2 * KV_DUP_W + MEM_W
LRU_IN_COLS = 2 * LRU_WIDTH + MEM_W
VMEM_LIMIT = 56 * 1024 * 1024

ROW_TILE = 512
LRU_TILE = 256
FF_CHUNK = 1024
PROJ_CHUNK = 512
PROJ_TILE = 1024
ATTN_TILE = 512
LOG2E = 1.4426950408889634
GELU_C0 = 0.7978845608028654
GELU_C1 = GELU_C0 * 0.044715


def _const_spec(shape):
    nd = len(shape)
    return pl.BlockSpec(shape, lambda *_: (0,) * nd, pipeline_mode=pl.Buffered(1))


def _params(n_axes):
    return pltpu.CompilerParams(dimension_semantics=("arbitrary",) * n_axes,
                                vmem_limit_bytes=VMEM_LIMIT)


def _rms(x, g):
    ms = jnp.mean(x * x, axis=-1, keepdims=True)
    return x * lax.rsqrt(ms + EPS) * g


def _norm_proj_kernel(x_ref, g_ref, w_ref, o_ref, *, chunk):
    xn = _rms(x_ref[...], g_ref[...]).astype(BF16)
    for c in range(0, o_ref.shape[-1], chunk):
        o_ref[:, c:c + chunk] = jnp.dot(xn, w_ref[:, c:c + chunk],
                                        preferred_element_type=F32).astype(BF16)


def _norm_proj(x, g, w, tile):
    t, d = x.shape
    n = w.shape[1]
    return pl.pallas_call(
        functools.partial(_norm_proj_kernel, chunk=PROJ_CHUNK),
        grid=(t // tile,),
        in_specs=[pl.BlockSpec((tile, d), lambda i: (i, 0)),
                  _const_spec((1, d)), _const_spec((d, n))],
        out_specs=pl.BlockSpec((tile, n), lambda i: (i, 0)),
        out_shape=jax.ShapeDtypeStruct((t, n), BF16),
        compiler_params=_params(1),
        name="norm_proj",
    )(x, g, w)


def _split_bf16(x):
    hi = x.astype(BF16).astype(F32)
    return hi, x - hi


def _attn_in_kernel(x_ref, g_ref, pos_ref, freq_ref, expand_ref, w_ref, o_ref):
    xn = _rms(x_ref[...], g_ref[...]).astype(BF16)
    ang = freq_ref[...] * pos_ref[...].astype(F32)
    cos_hi, cos_lo = _split_bf16(jnp.cos(ang))
    sin_hi_, sin_lo_ = _split_bf16(jnp.sin(ang))
    one = jnp.where(lax.broadcasted_iota(jnp.int32, (BF16_ROWS, ang.shape[1]), 0) == 0, 1.0, 0.0)
    trig = jnp.concatenate([cos_hi, cos_lo, sin_hi_, sin_lo_, one], axis=0).astype(BF16)
    tables = lax.dot_general(trig, expand_ref[...], (((0,), (0,)), ((), ())),
                             preferred_element_type=F32)
    cos = tables[:, 0:LANES]
    sin_lo = tables[:, LANES:2 * LANES]
    sin_hi = tables[:, 2 * LANES:3 * LANES]
    q_scale = LOG2E * ATTN_HEAD_DIM ** -0.5
    mem_scale = LOG2E * MEM_HEAD_DIM ** -0.5
    half = ROPE_DIM // 2
    rot_cols = Q_W + KV_DUP_W
    for c in range(0, ATTN_IN_COLS, 2 * LANES):
        acc = jnp.dot(xn, w_ref[:, c:c + 2 * LANES], preferred_element_type=F32)
        for s in range(2):
            col = c + s * LANES
            slab = acc[:, s * LANES:(s + 1) * LANES]
            if col < rot_cols:
                slab = (slab * cos + pltpu.roll(slab, LANES - half, 1) * sin_lo
                        + pltpu.roll(slab, half, 1) * sin_hi)
                if col < Q_W:
                    slab = slab * q_scale
            elif col >= Q_W + 2 * KV_DUP_W:
                slab = slab * mem_scale
            o_ref[:, col:col + LANES] = slab.astype(BF16)


def _attn_in(x, g, pos, freq, expand, w, tile):
    t, d = x.shape
    n = w.shape[1]
    return pl.pallas_call(
        _attn_in_kernel,
        grid=(t // tile,),
        in_specs=[pl.BlockSpec((tile, d), lambda i: (i, 0)),
                  _const_spec((1, d)),
                  pl.BlockSpec((None, 1, tile), lambda i: (i, 0, 0)),
                  _const_spec(freq.shape), _const_spec(expand.shape), _const_spec((d, n))],
        out_specs=pl.BlockSpec((tile, n), lambda i: (i, 0)),
        out_shape=jax.ShapeDtypeStruct((t, n), BF16),
        compiler_params=_params(1),
        name="attn_in",
    )(x, g, pos, freq, expand, w)


def _score_ahead(items, scores, finish):
    pending = scores(items[0])
    yield "S"
    for n, item in enumerate(items):
        current = pending
        if n + 1 < len(items):
            pending = scores(items[n + 1])
            yield "S"
        finish(item, current)
        yield "F"


def _mem_attention_stages(mq_ref, mk_ref, mv_ref, mo_ref):
    rows = mq_ref.shape[0]

    def scores(item):
        r0, h = item
        sl = slice(h * MEM_HEAD_DIM, (h + 1) * MEM_HEAD_DIM)
        return lax.dot_general(mq_ref[r0:r0 + BLOCK, sl], mk_ref[:, sl], (((1,), (1,)), ((), ())),
                               preferred_element_type=F32)

    def finish(item, s):
        r0, h = item
        sl = slice(h * MEM_HEAD_DIM, (h + 1) * MEM_HEAD_DIM)
        m = jnp.max(s, axis=-1, keepdims=True)
        p = jnp.exp2(s - m)
        den = jnp.sum(p, axis=-1, keepdims=True)
        o = jnp.dot(p.astype(BF16), mv_ref[:, sl], preferred_element_type=F32)
        mo_ref[r0:r0 + BLOCK, sl] = (o / den).astype(BF16)

    return _score_ahead([(r0, h) for r0 in range(0, rows, BLOCK) for h in range(MEM_HEADS)], scores, finish)


def _mem_attention(mq_ref, mk_ref, mv_ref, mo_ref):
    for _ in _mem_attention_stages(mq_ref, mk_ref, mv_ref, mo_ref):
        pass


def _window_attention_stages(sink_ref, q_ref, k_refs, v_refs, o_ref, has_prev, has_next):
    nq = q_ref.shape[0] // BLOCK
    key = lax.broadcasted_iota(jnp.int32, (BLOCK, ATTN_GROUP * BLOCK), 0)
    qry = lax.broadcasted_iota(jnp.int32, (BLOCK, ATTN_GROUP * BLOCK), 1) % BLOCK
    band_prev = key >= qry
    band_next = key <= qry
    lo = lax.broadcasted_iota(jnp.int32, (BLOCK, LANES), 1) < ATTN_HEAD_DIM
    top = lax.broadcasted_iota(jnp.int32, (LANES, BLOCK), 0) < ATTN_HEAD_DIM
    k_all = jnp.concatenate([r[...] for r in k_refs], axis=0)
    v_all = jnp.concatenate([r[...] for r in v_refs], axis=0)
    zero = jnp.zeros((), BF16)
    def scores(item):
        j, h = item
        r0 = j * BLOCK
        kh = k_all[r0:r0 + 3 * BLOCK, h * LANES:(h + 1) * LANES]
        parts = []
        for g in range(ATTN_GROUP):
            slab = q_ref[r0:r0 + BLOCK, (2 * h + g // 2) * LANES:(2 * h + g // 2 + 1) * LANES]
            parts.append(jnp.where(lo if g % 2 == 0 else ~lo, slab, zero))
        qs = jnp.concatenate(parts, axis=0)
        return lax.dot_general(kh, qs, (((1,), (1,)), ((), ())),
                               preferred_element_type=F32)

    def finish(item, st):
        j, h = item
        r0 = j * BLOCK
        m_prev = band_prev if j > 0 else band_prev & has_prev
        m_next = band_next if j < nq - 1 else band_next & has_next
        vh = v_all[r0:r0 + 3 * BLOCK, h * LANES:(h + 1) * LANES]
        s0 = jnp.where(m_prev, st[0:BLOCK], NEG)
        s1 = st[BLOCK:2 * BLOCK]
        s2 = jnp.where(m_next, st[2 * BLOCK:3 * BLOCK], NEG)
        sink = jnp.concatenate(
            [jnp.full((1, BLOCK), sink_ref[ATTN_GROUP * h + g] * LOG2E, F32) for g in range(ATTN_GROUP)],
            axis=1)
        m = jnp.max(jnp.maximum(jnp.maximum(s0, s1), s2), axis=0, keepdims=True)
        m = jnp.maximum(m, sink)
        p0, p1, p2 = jnp.exp2(s0 - m), jnp.exp2(s1 - m), jnp.exp2(s2 - m)
        den = jnp.sum(p0 + p1 + p2, axis=0, keepdims=True) + jnp.exp2(sink - m)
        pt = jnp.concatenate([p0, p1, p2], axis=0).astype(BF16)
        ot = lax.dot_general(vh, pt, (((0,), (0,)), ((), ())),
                             preferred_element_type=F32)
        ot = ot / den
        for pair in range(2):
            even = ot[:, (2 * pair) * BLOCK:(2 * pair + 1) * BLOCK]
            odd = ot[:, (2 * pair + 1) * BLOCK:(2 * pair + 2) * BLOCK]
            slab = jnp.where(top, even, odd).T
            o_ref[r0:r0 + BLOCK, (2 * h + pair) * LANES:(2 * h + pair + 1) * LANES] = slab.astype(BF16)

    return _score_ahead([(j, h) for j in range(nq) for h in range(ATTN_KV_HEADS)], scores, finish)


MLP_COARSE = (D_MODEL, FF_CHUNK, D_MODEL // 2)
MLP_FINE = (2 * LANES, 2 * LANES, 2 * LANES)


def _mlp_stages(x_ref, y_ref, mo_ref, wo_ref, g_ref, wup_ref, wdn_ref, result, widths=MLP_COARSE):
    proj_w, up_w, down_w = widths
    mix_w = y_ref.shape[-1]
    pieces = []
    for c in range(0, D_MODEL, proj_w):
        pieces.append(x_ref[:, c:c + proj_w]
                      + jnp.dot(y_ref[...], wo_ref[0:mix_w, c:c + proj_w], preferred_element_type=F32)
                      + jnp.dot(mo_ref[...], wo_ref[mix_w:, c:c + proj_w], preferred_element_type=F32))
        yield
    x1 = jnp.concatenate(pieces, axis=1)
    xn = _rms(x1, g_ref[...]).astype(BF16)
    hids = []
    for c in range(0, D_FF, up_w):
        hid = jnp.dot(xn, wup_ref[:, c:c + up_w], preferred_element_type=F32)
        hid = jnp.maximum(hid, 0.0)
        hids.append((hid * hid).astype(BF16))
        yield
    hid = jnp.concatenate(hids, axis=1)
    pieces = []
    for c in range(0, D_MODEL, down_w):
        pieces.append(x1[:, c:c + down_w]
                      + jnp.dot(hid, wdn_ref[:, c:c + down_w], preferred_element_type=F32))
        if c + down_w == D_MODEL:
            result.append(jnp.concatenate(pieces, axis=1))
        yield


def _attn_mlp_kernel(sink_ref, x_ref, q_ref, kp_ref, kc_ref, kn_ref, vp_ref, vc_ref, vn_ref,
                     mq_ref, mk_ref, mv_ref, wo_ref, g_ref, wup_ref, wdn_ref, fg_ref, o_ref,
                     y_ref, mo_ref, *, tiles_per_row, final):
    k = pl.program_id(0)
    kk = jnp.minimum(k, pl.num_programs(0) - 2)
    t_idx = kk % tiles_per_row

    @pl.when(k == 0)
    def _():
        y_ref[...] = jnp.zeros_like(y_ref)
        mo_ref[...] = jnp.zeros_like(mo_ref)

    def attention():
        yield from _window_attention_stages(sink_ref, q_ref, (kp_ref, kc_ref, kn_ref),
                                            (vp_ref, vc_ref, vn_ref), y_ref,
                                            t_idx > 0, t_idx < tiles_per_row - 1)
        yield from _mem_attention_stages(mq_ref, mk_ref, mv_ref, mo_ref)

    result = []
    attn = attention()
    mlp = _mlp_stages(x_ref, y_ref, mo_ref, wo_ref, g_ref, wup_ref, wdn_ref, result, MLP_FINE)
    next(attn)
    next(mlp)
    next(attn)
    for _ in range(D_MODEL // MLP_FINE[0] - 1):
        next(mlp)
    done = object()
    for tag in attn:
        if tag == "S":
            next(mlp, done)
    for _ in mlp:
        pass
    acc = result[0]
    if final:
        acc = _rms(acc, fg_ref[...])
    o_ref[...] = acc


def _attention_mlp(x3, p, memkv, sinks, layer, wo, g, wup, wdn, fg, final, batch, seq, tq):
    nt = seq // tq
    n_tiles = batch * nt
    per = tq // BLOCK
    nb = seq // BLOCK
    d = x3.shape[-1]
    p3 = p.reshape(batch, seq, ATTN_IN_COLS)
    kcol = Q_W // KV_DUP_W
    vcol = kcol + 1
    mqcol = (Q_W + 2 * KV_DUP_W) // MEM_W

    def attn_bt(k):
        kk = jnp.minimum(k, n_tiles - 1)
        return kk // nt, kk % nt

    def mlp_bt(k):
        return attn_bt(jnp.maximum(k - 1, 0))

    def kv_specs(colblk):
        def prev(k):
            b, i = attn_bt(k)
            return b, jnp.maximum(i * per - 1, 0), colblk

        def nxt(k):
            b, i = attn_bt(k)
            return b, jnp.minimum((i + 1) * per, nb - 1), colblk

        return [pl.BlockSpec((None, BLOCK, KV_DUP_W), prev),
                pl.BlockSpec((None, tq, KV_DUP_W), lambda k: (*attn_bt(k), colblk)),
                pl.BlockSpec((None, BLOCK, KV_DUP_W), nxt)]

    return pl.pallas_call(
        functools.partial(_attn_mlp_kernel, tiles_per_row=nt, final=final),
        grid=(n_tiles + 1,),
        in_specs=[pl.BlockSpec(memory_space=pltpu.SMEM),
                  pl.BlockSpec((None, tq, d), lambda k: (*mlp_bt(k), 0)),
                  pl.BlockSpec((None, tq, Q_W), lambda k: (*attn_bt(k), 0))]
        + kv_specs(kcol) + kv_specs(vcol)
        + [pl.BlockSpec((None, tq, MEM_W), lambda k: (*attn_bt(k), mqcol)),
           pl.BlockSpec((None, MEM_LEN, MEM_W), lambda k: (attn_bt(k)[0], 0, 2 * layer)),
           pl.BlockSpec((None, MEM_LEN, MEM_W), lambda k: (attn_bt(k)[0], 0, 2 * layer + 1)),
           _const_spec(wo.shape), _const_spec((1, d)), _const_spec(wup.shape), _const_spec(wdn.shape),
           _const_spec((1, d))],
        out_specs=pl.BlockSpec((None, tq, d), lambda k: (*mlp_bt(k), 0)),
        out_shape=jax.ShapeDtypeStruct(x3.shape, F32),
        scratch_shapes=[pltpu.VMEM((tq, Q_W), BF16), pltpu.VMEM((tq, MEM_W), BF16)],
        compiler_params=_params(1),
        name="attention_mlp",
    )(sinks, x3, p3, p3, p3, p3, p3, p3, p3, p3, memkv, memkv, wo, g, wup, wdn, fg)


def _permuted_conv(xp, before, after, cw_ref, cb_ref, ext_ref):
    ts, w = xp.shape
    n = ts // SUBLANES
    xp = xp.reshape(n, SUBLANES, w)
    ext_ref[1:n + 1] = xp
    sub = lax.broadcasted_iota(jnp.int32, (SUBLANES, w), 0)
    ext_ref[0] = jnp.where(sub == 0, before, pltpu.roll(xp[n - 1], 1, 0))
    ext_ref[n + 1] = jnp.where(sub == SUBLANES - 1, after[0:1], pltpu.roll(xp[0], SUBLANES - 1, 0))
    ext_ref[n + 2] = jnp.where(sub == SUBLANES - 1, after[1:2], pltpu.roll(xp[1], SUBLANES - 1, 0))
    xc = cb_ref[...][None]
    for tap in range(CONV_WIDTH):
        xc = xc + ext_ref[tap:tap + n] * cw_ref[tap:tap + 1, :][None]
    return xc.reshape(ts, w)


def _lru_gate_block(blk, xc, xcb, wg_ref, ba_ref, bx_ref, lam_ref, a_ref, u_ref):
    sl = slice(blk * LRU_BLOCK_DIM, (blk + 1) * LRU_BLOCK_DIM)
    gates = jnp.dot(xcb[:, sl], wg_ref[blk], preferred_element_type=F32)
    lam = lam_ref[:, sl]
    softplus_neg_lam = jnp.maximum(-lam, 0.0) + jnp.log1p(jnp.exp(-jnp.abs(lam)))
    rate = (-0.5 * LRU_C * LOG2E) * softplus_neg_lam
    t_rec = jnp.tanh(gates[:, :LRU_BLOCK_DIM] + ba_ref[:, sl])
    t_in = jnp.tanh(gates[:, LRU_BLOCK_DIM:] + bx_ref[:, sl])
    a = jnp.exp2(rate * t_rec + rate)
    a_ref[:, sl] = a
    z = 1.0 - a * a
    root = jnp.where(z > 0.0, z * lax.rsqrt(z), 0.0)
    u_ref[:, sl] = root * ((0.5 * t_in + 0.5) * xc[:, sl])


def _lru_scan(a_ref, u_ref, h_ref, carry_ref, reverse):
    ts, w = a_ref.shape
    n = ts // SUBLANES
    order = range(n - 1, -1, -1) if reverse else range(n)

    def rows(ref, j):
        return ref[j * SUBLANES:(j + 1) * SUBLANES, :]

    local = jnp.zeros((SUBLANES, w), F32)
    prod = jnp.ones((SUBLANES, w), F32)
    for j in order:
        a = rows(a_ref, j)
        local = a * local + rows(u_ref, j)
        prod = a * prod
    c = carry_ref[...]
    inflow = [None] * SUBLANES
    for s in (range(SUBLANES - 1, -1, -1) if reverse else range(SUBLANES)):
        inflow[s] = c
        c = prod[s:s + 1] * c + local[s:s + 1]
    carry_ref[...] = c
    h = jnp.concatenate(inflow, axis=0)
    for j in order:
        h = rows(a_ref, j) * h + rows(u_ref, j)
        h_ref[j * SUBLANES:(j + 1) * SUBLANES, :] = h


def _lru_stages(prev_ref, cur_ref, next_ref, first, last, lru_refs, scratch, reverse, pre, emit):
    perm_ref, cw_ref, cb_ref, wg_ref, ba_ref, bx_ref, lam_ref = lru_refs
    ext_ref, a_ref, u_ref, h_ref, carry_ref = scratch
    rows = cur_ref.shape[0]
    starts = list(range(0, rows, LRU_TILE))
    starts = starts[::-1] if reverse else starts
    permuted = [(jnp.dot(perm_ref[...], cur_ref[r0:r0 + LRU_TILE, :], preferred_element_type=F32), pre(r0))
                for r0 in starts]
    yield
    for r0, (xp, aux) in zip(starts, permuted):
        if r0 == 0:
            before = jnp.where(first, 0.0, prev_ref[BF16_ROWS - 1:BF16_ROWS, :].astype(F32))
        else:
            before = cur_ref[r0 - 1:r0, :].astype(F32)
        if r0 + LRU_TILE == rows:
            after = jnp.where(last, 0.0, next_ref[0:2, :].astype(F32))
        else:
            after = cur_ref[r0 + LRU_TILE:r0 + LRU_TILE + 2, :].astype(F32)
        xc = _permuted_conv(xp, before, after, cw_ref, cb_ref, ext_ref)
        xcb = xc.astype(BF16)
        for blk in range(LRU_BLOCKS):
            _lru_gate_block(blk, xc, xcb, wg_ref, ba_ref, bx_ref, lam_ref, a_ref, u_ref)
            yield
        _lru_scan(a_ref, u_ref, h_ref, carry_ref, reverse)
        emit(r0, aux)
        yield


def _lru_in_stages(x_ref, g_ref, w_ref, mk_ref, mv_ref, xg_ref, mq_ref, mo_ref):
    xn = _rms(x_ref[...], g_ref[...]).astype(BF16)
    xg_w = xg_ref.shape[-1]
    piece = MLP_FINE[0]
    for c in range(0, LRU_IN_COLS, piece):
        acc = jnp.dot(xn, w_ref[:, c:c + piece], preferred_element_type=F32)
        if c < xg_w:
            xg_ref[:, c:c + piece] = acc.astype(BF16)
        else:
            mq_ref[:, c - xg_w:c - xg_w + piece] = (acc * (LOG2E * MEM_HEAD_DIM ** -0.5)).astype(BF16)
        yield
    yield from _mem_attention_stages(mq_ref, mk_ref, mv_ref, mo_ref)


def _lru_in_fwd_kernel(x_ref, g_ref, w_ref, mk_ref, mv_ref,
                       perm_ref, cw_ref, cb_ref, wg_ref, ba_ref, bx_ref, lam_ref,
                       xg_ref, mo_ref, hf_ref,
                       xprev_ref, tail_ref, mq_ref, ext_ref, a_ref, u_ref, h_ref, carry_ref,
                       *, blocks_per_row):
    k = pl.program_id(0)
    t_prev = jnp.maximum(k - 1, 0) % blocks_per_row

    @pl.when(k == 0)
    def _():
        xprev_ref[...] = jnp.zeros_like(xprev_ref)
        tail_ref[...] = jnp.zeros_like(tail_ref)

    @pl.when(t_prev == 0)
    def _():
        carry_ref[...] = jnp.zeros_like(carry_ref)

    def emit(r0, aux):
        hf_ref[r0:r0 + LRU_TILE, :] = h_ref[...].astype(BF16)

    proj = _lru_in_stages(x_ref, g_ref, w_ref, mk_ref, mv_ref, xg_ref, mq_ref, mo_ref)
    lru = _lru_stages(tail_ref, xprev_ref, xg_ref.at[:, 0:LRU_WIDTH],
                      t_prev == 0, t_prev == blocks_per_row - 1,
                      (perm_ref, cw_ref, cb_ref, wg_ref, ba_ref, bx_ref, lam_ref),
                      (ext_ref, a_ref, u_ref, h_ref, carry_ref), False, lambda r0: None, emit)
    done = object()
    next(lru)
    next(proj)
    traced = 1
    stages_per_tile = LRU_BLOCKS + 1
    for n, _ in enumerate(lru):
        if n == stages_per_tile - 1:
            assert traced * MLP_FINE[0] >= LRU_WIDTH
        if n % 2 == 1:
            next(proj, done)
            traced += 1
    for _ in proj:
        pass
    rows = xprev_ref.shape[0]
    tail_ref[...] = xprev_ref[rows - BF16_ROWS:rows, :]
    xprev_ref[...] = xg_ref[:, 0:LRU_WIDTH]


def _lru_rev_mlp_kernel(x_ref, mo_ref, xp_ref, xc_ref, xn_ref, gate_ref, hf_ref,
                        perm_ref, unperm_ref, cw_ref, cb_ref, wg_ref, ba_ref, bx_ref, lam_ref,
                        wo_ref, g_ref, wup_ref, wdn_ref, fg_ref, o_ref,
                        ext_ref, a_ref, u_ref, h_ref, carry_ref, y_ref, *, blocks_per_row, final):
    k = pl.program_id(0)
    kk = jnp.minimum(k, pl.num_programs(0) - 2)
    t_blk = blocks_per_row - 1 - kk % blocks_per_row

    @pl.when(k == 0)
    def _():
        y_ref[...] = jnp.zeros_like(y_ref)

    @pl.when(kk % blocks_per_row == 0)
    def _():
        carry_ref[...] = jnp.zeros_like(carry_ref)

    def pre(r0):
        return jnp.dot(perm_ref[...], gate_ref[r0:r0 + LRU_TILE, :], preferred_element_type=F32)

    def emit(r0, gate):
        inner = gate * (GELU_C0 + GELU_C1 * (gate * gate))
        hsum = hf_ref[r0:r0 + LRU_TILE, :].astype(F32) + h_ref[...]
        y = (hsum * gate) * (0.5 * jnp.tanh(inner) + 0.5)
        y_ref[r0:r0 + LRU_TILE, :] = jnp.dot(unperm_ref[...], y.astype(BF16),
                                              preferred_element_type=F32).astype(BF16)

    result = []
    mlp = _mlp_stages(x_ref, y_ref, mo_ref, wo_ref, g_ref, wup_ref, wdn_ref, result, MLP_FINE)
    lru = _lru_stages(xp_ref, xc_ref, xn_ref, t_blk == 0, t_blk == blocks_per_row - 1,
                      (perm_ref, cw_ref, cb_ref, wg_ref, ba_ref, bx_ref, lam_ref),
                      (ext_ref, a_ref, u_ref, h_ref, carry_ref), True, pre, emit)
    next(lru)
    for _ in range(D_MODEL // MLP_FINE[0]):
        next(mlp)
    done = object()
    for _ in lru:
        next(mlp, done)
    for _ in mlp:
        pass
    acc = result[0]
    if final:
        acc = _rms(acc, fg_ref[...])
    o_ref[...] = acc


def _lru_weight_specs():
    return [_const_spec((CONV_WIDTH, LRU_WIDTH)), _const_spec((1, LRU_WIDTH)),
            _const_spec((LRU_BLOCKS, LRU_BLOCK_DIM, 2 * LRU_BLOCK_DIM)),
            _const_spec((1, LRU_WIDTH)), _const_spec((1, LRU_WIDTH)), _const_spec((1, LRU_WIDTH))]


def _lru_scratch():
    ts = LRU_TILE
    return [pltpu.VMEM((ts // SUBLANES + CONV_WIDTH - 1, SUBLANES, LRU_WIDTH), F32),
            pltpu.VMEM((ts, LRU_WIDTH), F32),
            pltpu.VMEM((ts, LRU_WIDTH), F32),
            pltpu.VMEM((ts, LRU_WIDTH), F32),
            pltpu.VMEM((1, LRU_WIDTH), F32)]


def _chunk_permutation(ts):
    n = ts // SUBLANES
    r = np.arange(ts)
    src = (r % SUBLANES) * n + r // SUBLANES
    return jnp.asarray(src[:, None] == np.arange(ts)[None, :], BF16)


def _lru_in_forward(x3, g, w_in, memkv, layer, perm, weights, batch, seq, blk):
    nt = seq // blk
    n_blocks = batch * nt
    d = x3.shape[-1]

    def cur_bt(k):
        kk = jnp.minimum(k, n_blocks - 1)
        return kk // nt, kk % nt

    def prev_bt(k):
        return cur_bt(jnp.maximum(k - 1, 0))

    return pl.pallas_call(
        functools.partial(_lru_in_fwd_kernel, blocks_per_row=nt),
        grid=(n_blocks + 1,),
        in_specs=[pl.BlockSpec((None, blk, d), lambda k: (*cur_bt(k), 0)),
                  _const_spec((1, d)), _const_spec((d, LRU_IN_COLS)),
                  pl.BlockSpec((None, MEM_LEN, MEM_W), lambda k: (cur_bt(k)[0], 0, 2 * layer)),
                  pl.BlockSpec((None, MEM_LEN, MEM_W), lambda k: (cur_bt(k)[0], 0, 2 * layer + 1)),
                  _const_spec((LRU_TILE, LRU_TILE))]
        + _lru_weight_specs(),
        out_specs=[pl.BlockSpec((None, blk, 2 * LRU_WIDTH), lambda k: (*cur_bt(k), 0)),
                   pl.BlockSpec((None, blk, MEM_W), lambda k: (*cur_bt(k), 0)),
                   pl.BlockSpec((None, blk, LRU_WIDTH), lambda k: (*prev_bt(k), 0))],
        out_shape=[jax.ShapeDtypeStruct((batch, seq, 2 * LRU_WIDTH), BF16),
                   jax.ShapeDtypeStruct((batch, seq, MEM_W), BF16),
                   jax.ShapeDtypeStruct((batch, seq, LRU_WIDTH), BF16)],
        scratch_shapes=[pltpu.VMEM((blk, LRU_WIDTH), BF16),
                        pltpu.VMEM((BF16_ROWS, LRU_WIDTH), BF16),
                        pltpu.VMEM((blk, MEM_W), BF16)]
        + _lru_scratch(),
        compiler_params=_params(1),
        name="lru_in_forward",
    )(x3, g, w_in, memkv, memkv, perm, *weights)


def _lru_reverse_mlp(x3, mo3, xg3, hf, perm, weights, wo, g, wup, wdn, fg, final, batch, seq, blk):
    nt = seq // blk
    n_blocks = batch * nt
    per = blk // BF16_ROWS
    last_halo = seq // BF16_ROWS - 1
    d = x3.shape[-1]

    def lru_bt(k):
        kk = jnp.minimum(k, n_blocks - 1)
        return kk // nt, nt - 1 - kk % nt

    def mlp_bt(k):
        return lru_bt(jnp.maximum(k - 1, 0))

    def at_mlp(col):
        return lambda k: (*mlp_bt(k), col)

    def at_lru(col):
        return lambda k: (*lru_bt(k), col)

    def prev_idx(k):
        b, t = lru_bt(k)
        return b, jnp.maximum(t * per - 1, 0), 0

    def next_idx(k):
        b, t = lru_bt(k)
        return b, jnp.minimum((t + 1) * per, last_halo), 0

    return pl.pallas_call(
        functools.partial(_lru_rev_mlp_kernel, blocks_per_row=nt, final=final),
        grid=(n_blocks + 1,),
        in_specs=[pl.BlockSpec((None, blk, d), at_mlp(0)),
                  pl.BlockSpec((None, blk, MEM_W), at_mlp(0)),
                  pl.BlockSpec((None, BF16_ROWS, LRU_WIDTH), prev_idx),
                  pl.BlockSpec((None, blk, LRU_WIDTH), at_lru(0)),
                  pl.BlockSpec((None, BF16_ROWS, LRU_WIDTH), next_idx),
                  pl.BlockSpec((None, blk, LRU_WIDTH), at_lru(1)),
                  pl.BlockSpec((None, blk, LRU_WIDTH), at_lru(0)),
                  _const_spec((LRU_TILE, LRU_TILE)), _const_spec((LRU_TILE, LRU_TILE))]
        + _lru_weight_specs()
        + [_const_spec(wo.shape), _const_spec((1, d)), _const_spec(wup.shape), _const_spec(wdn.shape),
           _const_spec((1, d))],
        out_specs=pl.BlockSpec((None, blk, d), at_mlp(0)),
        out_shape=jax.ShapeDtypeStruct(x3.shape, F32),
        scratch_shapes=_lru_scratch() + [pltpu.VMEM((blk, LRU_WIDTH), BF16)],
        compiler_params=_params(1),
        name="lru_reverse_mlp",
    )(x3, mo3, xg3, xg3, xg3, xg3, hf, perm, perm.T, *weights, wo, g, wup, wdn, fg)


def _dup_heads(w):
    d = w.shape[0]
    w4 = w.reshape(d, ATTN_KV_HEADS, 1, ATTN_HEAD_DIM)
    return jnp.broadcast_to(w4, (d, ATTN_KV_HEADS, 2, ATTN_HEAD_DIM)).reshape(d, KV_DUP_W)


def _rotary_tables(tile):
    half = ROPE_DIM // 2
    inv_freq = ROPE_THETA ** (-2.0 * jnp.arange(half, dtype=F32) / ROPE_DIM)
    freq = jnp.broadcast_to(inv_freq[:, None], (half, tile))
    d = np.arange(LANES) % ATTN_HEAD_DIM
    first = d < half
    second = (d >= half) & (d < ROPE_DIM)
    pick = np.arange(half)[:, None] == (d % half)[None, :]
    zeros8 = np.zeros((half, LANES), np.float32)
    cos_rows = np.where(pick & (d < ROPE_DIM)[None, :], 1.0, 0.0)
    lo_rows = np.where(pick & first[None, :], -1.0, 0.0)
    hi_rows = np.where(pick & second[None, :], 1.0, 0.0)
    ones_rows = np.zeros((BF16_ROWS, LANES), np.float32)
    ones_rows[0] = np.where(d < ROPE_DIM, 0.0, 1.0)
    zeros16 = np.zeros((BF16_ROWS, LANES), np.float32)
    cos_cols = np.concatenate([cos_rows, cos_rows, zeros8, zeros8, ones_rows], axis=0)
    lo_cols = np.concatenate([zeros8, zeros8, lo_rows, lo_rows, zeros16], axis=0)
    hi_cols = np.concatenate([zeros8, zeros8, hi_rows, hi_rows, zeros16], axis=0)
    return freq, jnp.asarray(np.concatenate([cos_cols, lo_cols, hi_cols], axis=1), BF16)


def _gate_weights(wa, wx):
    return (0.5 * jnp.concatenate([wa, wx], axis=-1)).astype(BF16)


def kernel(x, mem, positions, mix_norm, mlp_norm, mem_norm, final_norm, w_mem_kv, w_out, w_up, w_down,
           attn_w_in, attn_sinks, lru_w_in, lru_conv_w, lru_conv_b, lru_wa, lru_ba, lru_wx, lru_bx,
           lru_lambda):
    batch, seq, d = x.shape
    t = batch * seq
    depth = mix_norm.shape[0]
    xf = x.reshape(t, d)
    ones_d = jnp.ones((1, d), F32)

    w_kv_all = jnp.concatenate([w_mem_kv[l] for l in range(depth)], axis=1).astype(BF16)
    memkv = _norm_proj(mem.reshape(batch * MEM_LEN, d), mem_norm.reshape(1, d), w_kv_all,
                       tile=batch * MEM_LEN // 2)
    memkv = memkv.reshape(batch, MEM_LEN, depth * 2 * MEM_W)

    pos_rows = positions.reshape(t // PROJ_TILE, 1, PROJ_TILE).astype(jnp.int32)
    rot_freq, rot_expand = _rotary_tables(PROJ_TILE)

    for l in range(depth):
        kind, j = l % 2, l // 2
        g_mix = mix_norm[l].reshape(1, d)
        final = l == depth - 1
        mlp_weights = (w_out[l].astype(BF16), mlp_norm[l].reshape(1, d), w_up[l].astype(BF16),
                       w_down[l].astype(BF16), final_norm.reshape(1, d) if final else ones_d)
        if kind == 0:
            w = attn_w_in[j]
            w_in = jnp.concatenate([w[:, :Q_W], _dup_heads(w[:, Q_W:Q_W + KV_W]),
                                    _dup_heads(w[:, Q_W + KV_W:Q_W + 2 * KV_W]),
                                    w[:, Q_W + 2 * KV_W:]], axis=1).astype(BF16)
            p = _attn_in(xf, g_mix, pos_rows, rot_freq, rot_expand, w_in, PROJ_TILE)
            xf = _attention_mlp(xf.reshape(batch, seq, d), p, memkv, attn_sinks[j].astype(F32), l,
                                *mlp_weights, final, batch, seq, ATTN_TILE).reshape(t, d)
        else:
            x3 = xf.reshape(batch, seq, d)
            cw, cb = lru_conv_w[j], lru_conv_b[j].reshape(1, LRU_WIDTH)

            def dir_weights(dirn):
                return (cw, cb, _gate_weights(lru_wa[j, dirn], lru_wx[j, dirn]),
                        0.5 * lru_ba[j, dirn].reshape(1, LRU_WIDTH),
                        0.5 * lru_bx[j, dirn].reshape(1, LRU_WIDTH),
                        lru_lambda[j, dirn].reshape(1, LRU_WIDTH))

            perm = _chunk_permutation(LRU_TILE)
            xg, mo, hf = _lru_in_forward(x3, g_mix, lru_w_in[j].astype(BF16), memkv, l, perm,
                                         dir_weights(0), batch, seq, ROW_TILE)
            xf = _lru_reverse_mlp(x3, mo, xg, hf, perm, dir_weights(1),
                                  *mlp_weights, final, batch, seq, ROW_TILE).reshape(t, d)
    return xf.reshape(batch, seq, d)
```

```python
import functools

import jax
import jax.numpy as jnp
import numpy as np
from jax import lax
from jax.experimental import pallas as pl
from jax.experimental.pallas import tpu as pltpu

F32 = jnp.float32
BF16 = jnp.bfloat16

D_MODEL = 1024
EPS = 1e-6
ATTN_HEADS = 16
ATTN_KV_HEADS = 4
ATTN_HEAD_DIM = 64
ATTN_GROUP = 4
WINDOW = 128
BLOCK = 128
ROPE_THETA = 500000.0
ROPE_DIM = 16
Q_W = 1024
KV_W = 256
MEM_LEN = 256
MEM_HEADS = 4
MEM_HEAD_DIM = 128
MEM_W = 512
LRU_WIDTH = 1024
LRU_BLOCKS = 8
LRU_BLOCK_DIM = 128
LRU_C = 8.0
CONV_WIDTH = 4
D_FF = 4096
NEG = -1e30

LANES = 128
SUBLANES = 8
BF16_ROWS = 16
KV_DUP_W = 2 * KV_W
ATTN_IN_COLS = Q_W + KV_DUP_W + MEM_W + KV_W
LRU_IN_COLS = 2 * LRU_WIDTH + MEM_W
VMEM_LIMIT = 56 * 1024 * 1024

ROW_TILE = 512
LRU_TILE = 256
FF_CHUNK = 1024
PROJ_CHUNK = 512
PROJ_TILE = 1024
ATTN_TILE = 512
LOG2E = 1.4426950408889634
GELU_C0 = 0.7978845608028654
GELU_C1 = GELU_C0 * 0.044715


def _const_spec(shape):
    nd = len(shape)
    return pl.BlockSpec(shape, lambda *_: (0,) * nd, pipeline_mode=pl.Buffered(1))


def _params(n_axes):
    return pltpu.CompilerParams(dimension_semantics=("arbitrary",) * n_axes,
                                vmem_limit_bytes=VMEM_LIMIT)


def _rms(x, g):
    ms = jnp.mean(x * x, axis=-1, keepdims=True)
    return x * lax.rsqrt(ms + EPS) * g


def _norm_proj_kernel(x_ref, g_ref, w_ref, o_ref, *, chunk):
    xn = _rms(x_ref[...], g_ref[...]).astype(BF16)
    for c in range(0, o_ref.shape[-1], chunk):
        o_ref[:, c:c + chunk] = jnp.dot(xn, w_ref[:, c:c + chunk],
                                        preferred_element_type=F32).astype(BF16)


def _norm_proj(x, g, w, tile):
    t, d = x.shape
    n = w.shape[1]
    return pl.pallas_call(
        functools.partial(_norm_proj_kernel, chunk=PROJ_CHUNK),
        grid=(t // tile,),
        in_specs=[pl.BlockSpec((tile, d), lambda i: (i, 0)),
                  _const_spec((1, d)), _const_spec((d, n))],
        out_specs=pl.BlockSpec((tile, n), lambda i: (i, 0)),
        out_shape=jax.ShapeDtypeStruct((t, n), BF16),
        compiler_params=_params(1),
        name="norm_proj",
    )(x, g, w)


def _split_bf16(x):
    hi = x.astype(BF16).astype(F32)
    return hi, x - hi


def _attn_in_kernel(x_ref, g_ref, pos_ref, freq_ref, expand_ref, w_ref, o_ref):
    xn = _rms(x_ref[...], g_ref[...]).astype(BF16)
    ang = freq_ref[...] * pos_ref[...].astype(F32)
    cos_hi, cos_lo = _split_bf16(jnp.cos(ang))
    sin_hi_, sin_lo_ = _split_bf16(jnp.sin(ang))
    one = jnp.where(lax.broadcasted_iota(jnp.int32, (BF16_ROWS, ang.shape[1]), 0) == 0, 1.0, 0.0)
    trig = jnp.concatenate([cos_hi, cos_lo, sin_hi_, sin_lo_, one], axis=0).astype(BF16)
    tables = lax.dot_general(trig, expand_ref[...], (((0,), (0,)), ((), ())),
                             preferred_element_type=F32)
    cos = tables[:, 0:LANES]
    sin_lo = tables[:, LANES:2 * LANES]
    sin_hi = tables[:, 2 * LANES:3 * LANES]
    q_scale = LOG2E * ATTN_HEAD_DIM ** -0.5
    mem_scale = LOG2E * MEM_HEAD_DIM ** -0.5
    half = ROPE_DIM // 2
    rot_cols = Q_W + KV_DUP_W
    for c in range(0, ATTN_IN_COLS, 2 * LANES):
        acc = jnp.dot(xn, w_ref[:, c:c + 2 * LANES], preferred_element_type=F32)
        for s in range(2):
            col = c + s * LANES
            slab = acc[:, s * LANES:(s + 1) * LANES]
            if col < rot_cols:
                slab = (slab * cos + pltpu.roll(slab, LANES - half, 1) * sin_lo
                        + pltpu.roll(slab, half, 1) * sin_hi)
                if col < Q_W:
                    slab = slab * q_scale
            elif col < rot_cols + MEM_W:
                slab = slab * mem_scale
            o_ref[:, col:col + LANES] = slab.astype(BF16)


def _attn_in(x, g, pos, freq, expand, w, tile):
    t, d = x.shape
    n = w.shape[1]
    return pl.pallas_call(
        _attn_in_kernel,
        grid=(t // tile,),
        in_specs=[pl.BlockSpec((tile, d), lambda i: (i, 0)),
                  _const_spec((1, d)),
                  pl.BlockSpec((None, 1, tile), lambda i: (i, 0, 0)),
                  _const_spec(freq.shape), _const_spec(expand.shape), _const_spec((d, n))],
        out_specs=pl.BlockSpec((tile, n), lambda i: (i, 0)),
        out_shape=jax.ShapeDtypeStruct((t, n), BF16),
        compiler_params=_params(1),
        name="attn_in",
    )(x, g, pos, freq, expand, w)


def _score_ahead(items, scores, finish):
    pending = scores(items[0])
    yield "S"
    for n, item in enumerate(items):
        current = pending
        if n + 1 < len(items):
            pending = scores(items[n + 1])
            yield "S"
        finish(item, current)
        yield "F"


def _mem_attention_stages(mq_ref, mk_ref, mv_ref, mo_ref):
    rows = mq_ref.shape[0]

    def scores(item):
        r0, h = item
        sl = slice(h * MEM_HEAD_DIM, (h + 1) * MEM_HEAD_DIM)
        return lax.dot_general(mq_ref[r0:r0 + BLOCK, sl], mk_ref[:, sl], (((1,), (1,)), ((), ())),
                               preferred_element_type=F32)

    def finish(item, s):
        r0, h = item
        sl = slice(h * MEM_HEAD_DIM, (h + 1) * MEM_HEAD_DIM)
        m = jnp.max(s, axis=-1, keepdims=True)
        p = jnp.exp2(s - m)
        den = jnp.sum(p, axis=-1, keepdims=True)
        o = jnp.dot(p.astype(BF16), mv_ref[:, sl], preferred_element_type=F32)
        mo_ref[r0:r0 + BLOCK, sl] = (o / den).astype(BF16)

    return _score_ahead([(r0, h) for r0 in range(0, rows, BLOCK) for h in range(MEM_HEADS)], scores, finish)


def _mem_attention(mq_ref, mk_ref, mv_ref, mo_ref):
    for _ in _mem_attention_stages(mq_ref, mk_ref, mv_ref, mo_ref):
        pass


def _window_attention_stages(sink_ref, q_ref, k_refs, v_refs, o_ref, has_prev, has_next):
    nq = q_ref.shape[0] // BLOCK
    key = lax.broadcasted_iota(jnp.int32, (BLOCK, ATTN_GROUP * BLOCK), 0)
    qry = lax.broadcasted_iota(jnp.int32, (BLOCK, ATTN_GROUP * BLOCK), 1) % BLOCK
    band_prev = key >= qry
    band_next = key <= qry
    lo = lax.broadcasted_iota(jnp.int32, (BLOCK, LANES), 1) < ATTN_HEAD_DIM
    k_all = jnp.concatenate([r[...] for r in k_refs], axis=0)
    v_t = jnp.concatenate([r[...] for r in v_refs], axis=0).T
    zero = jnp.zeros((), BF16)
    def scores(item):
        j, h = item
        r0 = j * BLOCK
        kh = k_all[r0:r0 + 3 * BLOCK, h * LANES:(h + 1) * LANES]
        parts = []
        for g in range(ATTN_GROUP):
            slab = q_ref[r0:r0 + BLOCK, (2 * h + g // 2) * LANES:(2 * h + g // 2 + 1) * LANES]
            parts.append(jnp.where(lo if g % 2 == 0 else ~lo, slab, zero))
        qs = jnp.concatenate(parts, axis=0)
        return lax.dot_general(kh, qs, (((1,), (1,)), ((), ())),
                               preferred_element_type=F32)

    def finish(item, st):
        j, h = item
        r0 = j * BLOCK
        m_prev = band_prev if j > 0 else band_prev & has_prev
        m_next = band_next if j < nq - 1 else band_next & has_next
        vh_t = v_t[h * ATTN_HEAD_DIM:(h + 1) * ATTN_HEAD_DIM, r0:r0 + 3 * BLOCK]
        s0 = jnp.where(m_prev, st[0:BLOCK], NEG)
        s1 = st[BLOCK:2 * BLOCK]
        s2 = jnp.where(m_next, st[2 * BLOCK:3 * BLOCK], NEG)
        sink = jnp.concatenate(
            [jnp.full((1, BLOCK), sink_ref[ATTN_GROUP * h + g] * LOG2E, F32) for g in range(ATTN_GROUP)],
            axis=1)
        m = jnp.max(jnp.maximum(jnp.maximum(s0, s1), s2), axis=0, keepdims=True)
        m = jnp.maximum(m, sink)
        p0, p1, p2 = jnp.exp2(s0 - m), jnp.exp2(s1 - m), jnp.exp2(s2 - m)
        den = jnp.sum(p0 + p1 + p2, axis=0, keepdims=True) + jnp.exp2(sink - m)
        pt = jnp.concatenate([p0, p1, p2], axis=0).astype(BF16)
        ot = jnp.dot(vh_t, pt, preferred_element_type=F32) / den
        for pair in range(2):
            even = ot[:, (2 * pair) * BLOCK:(2 * pair + 1) * BLOCK]
            odd = ot[:, (2 * pair + 1) * BLOCK:(2 * pair + 2) * BLOCK]
            slab = jnp.concatenate([even, odd], axis=0).T
            o_ref[r0:r0 + BLOCK, (2 * h + pair) * LANES:(2 * h + pair + 1) * LANES] = slab.astype(BF16)

    return _score_ahead([(j, h) for j in range(nq) for h in range(ATTN_KV_HEADS)], scores, finish)


MLP_COARSE = (D_MODEL, FF_CHUNK, D_MODEL // 2)
MLP_FINE = (2 * LANES, 2 * LANES, 2 * LANES)


def _mlp_stages(x_ref, y_ref, mo_ref, wo_ref, g_ref, wup_ref, wdn_ref, result, widths=MLP_COARSE):
    proj_w, up_w, down_w = widths
    mix_w = y_ref.shape[-1]
    pieces = []
    for c in range(0, D_MODEL, proj_w):
        pieces.append(x_ref[:, c:c + proj_w]
                      + jnp.dot(y_ref[...], wo_ref[0:mix_w, c:c + proj_w], preferred_element_type=F32)
                      + jnp.dot(mo_ref[...], wo_ref[mix_w:, c:c + proj_w], preferred_element_type=F32))
        yield
    x1 = jnp.concatenate(pieces, axis=1)
    xn = _rms(x1, g_ref[...]).astype(BF16)
    hids = []
    for c in range(0, D_FF, up_w):
        hid = jnp.dot(xn, wup_ref[:, c:c + up_w], preferred_element_type=F32)
        hid = jnp.maximum(hid, 0.0)
        hids.append((hid * hid).astype(BF16))
        yield
    hid = jnp.concatenate(hids, axis=1)
    pieces = []
    for c in range(0, D_MODEL, down_w):
        pieces.append(x1[:, c:c + down_w]
                      + jnp.dot(hid, wdn_ref[:, c:c + down_w], preferred_element_type=F32))
        if c + down_w == D_MODEL:
            result.append(jnp.concatenate(pieces, axis=1))
        yield


def _attn_mlp_kernel(sink_ref, x_ref, q_ref, kp_ref, kc_ref, kn_ref, vp_ref, vc_ref, vn_ref,
                     mq_ref, mk_ref, mv_ref, wo_ref, g_ref, wup_ref, wdn_ref, fg_ref, o_ref,
                     y_ref, mo_ref, *, tiles_per_row, final):
    k = pl.program_id(0)
    kk = jnp.minimum(k, pl.num_programs(0) - 2)
    t_idx = kk % tiles_per_row

    @pl.when(k == 0)
    def _():
        y_ref[...] = jnp.zeros_like(y_ref)
        mo_ref[...] = jnp.zeros_like(mo_ref)

    def attention():
        yield from _window_attention_stages(sink_ref, q_ref, (kp_ref, kc_ref, kn_ref),
                                            (vp_ref, vc_ref, vn_ref), y_ref,
                                            t_idx > 0, t_idx < tiles_per_row - 1)
        yield from _mem_attention_stages(mq_ref, mk_ref, mv_ref, mo_ref)

    result = []
    attn = attention()
    mlp = _mlp_stages(x_ref, y_ref, mo_ref, wo_ref, g_ref, wup_ref, wdn_ref, result, MLP_FINE)
    next(attn)
    next(mlp)
    next(attn)
    for _ in range(D_MODEL // MLP_FINE[0] - 1):
        next(mlp)
    done = object()
    for tag in attn:
        if tag == "S":
            next(mlp, done)
    for _ in mlp:
        pass
    acc = result[0]
    if final:
        acc = _rms(acc, fg_ref[...])
    o_ref[...] = acc


def _attention_mlp(x3, p, memkv, sinks, layer, wo, g, wup, wdn, fg, final, batch, seq, tq):
    nt = seq // tq
    n_tiles = batch * nt
    per = tq // BLOCK
    nb = seq // BLOCK
    d = x3.shape[-1]
    p3 = p.reshape(batch, seq, ATTN_IN_COLS)
    kcol = Q_W // KV_DUP_W
    mqcol = (Q_W + KV_DUP_W) // MEM_W
    vcol = (Q_W + KV_DUP_W + MEM_W) // KV_W

    def attn_bt(k):
        kk = jnp.minimum(k, n_tiles - 1)
        return kk // nt, kk % nt

    def mlp_bt(k):
        return attn_bt(jnp.maximum(k - 1, 0))

    def kv_specs(colblk, width):
        def prev(k):
            b, i = attn_bt(k)
            return b, jnp.maximum(i * per - 1, 0), colblk

        def nxt(k):
            b, i = attn_bt(k)
            return b, jnp.minimum((i + 1) * per, nb - 1), colblk

        return [pl.BlockSpec((None, BLOCK, width), prev),
                pl.BlockSpec((None, tq, width), lambda k: (*attn_bt(k), colblk)),
                pl.BlockSpec((None, BLOCK, width), nxt)]

    return pl.pallas_call(
        functools.partial(_attn_mlp_kernel, tiles_per_row=nt, final=final),
        grid=(n_tiles + 1,),
        in_specs=[pl.BlockSpec(memory_space=pltpu.SMEM),
                  pl.BlockSpec((None, tq, d), lambda k: (*mlp_bt(k), 0)),
                  pl.BlockSpec((None, tq, Q_W), lambda k: (*attn_bt(k), 0))]
        + kv_specs(kcol, KV_DUP_W) + kv_specs(vcol, KV_W)
        + [pl.BlockSpec((None, tq, MEM_W), lambda k: (*attn_bt(k), mqcol)),
           pl.BlockSpec((None, MEM_LEN, MEM_W), lambda k: (attn_bt(k)[0], 0, 2 * layer)),
           pl.BlockSpec((None, MEM_LEN, MEM_W), lambda k: (attn_bt(k)[0], 0, 2 * layer + 1)),
           _const_spec(wo.shape), _const_spec((1, d)), _const_spec(wup.shape), _const_spec(wdn.shape),
           _const_spec((1, d))],
        out_specs=pl.BlockSpec((None, tq, d), lambda k: (*mlp_bt(k), 0)),
        out_shape=jax.ShapeDtypeStruct(x3.shape, F32),
        scratch_shapes=[pltpu.VMEM((tq, Q_W), BF16), pltpu.VMEM((tq, MEM_W), BF16)],
        compiler_params=_params(1),
        name="attention_mlp",
    )(sinks, x3, p3, p3, p3, p3, p3, p3, p3, p3, memkv, memkv, wo, g, wup, wdn, fg)


def _permuted_conv(xp, before, after, cw_ref, cb_ref, ext_ref):
    ts, w = xp.shape
    n = ts // SUBLANES
    xp = xp.reshape(n, SUBLANES, w)
    ext_ref[1:n + 1] = xp
    sub = lax.broadcasted_iota(jnp.int32, (SUBLANES, w), 0)
    ext_ref[0] = jnp.where(sub == 0, before, pltpu.roll(xp[n - 1], 1, 0))
    ext_ref[n + 1] = jnp.where(sub == SUBLANES - 1, after[0:1], pltpu.roll(xp[0], SUBLANES - 1, 0))
    ext_ref[n + 2] = jnp.where(sub == SUBLANES - 1, after[1:2], pltpu.roll(xp[1], SUBLANES - 1, 0))
    xc = cb_ref[...][None]
    for tap in range(CONV_WIDTH):
        xc = xc + ext_ref[tap:tap + n] * cw_ref[tap:tap + 1, :][None]
    return xc.reshape(ts, w)


def _lru_gate_block(blk, xc, xcb, wg_ref, ba_ref, bx_ref, lam_ref, a_ref, u_ref):
    sl = slice(blk * LRU_BLOCK_DIM, (blk + 1) * LRU_BLOCK_DIM)
    gates = jnp.dot(xcb[:, sl], wg_ref[blk], preferred_element_type=F32)
    lam = lam_ref[:, sl]
    softplus_neg_lam = jnp.maximum(-lam, 0.0) + jnp.log1p(jnp.exp(-jnp.abs(lam)))
    rate = (-0.5 * LRU_C * LOG2E) * softplus_neg_lam
    t_rec = jnp.tanh(gates[:, :LRU_BLOCK_DIM] + ba_ref[:, sl])
    t_in = jnp.tanh(gates[:, LRU_BLOCK_DIM:] + bx_ref[:, sl])
    a = jnp.exp2(rate * t_rec + rate)
    a_ref[:, sl] = a
    z = 1.0 - a * a
    root = jnp.where(z > 0.0, z * lax.rsqrt(z), 0.0)
    u_ref[:, sl] = root * ((0.5 * t_in + 0.5) * xc[:, sl])


def _lru_scan(a_ref, u_ref, h_ref, carry_ref, reverse):
    ts, w = a_ref.shape
    n = ts // SUBLANES
    order = range(n - 1, -1, -1) if reverse else range(n)

    def rows(ref, j):
        return ref[j * SUBLANES:(j + 1) * SUBLANES, :]

    local = jnp.zeros((SUBLANES, w), F32)
    prod = jnp.ones((SUBLANES, w), F32)
    for j in order:
        a = rows(a_ref, j)
        local = a * local + rows(u_ref, j)
        prod = a * prod
    c = carry_ref[...]
    inflow = [None] * SUBLANES
    for s in (range(SUBLANES - 1, -1, -1) if reverse else range(SUBLANES)):
        inflow[s] = c
        c = prod[s:s + 1] * c + local[s:s + 1]
    carry_ref[...] = c
    h = jnp.concatenate(inflow, axis=0)
    for j in order:
        h = rows(a_ref, j) * h + rows(u_ref, j)
        h_ref[j * SUBLANES:(j + 1) * SUBLANES, :] = h


def _lru_stages(prev_ref, cur_ref, next_ref, first, last, lru_refs, scratch, reverse, pre, emit):
    perm_ref, cw_ref, cb_ref, wg_ref, ba_ref, bx_ref, lam_ref = lru_refs
    ext_ref, a_ref, u_ref, h_ref, carry_ref = scratch
    rows = cur_ref.shape[0]
    starts = list(range(0, rows, LRU_TILE))
    starts = starts[::-1] if reverse else starts
    permuted = [(jnp.dot(perm_ref[...], cur_ref[r0:r0 + LRU_TILE, :], preferred_element_type=F32), pre(r0))
                for r0 in starts]
    yield
    for r0, (xp, aux) in zip(starts, permuted):
        if r0 == 0:
            before = jnp.where(first, 0.0, prev_ref[BF16_ROWS - 1:BF16_ROWS, :].astype(F32))
        else:
            before = cur_ref[r0 - 1:r0, :].astype(F32)
        if r0 + LRU_TILE == rows:
            after = jnp.where(last, 0.0, next_ref[0:2, :].astype(F32))
        else:
            after = cur_ref[r0 + LRU_TILE:r0 + LRU_TILE + 2, :].astype(F32)
        xc = _permuted_conv(xp, before, after, cw_ref, cb_ref, ext_ref)
        xcb = xc.astype(BF16)
        for blk in range(LRU_BLOCKS):
            _lru_gate_block(blk, xc, xcb, wg_ref, ba_ref, bx_ref, lam_ref, a_ref, u_ref)
            yield
        _lru_scan(a_ref, u_ref, h_ref, carry_ref, reverse)
        emit(r0, aux)
        yield


def _lru_in_stages(x_ref, g_ref, w_ref, mk_ref, mv_ref, xg_ref, mq_ref, mo_ref):
    xn = _rms(x_ref[...], g_ref[...]).astype(BF16)
    xg_w = xg_ref.shape[-1]
    piece = MLP_FINE[0]
    for c in range(0, LRU_IN_COLS, piece):
        acc = jnp.dot(xn, w_ref[:, c:c + piece], preferred_element_type=F32)
        if c < xg_w:
            xg_ref[:, c:c + piece] = acc.astype(BF16)
        else:
            mq_ref[:, c - xg_w:c - xg_w + piece] = (acc * (LOG2E * MEM_HEAD_DIM ** -0.5)).astype(BF16)
        yield
    yield from _mem_attention_stages(mq_ref, mk_ref, mv_ref, mo_ref)


def _lru_in_fwd_kernel(x_ref, g_ref, w_ref, mk_ref, mv_ref,
                       perm_ref, cw_ref, cb_ref, wg_ref, ba_ref, bx_ref, lam_ref,
                       xg_ref, mo_ref, hf_ref,
                       xprev_ref, tail_ref, mq_ref, ext_ref, a_ref, u_ref, h_ref, carry_ref,
                       *, blocks_per_row):
    k = pl.program_id(0)
    t_prev = jnp.maximum(k - 1, 0) % blocks_per_row

    @pl.when(k == 0)
    def _():
        xprev_ref[...] = jnp.zeros_like(xprev_ref)
        tail_ref[...] = jnp.zeros_like(tail_ref)

    @pl.when(t_prev == 0)
    def _():
        carry_ref[...] = jnp.zeros_like(carry_ref)

    def emit(r0, aux):
        hf_ref[r0:r0 + LRU_TILE, :] = h_ref[...].astype(BF16)

    proj = _lru_in_stages(x_ref, g_ref, w_ref, mk_ref, mv_ref, xg_ref, mq_ref, mo_ref)
    lru = _lru_stages(tail_ref, xprev_ref, xg_ref.at[:, 0:LRU_WIDTH],
                      t_prev == 0, t_prev == blocks_per_row - 1,
                      (perm_ref, cw_ref, cb_ref, wg_ref, ba_ref, bx_ref, lam_ref),
                      (ext_ref, a_ref, u_ref, h_ref, carry_ref), False, lambda r0: None, emit)
    done = object()
    next(lru)
    next(proj)
    traced = 1
    stages_per_tile = LRU_BLOCKS + 1
    for n, _ in enumerate(lru):
        if n == stages_per_tile - 1:
            assert traced * MLP_FINE[0] >= LRU_WIDTH
        if n % 2 == 1:
            next(proj, done)
            traced += 1
    for _ in proj:
        pass
    rows = xprev_ref.shape[0]
    tail_ref[...] = xprev_ref[rows - BF16_ROWS:rows, :]
    xprev_ref[...] = xg_ref[:, 0:LRU_WIDTH]


def _lru_rev_mlp_kernel(x_ref, mo_ref, xp_ref, xc_ref, xn_ref, gate_ref, hf_ref,
                        perm_ref, unperm_ref, cw_ref, cb_ref, wg_ref, ba_ref, bx_ref, lam_ref,
                        wo_ref, g_ref, wup_ref, wdn_ref, fg_ref, o_ref,
                        ext_ref, a_ref, u_ref, h_ref, carry_ref, y_ref, *, blocks_per_row, final):
    k = pl.program_id(0)
    kk = jnp.minimum(k, pl.num_programs(0) - 2)
    t_blk = blocks_per_row - 1 - kk % blocks_per_row

    @pl.when(k == 0)
    def _():
        y_ref[...] = jnp.zeros_like(y_ref)

    @pl.when(kk % blocks_per_row == 0)
    def _():
        carry_ref[...] = jnp.zeros_like(carry_ref)

    def pre(r0):
        return jnp.dot(perm_ref[...], gate_ref[r0:r0 + LRU_TILE, :], preferred_element_type=F32)

    def emit(r0, gate):
        inner = gate * (GELU_C0 + GELU_C1 * (gate * gate))
        hsum = hf_ref[r0:r0 + LRU_TILE, :].astype(F32) + h_ref[...]
        y = (hsum * gate) * (0.5 * jnp.tanh(inner) + 0.5)
        y_ref[r0:r0 + LRU_TILE, :] = jnp.dot(unperm_ref[...], y.astype(BF16),
                                              preferred_element_type=F32).astype(BF16)

    result = []
    mlp = _mlp_stages(x_ref, y_ref, mo_ref, wo_ref, g_ref, wup_ref, wdn_ref, result, MLP_FINE)
    lru = _lru_stages(xp_ref, xc_ref, xn_ref, t_blk == 0, t_blk == blocks_per_row - 1,
                      (perm_ref, cw_ref, cb_ref, wg_ref, ba_ref, bx_ref, lam_ref),
                      (ext_ref, a_ref, u_ref, h_ref, carry_ref), True, pre, emit)
    next(lru)
    for _ in range(D_MODEL // MLP_FINE[0]):
        next(mlp)
    done = object()
    for _ in lru:
        next(mlp, done)
    for _ in mlp:
        pass
    acc = result[0]
    if final:
        acc = _rms(acc, fg_ref[...])
    o_ref[...] = acc


def _lru_weight_specs():
    return [_const_spec((CONV_WIDTH, LRU_WIDTH)), _const_spec((1, LRU_WIDTH)),
            _const_spec((LRU_BLOCKS, LRU_BLOCK_DIM, 2 * LRU_BLOCK_DIM)),
            _const_spec((1, LRU_WIDTH)), _const_spec((1, LRU_WIDTH)), _const_spec((1, LRU_WIDTH))]


def _lru_scratch():
    ts = LRU_TILE
    return [pltpu.VMEM((ts // SUBLANES + CONV_WIDTH - 1, SUBLANES, LRU_WIDTH), F32),
            pltpu.VMEM((ts, LRU_WIDTH), F32),
            pltpu.VMEM((ts, LRU_WIDTH), F32),
            pltpu.VMEM((ts, LRU_WIDTH), F32),
            pltpu.VMEM((1, LRU_WIDTH), F32)]


def _chunk_permutation(ts):
    n = ts // SUBLANES
    r = np.arange(ts)
    src = (r % SUBLANES) * n + r // SUBLANES
    return jnp.asarray(src[:, None] == np.arange(ts)[None, :], BF16)


def _lru_in_forward(x3, g, w_in, memkv, layer, perm, weights, batch, seq, blk):
    nt = seq // blk
    n_blocks = batch * nt
    d = x3.shape[-1]

    def cur_bt(k):
        kk = jnp.minimum(k, n_blocks - 1)
        return kk // nt, kk % nt

    def prev_bt(k):
        return cur_bt(jnp.maximum(k - 1, 0))

    return pl.pallas_call(
        functools.partial(_lru_in_fwd_kernel, blocks_per_row=nt),
        grid=(n_blocks + 1,),
        in_specs=[pl.BlockSpec((None, blk, d), lambda k: (*cur_bt(k), 0)),
                  _const_spec((1, d)), _const_spec((d, LRU_IN_COLS)),
                  pl.BlockSpec((None, MEM_LEN, MEM_W), lambda k: (cur_bt(k)[0], 0, 2 * layer)),
                  pl.BlockSpec((None, MEM_LEN, MEM_W), lambda k: (cur_bt(k)[0], 0, 2 * layer + 1)),
                  _const_spec((LRU_TILE, LRU_TILE))]
        + _lru_weight_specs(),
        out_specs=[pl.BlockSpec((None, blk, 2 * LRU_WIDTH), lambda k: (*cur_bt(k), 0)),
                   pl.BlockSpec((None, blk, MEM_W), lambda k: (*cur_bt(k), 0)),
                   pl.BlockSpec((None, blk, LRU_WIDTH), lambda k: (*prev_bt(k), 0))],
        out_shape=[jax.ShapeDtypeStruct((batch, seq, 2 * LRU_WIDTH), BF16),
                   jax.ShapeDtypeStruct((batch, seq, MEM_W), BF16),
                   jax.ShapeDtypeStruct((batch, seq, LRU_WIDTH), BF16)],
        scratch_shapes=[pltpu.VMEM((blk, LRU_WIDTH), BF16),
                        pltpu.VMEM((BF16_ROWS, LRU_WIDTH), BF16),
                        pltpu.VMEM((blk, MEM_W), BF16)]
        + _lru_scratch(),
        compiler_params=_params(1),
        name="lru_in_forward",
    )(x3, g, w_in, memkv, memkv, perm, *weights)


def _lru_reverse_mlp(x3, mo3, xg3, hf, perm, weights, wo, g, wup, wdn, fg, final, batch, seq, blk):
    nt = seq // blk
    n_blocks = batch * nt
    per = blk // BF16_ROWS
    last_halo = seq // BF16_ROWS - 1
    d = x3.shape[-1]

    def lru_bt(k):
        kk = jnp.minimum(k, n_blocks - 1)
        return kk // nt, nt - 1 - kk % nt

    def mlp_bt(k):
        return lru_bt(jnp.maximum(k - 1, 0))

    def at_mlp(col):
        return lambda k: (*mlp_bt(k), col)

    def at_lru(col):
        return lambda k: (*lru_bt(k), col)

    def prev_idx(k):
        b, t = lru_bt(k)
        return b, jnp.maximum(t * per - 1, 0), 0

    def next_idx(k):
        b, t = lru_bt(k)
        return b, jnp.minimum((t + 1) * per, last_halo), 0

    return pl.pallas_call(
        functools.partial(_lru_rev_mlp_kernel, blocks_per_row=nt, final=final),
        grid=(n_blocks + 1,),
        in_specs=[pl.BlockSpec((None, blk, d), at_mlp(0)),
                  pl.BlockSpec((None, blk, MEM_W), at_mlp(0)),
                  pl.BlockSpec((None, BF16_ROWS, LRU_WIDTH), prev_idx),
                  pl.BlockSpec((None, blk, LRU_WIDTH), at_lru(0)),
                  pl.BlockSpec((None, BF16_ROWS, LRU_WIDTH), next_idx),
                  pl.BlockSpec((None, blk, LRU_WIDTH), at_lru(1)),
                  pl.BlockSpec((None, blk, LRU_WIDTH), at_lru(0)),
                  _const_spec((LRU_TILE, LRU_TILE)), _const_spec((LRU_TILE, LRU_TILE))]
        + _lru_weight_specs()
        + [_const_spec(wo.shape), _const_spec((1, d)), _const_spec(wup.shape), _const_spec(wdn.shape),
           _const_spec((1, d))],
        out_specs=pl.BlockSpec((None, blk, d), at_mlp(0)),
        out_shape=jax.ShapeDtypeStruct(x3.shape, F32),
        scratch_shapes=_lru_scratch() + [pltpu.VMEM((blk, LRU_WIDTH), BF16)],
        compiler_params=_params(1),
        name="lru_reverse_mlp",
    )(x3, mo3, xg3, xg3, xg3, xg3, hf, perm, perm.T, *weights, wo, g, wup, wdn, fg)


def _dup_heads(w):
    d = w.shape[0]
    w4 = w.reshape(d, ATTN_KV_HEADS, 1, ATTN_HEAD_DIM)
    return jnp.broadcast_to(w4, (d, ATTN_KV_HEADS, 2, ATTN_HEAD_DIM)).reshape(d, KV_DUP_W)


def _rotary_tables(tile):
    half = ROPE_DIM // 2
    inv_freq = ROPE_THETA ** (-2.0 * jnp.arange(half, dtype=F32) / ROPE_DIM)
    freq = jnp.broadcast_to(inv_freq[:, None], (half, tile))
    d = np.arange(LANES) % ATTN_HEAD_DIM
    first = d < half
    second = (d >= half) & (d < ROPE_DIM)
    pick = np.arange(half)[:, None] == (d % half)[None, :]
    zeros8 = np.zeros((half, LANES), np.float32)
    cos_rows = np.where(pick & (d < ROPE_DIM)[None, :], 1.0, 0.0)
    lo_rows = np.where(pick & first[None, :], -1.0, 0.0)
    hi_rows = np.where(pick & second[None, :], 1.0, 0.0)
    ones_rows = np.zeros((BF16_ROWS, LANES), np.float32)
    ones_rows[0] = np.where(d < ROPE_DIM, 0.0, 1.0)
    zeros16 = np.zeros((BF16_ROWS, LANES), np.float32)
    cos_cols = np.concatenate([cos_rows, cos_rows, zeros8, zeros8, ones_rows], axis=0)
    lo_cols = np.concatenate([zeros8, zeros8, lo_rows, lo_rows, zeros16], axis=0)
    hi_cols = np.concatenate([zeros8, zeros8, hi_rows, hi_rows, zeros16], axis=0)
    return freq, jnp.asarray(np.concatenate([cos_cols, lo_cols, hi_cols], axis=1), BF16)


def _gate_weights(wa, wx):
    return (0.5 * jnp.concatenate([wa, wx], axis=-1)).astype(BF16)


def kernel(x, mem, positions, mix_norm, mlp_norm, mem_norm, final_norm, w_mem_kv, w_out, w_up, w_down,
           attn_w_in, attn_sinks, lru_w_in, lru_conv_w, lru_conv_b, lru_wa, lru_ba, lru_wx, lru_bx,
           lru_lambda):
    batch, seq, d = x.shape
    t = batch * seq
    depth = mix_norm.shape[0]
    xf = x.reshape(t, d)
    ones_d = jnp.ones((1, d), F32)

    w_kv_all = jnp.concatenate([w_mem_kv[l] for l in range(depth)], axis=1).astype(BF16)
    memkv = _norm_proj(mem.reshape(batch * MEM_LEN, d), mem_norm.reshape(1, d), w_kv_all,
                       tile=batch * MEM_LEN // 2)
    memkv = memkv.reshape(batch, MEM_LEN, depth * 2 * MEM_W)

    pos_rows = positions.reshape(t // PROJ_TILE, 1, PROJ_TILE).astype(jnp.int32)
    rot_freq, rot_expand = _rotary_tables(PROJ_TILE)

    for l in range(depth):
        kind, j = l % 2, l // 2
        g_mix = mix_norm[l].reshape(1, d)
        final = l == depth - 1
        mlp_weights = (w_out[l].astype(BF16), mlp_norm[l].reshape(1, d), w_up[l].astype(BF16),
                       w_down[l].astype(BF16), final_norm.reshape(1, d) if final else ones_d)
        if kind == 0:
            w = attn_w_in[j]
            w_in = jnp.concatenate([w[:, :Q_W], _dup_heads(w[:, Q_W:Q_W + KV_W]),
                                    w[:, Q_W + 2 * KV_W:], w[:, Q_W + KV_W:Q_W + 2 * KV_W]],
                                   axis=1).astype(BF16)
            p = _attn_in(xf, g_mix, pos_rows, rot_freq, rot_expand, w_in, PROJ_TILE)
            xf = _attention_mlp(xf.reshape(batch, seq, d), p, memkv, attn_sinks[j].astype(F32), l,
                                *mlp_weights, final, batch, seq, ATTN_TILE).reshape(t, d)
        else:
            x3 = xf.reshape(batch, seq, d)
            cw, cb = lru_conv_w[j], lru_conv_b[j].reshape(1, LRU_WIDTH)

            def dir_weights(dirn):
                return (cw, cb, _gate_weights(lru_wa[j, dirn], lru_wx[j, dirn]),
                        0.5 * lru_ba[j, dirn].reshape(1, LRU_WIDTH),
                        0.5 * lru_bx[j, dirn].reshape(1, LRU_WIDTH),
                        lru_lambda[j, dirn].reshape(1, LRU_WIDTH))

            perm = _chunk_permutation(LRU_TILE)
            xg, mo, hf = _lru_in_forward(x3, g_mix, lru_w_in[j].astype(BF16), memkv, l, perm,
                                         dir_weights(0), batch, seq, ROW_TILE)
            xf = _lru_reverse_mlp(x3, mo, xg, hf, perm, dir_weights(1),
                                  *mlp_weights, final, batch, seq, ROW_TILE).reshape(t, d)
    return xf.reshape(batch, seq, d)
```

```python
import functools

import jax
import jax.numpy as jnp
import numpy as np
from jax import lax
from jax.experimental import pallas as pl
from jax.experimental.pallas import tpu as pltpu

F32 = jnp.float32
BF16 = jnp.bfloat16

D_MODEL = 1024
EPS = 1e-6
ATTN_HEADS = 16
ATTN_KV_HEADS = 4
ATTN_HEAD_DIM = 64
ATTN_GROUP = 4
WINDOW = 128
BLOCK = 128
ROPE_THETA = 500000.0
ROPE_DIM = 16
Q_W = 1024
KV_W = 256
MEM_LEN = 256
MEM_HEADS = 4
MEM_HEAD_DIM = 128
MEM_W = 512
LRU_WIDTH = 1024
LRU_BLOCKS = 8
LRU_BLOCK_DIM = 128
LRU_C = 8.0
CONV_WIDTH = 4
D_FF = 4096
NEG = -1e30

LANES = 128
SUBLANES = 8
BF16_ROWS = 16
KV_DUP_W = 2 * KV_W
ATTN_IN_COLS = Q_W + 2 * KV_DUP_W + MEM_W
LRU_IN_COLS = 2 * LRU_WIDTH + MEM_W
VMEM_LIMIT = 56 * 1024 * 1024

ROW_TILE = 512
LRU_TILE = 256
FF_CHUNK = 1024
PROJ_CHUNK = 512
PROJ_TILE = 1024
ATTN_TILE = 512
LOG2E = 1.4426950408889634
GELU_C0 = 0.7978845608028654
GELU_C1 = GELU_C0 * 0.044715


def _const_spec(shape):
    nd = len(shape)
    return pl.BlockSpec(shape, lambda *_: (0,) * nd, pipeline_mode=pl.Buffered(1))


def _params(n_axes):
    return pltpu.CompilerParams(dimension_semantics=("arbitrary",) * n_axes,
                                vmem_limit_bytes=VMEM_LIMIT)


def _rms(x, g):
    ms = jnp.mean(x * x, axis=-1, keepdims=True)
    return x * lax.rsqrt(ms + EPS) * g


def _norm_proj_kernel(x_ref, g_ref, w_ref, o_ref, *, chunk):
    xn = _rms(x_ref[...], g_ref[...]).astype(BF16)
    for c in range(0, o_ref.shape[-1], chunk):
        o_ref[:, c:c + chunk] = jnp.dot(xn, w_ref[:, c:c + chunk],
                                        preferred_element_type=F32).astype(BF16)


def _norm_proj(x, g, w, tile):
    t, d = x.shape
    n = w.shape[1]
    return pl.pallas_call(
        functools.partial(_norm_proj_kernel, chunk=PROJ_CHUNK),
        grid=(t // tile,),
        in_specs=[pl.BlockSpec((tile, d), lambda i: (i, 0)),
                  _const_spec((1, d)), _const_spec((d, n))],
        out_specs=pl.BlockSpec((tile, n), lambda i: (i, 0)),
        out_shape=jax.ShapeDtypeStruct((t, n), BF16),
        compiler_params=_params(1),
        name="norm_proj",
    )(x, g, w)


def _split_bf16(x):
    hi = x.astype(BF16).astype(F32)
    return hi, x - hi


def _attn_in_kernel(x_ref, g_ref, pos_ref, freq_ref, expand_ref, w_ref, o_ref):
    xn = _rms(x_ref[...], g_ref[...]).astype(BF16)
    ang = freq_ref[...] * pos_ref[...].astype(F32)
    cos_hi, cos_lo = _split_bf16(jnp.cos(ang))
    sin_hi_, sin_lo_ = _split_bf16(jnp.sin(ang))
    one = jnp.where(lax.broadcasted_iota(jnp.int32, (BF16_ROWS, ang.shape[1]), 0) == 0, 1.0, 0.0)
    trig = jnp.concatenate([cos_hi, cos_lo, sin_hi_, sin_lo_, one], axis=0).astype(BF16)
    tables = lax.dot_general(trig, expand_ref[...], (((0,), (0,)), ((), ())),
                             preferred_element_type=F32)
    cos = tables[:, 0:LANES]
    sin_lo = tables[:, LANES:2 * LANES]
    sin_hi = tables[:, 2 * LANES:3 * LANES]
    q_scale = LOG2E * ATTN_HEAD_DIM ** -0.5
    mem_scale = LOG2E * MEM_HEAD_DIM ** -0.5
    half = ROPE_DIM // 2
    rot_cols = Q_W + KV_DUP_W
    for c in range(0, ATTN_IN_COLS, 2 * LANES):
        acc = jnp.dot(xn, w_ref[:, c:c + 2 * LANES], preferred_element_type=F32)
        for s in range(2):
            col = c + s * LANES
            slab = acc[:, s * LANES:(s + 1) * LANES]
            if col < rot_cols:
                slab = (slab * cos + pltpu.roll(slab, LANES - half, 1) * sin_lo
                        + pltpu.roll(slab, half, 1) * sin_hi)
                if col < Q_W:
                    slab = slab * q_scale
            elif col >= Q_W + 2 * KV_DUP_W:
                slab = slab * mem_scale
            o_ref[:, col:col + LANES] = slab.astype(BF16)


def _attn_in(x, g, pos, freq, expand, w, tile):
    t, d = x.shape
    n = w.shape[1]
    return pl.pallas_call(
        _attn_in_kernel,
        grid=(t // tile,),
        in_specs=[pl.BlockSpec((tile, d), lambda i: (i, 0)),
                  _const_spec((1, d)),
                  pl.BlockSpec((None, 1, tile), lambda i: (i, 0, 0)),
                  _const_spec(freq.shape), _const_spec(expand.shape), _const_spec((d, n))],
        out_specs=pl.BlockSpec((tile, n), lambda i: (i, 0)),
        out_shape=jax.ShapeDtypeStruct((t, n), BF16),
        compiler_params=_params(1),
        name="attn_in",
    )(x, g, pos, freq, expand, w)


def _score_ahead(items, scores, finish):
    pending = scores(items[0])
    yield "S"
    for n, item in enumerate(items):
        current = pending
        if n + 1 < len(items):
            pending = scores(items[n + 1])
            yield "S"
        finish(item, current)
        yield "F"


def _mem_attention_stages(mq_ref, mk_ref, mv_ref, mo_ref):
    rows = mq_ref.shape[0]

    def scores(item):
        r0, h = item
        sl = slice(h * MEM_HEAD_DIM, (h + 1) * MEM_HEAD_DIM)
        return lax.dot_general(mq_ref[r0:r0 + BLOCK, sl], mk_ref[:, sl], (((1,), (1,)), ((), ())),
                               preferred_element_type=F32)

    def finish(item, s):
        r0, h = item
        sl = slice(h * MEM_HEAD_DIM, (h + 1) * MEM_HEAD_DIM)
        m = jnp.max(s, axis=-1, keepdims=True)
        p = jnp.exp2(s - m)
        den = jnp.sum(p, axis=-1, keepdims=True)
        o = jnp.dot(p.astype(BF16), mv_ref[:, sl], preferred_element_type=F32)
        mo_ref[r0:r0 + BLOCK, sl] = (o / den).astype(BF16)

    return _score_ahead([(r0, h) for r0 in range(0, rows, BLOCK) for h in range(MEM_HEADS)], scores, finish)


def _mem_attention(mq_ref, mk_ref, mv_ref, mo_ref):
    for _ in _mem_attention_stages(mq_ref, mk_ref, mv_ref, mo_ref):
        pass


def _window_attention_stages(sink_ref, q_ref, k_refs, v_refs, o_ref, has_prev, has_next):
    nq = q_ref.shape[0] // BLOCK
    key = lax.broadcasted_iota(jnp.int32, (BLOCK, ATTN_GROUP * BLOCK), 0)
    qry = lax.broadcasted_iota(jnp.int32, (BLOCK, ATTN_GROUP * BLOCK), 1) % BLOCK
    band_prev = key >= qry
    band_next = key <= qry
    lo = lax.broadcasted_iota(jnp.int32, (BLOCK, LANES), 1) < ATTN_HEAD_DIM
    top = lax.broadcasted_iota(jnp.int32, (LANES, BLOCK), 0) < ATTN_HEAD_DIM
    k_all = jnp.concatenate([r[...] for r in k_refs], axis=0)
    v_all = jnp.concatenate([r[...] for r in v_refs], axis=0)
    zero = jnp.zeros((), BF16)
    def scores(item):
        j, h = item
        r0 = j * BLOCK
        kh = k_all[r0:r0 + 3 * BLOCK, h * LANES:(h + 1) * LANES]
        parts = []
        for g in range(ATTN_GROUP):
            slab = q_ref[r0:r0 + BLOCK, (2 * h + g // 2) * LANES:(2 * h + g // 2 + 1) * LANES]
            parts.append(jnp.where(lo if g % 2 == 0 else ~lo, slab, zero))
        qs = jnp.concatenate(parts, axis=0)
        return lax.dot_general(kh, qs, (((1,), (1,)), ((), ())),
                               preferred_element_type=F32)

    def finish(item, st):
        j, h = item
        r0 = j * BLOCK
        m_prev = band_prev if j > 0 else band_prev & has_prev
        m_next = band_next if j < nq - 1 else band_next & has_next
        vh = v_all[r0:r0 + 3 * BLOCK, h * LANES:(h + 1) * LANES]
        s0 = jnp.where(m_prev, st[0:BLOCK], NEG)
        s1 = st[BLOCK:2 * BLOCK]
        s2 = jnp.where(m_next, st[2 * BLOCK:3 * BLOCK], NEG)
        sink = jnp.concatenate(
            [jnp.full((1, BLOCK), sink_ref[ATTN_GROUP * h + g] * LOG2E, F32) for g in range(ATTN_GROUP)],
            axis=1)
        m = jnp.max(jnp.maximum(jnp.maximum(s0, s1), s2), axis=0, keepdims=True)
        m = jnp.maximum(m, sink)
        p0, p1, p2 = jnp.exp2(s0 - m), jnp.exp2(s1 - m), jnp.exp2(s2 - m)
        den = jnp.sum(p0 + p1 + p2, axis=0, keepdims=True) + jnp.exp2(sink - m)
        pt = jnp.concatenate([p0, p1, p2], axis=0).astype(BF16)
        ot = lax.dot_general(vh, pt, (((0,), (0,)), ((), ())),
                             preferred_element_type=F32)
        ot = ot / den
        for pair in range(2):
            even = ot[:, (2 * pair) * BLOCK:(2 * pair + 1) * BLOCK]
            odd = ot[:, (2 * pair + 1) * BLOCK:(2 * pair + 2) * BLOCK]
            slab = jnp.where(top, even, odd).T
            o_ref[r0:r0 + BLOCK, (2 * h + pair) * LANES:(2 * h + pair + 1) * LANES] = slab.astype(BF16)

    return _score_ahead([(j, h) for j in range(nq) for h in range(ATTN_KV_HEADS)], scores, finish)


MLP_COARSE = (D_MODEL, FF_CHUNK, D_MODEL // 2)
MLP_FINE = (2 * LANES, 2 * LANES, 2 * LANES)


def _mlp_stages(x_ref, y_ref, mo_ref, wo_ref, g_ref, wup_ref, wdn_ref, result, widths=MLP_COARSE):
    proj_w, up_w, down_w = widths
    mix_w = y_ref.shape[-1]
    pieces = []
    for c in range(0, D_MODEL, proj_w):
        pieces.append(x_ref[:, c:c + proj_w]
                      + jnp.dot(y_ref[...], wo_ref[0:mix_w, c:c + proj_w], preferred_element_type=F32)
                      + jnp.dot(mo_ref[...], wo_ref[mix_w:, c:c + proj_w], preferred_element_type=F32))
        yield
    x1 = jnp.concatenate(pieces, axis=1)
    xn = _rms(x1, g_ref[...]).astype(BF16)
    hids = []
    for c in range(0, D_FF, up_w):
        hid = jnp.dot(xn, wup_ref[:, c:c + up_w], preferred_element_type=F32)
        hid = jnp.maximum(hid, 0.0)
        hids.append((hid * hid).astype(BF16))
        yield
    hid = jnp.concatenate(hids, axis=1)
    pieces = []
    for c in range(0, D_MODEL, down_w):
        pieces.append(x1[:, c:c + down_w]
                      + jnp.dot(hid, wdn_ref[:, c:c + down_w], preferred_element_type=F32))
        if c + down_w == D_MODEL:
            result.append(jnp.concatenate(pieces, axis=1))
        yield


def _attn_mlp_kernel(sink_ref, x_ref, q_ref, kvp_ref, kvc_ref, kvn_ref,
                     mq_ref, mk_ref, mv_ref, wo_ref, g_ref, wup_ref, wdn_ref, fg_ref, o_ref,
                     y_ref, mo_ref, *, tiles_per_row, final):
    k = pl.program_id(0)
    kk = jnp.minimum(k, pl.num_programs(0) - 2)
    t_idx = kk % tiles_per_row

    @pl.when(k == 0)
    def _():
        y_ref[...] = jnp.zeros_like(y_ref)
        mo_ref[...] = jnp.zeros_like(mo_ref)

    def attention():
        kv_refs = (kvp_ref, kvc_ref, kvn_ref)
        yield from _window_attention_stages(sink_ref, q_ref,
                                            [r.at[:, 0:KV_DUP_W] for r in kv_refs],
                                            [r.at[:, KV_DUP_W:2 * KV_DUP_W] for r in kv_refs], y_ref,
                                            t_idx > 0, t_idx < tiles_per_row - 1)
        yield from _mem_attention_stages(mq_ref, mk_ref, mv_ref, mo_ref)

    result = []
    attn = attention()
    mlp = _mlp_stages(x_ref, y_ref, mo_ref, wo_ref, g_ref, wup_ref, wdn_ref, result, MLP_FINE)
    next(attn)
    next(mlp)
    next(attn)
    for _ in range(D_MODEL // MLP_FINE[0] - 1):
        next(mlp)
    done = object()
    for tag in attn:
        if tag == "S":
            next(mlp, done)
    for _ in mlp:
        pass
    acc = result[0]
    if final:
        acc = _rms(acc, fg_ref[...])
    o_ref[...] = acc


def _attention_mlp(x3, p, memkv, sinks, layer, wo, g, wup, wdn, fg, final, batch, seq, tq):
    nt = seq // tq
    n_tiles = batch * nt
    per = tq // BLOCK
    nb = seq // BLOCK
    d = x3.shape[-1]
    p3 = p.reshape(batch, seq, ATTN_IN_COLS)
    kv_w = 2 * KV_DUP_W
    kvcol = Q_W // kv_w
    mqcol = (Q_W + kv_w) // MEM_W

    def attn_bt(k):
        kk = jnp.minimum(k, n_tiles - 1)
        return kk // nt, kk % nt

    def mlp_bt(k):
        return attn_bt(jnp.maximum(k - 1, 0))

    def kv_prev(k):
        b, i = attn_bt(k)
        return b, jnp.maximum(i * per - 1, 0), kvcol

    def kv_next(k):
        b, i = attn_bt(k)
        return b, jnp.minimum((i + 1) * per, nb - 1), kvcol

    kv_specs = [pl.BlockSpec((None, BLOCK, kv_w), kv_prev),
                pl.BlockSpec((None, tq, kv_w), lambda k: (*attn_bt(k), kvcol)),
                pl.BlockSpec((None, BLOCK, kv_w), kv_next)]

    return pl.pallas_call(
        functools.partial(_attn_mlp_kernel, tiles_per_row=nt, final=final),
        grid=(n_tiles + 1,),
        in_specs=[pl.BlockSpec(memory_space=pltpu.SMEM),
                  pl.BlockSpec((None, tq, d), lambda k: (*mlp_bt(k), 0)),
                  pl.BlockSpec((None, tq, Q_W), lambda k: (*attn_bt(k), 0))]
        + kv_specs
        + [pl.BlockSpec((None, tq, MEM_W), lambda k: (*attn_bt(k), mqcol)),
           pl.BlockSpec((None, MEM_LEN, MEM_W), lambda k: (attn_bt(k)[0], 0, 2 * layer)),
           pl.BlockSpec((None, MEM_LEN, MEM_W), lambda k: (attn_bt(k)[0], 0, 2 * layer + 1)),
           _const_spec(wo.shape), _const_spec((1, d)), _const_spec(wup.shape), _const_spec(wdn.shape),
           _const_spec((1, d))],
        out_specs=pl.BlockSpec((None, tq, d), lambda k: (*mlp_bt(k), 0)),
        out_shape=jax.ShapeDtypeStruct(x3.shape, F32),
        scratch_shapes=[pltpu.VMEM((tq, Q_W), BF16), pltpu.VMEM((tq, MEM_W), BF16)],
        compiler_params=_params(1),
        name="attention_mlp",
    )(sinks, x3, p3, p3, p3, p3, p3, memkv, memkv, wo, g, wup, wdn, fg)


def _permuted_conv(xp, before, after, cw_ref, cb_ref, ext_ref):
    ts, w = xp.shape
    n = ts // SUBLANES
    xp = xp.reshape(n, SUBLANES, w)
    ext_ref[1:n + 1] = xp
    sub = lax.broadcasted_iota(jnp.int32, (SUBLANES, w), 0)
    ext_ref[0] = jnp.where(sub == 0, before, pltpu.roll(xp[n - 1], 1, 0))
    ext_ref[n + 1] = jnp.where(sub == SUBLANES - 1, after[0:1], pltpu.roll(xp[0], SUBLANES - 1, 0))
    ext_ref[n + 2] = jnp.where(sub == SUBLANES - 1, after[1:2], pltpu.roll(xp[1], SUBLANES - 1, 0))
    xc = cb_ref[...][None]
    for tap in range(CONV_WIDTH):
        xc = xc + ext_ref[tap:tap + n] * cw_ref[tap:tap + 1, :][None]
    return xc.reshape(ts, w)


def _lru_gate_block(blk, xc, xcb, wg_ref, ba_ref, bx_ref, lam_ref, a_ref, u_ref):
    sl = slice(blk * LRU_BLOCK_DIM, (blk + 1) * LRU_BLOCK_DIM)
    gates = jnp.dot(xcb[:, sl], wg_ref[blk], preferred_element_type=F32)
    lam = lam_ref[:, sl]
    softplus_neg_lam = jnp.maximum(-lam, 0.0) + jnp.log1p(jnp.exp(-jnp.abs(lam)))
    rate = (-0.5 * LRU_C * LOG2E) * softplus_neg_lam
    t_rec = jnp.tanh(gates[:, :LRU_BLOCK_DIM] + ba_ref[:, sl])
    t_in = jnp.tanh(gates[:, LRU_BLOCK_DIM:] + bx_ref[:, sl])
    a = jnp.exp2(rate * t_rec + rate)
    a_ref[:, sl] = a
    z = 1.0 - a * a
    root = jnp.where(z > 0.0, z * lax.rsqrt(z), 0.0)
    u_ref[:, sl] = root * ((0.5 * t_in + 0.5) * xc[:, sl])


def _lru_scan(a_ref, u_ref, h_ref, carry_ref, reverse):
    ts, w = a_ref.shape
    n = ts // SUBLANES
    order = range(n - 1, -1, -1) if reverse else range(n)

    def rows(ref, j):
        return ref[j * SUBLANES:(j + 1) * SUBLANES, :]

    local = jnp.zeros((SUBLANES, w), F32)
    prod = jnp.ones((SUBLANES, w), F32)
    for j in order:
        a = rows(a_ref, j)
        local = a * local + rows(u_ref, j)
        prod = a * prod
    c = carry_ref[...]
    inflow = [None] * SUBLANES
    for s in (range(SUBLANES - 1, -1, -1) if reverse else range(SUBLANES)):
        inflow[s] = c
        c = prod[s:s + 1] * c + local[s:s + 1]
    carry_ref[...] = c
    h = jnp.concatenate(inflow, axis=0)
    for j in order:
        h = rows(a_ref, j) * h + rows(u_ref, j)
        h_ref[j * SUBLANES:(j + 1) * SUBLANES, :] = h


def _lru_stages(prev_ref, cur_ref, next_ref, first, last, lru_refs, scratch, reverse, pre, emit):
    perm_ref, cw_ref, cb_ref, wg_ref, ba_ref, bx_ref, lam_ref = lru_refs
    ext_ref, a_ref, u_ref, h_ref, carry_ref = scratch
    rows = cur_ref.shape[0]
    starts = list(range(0, rows, LRU_TILE))
    starts = starts[::-1] if reverse else starts
    permuted = [(jnp.dot(perm_ref[...], cur_ref[r0:r0 + LRU_TILE, :], preferred_element_type=F32), pre(r0))
                for r0 in starts]
    yield
    for r0, (xp, aux) in zip(starts, permuted):
        if r0 == 0:
            before = jnp.where(first, 0.0, prev_ref[BF16_ROWS - 1:BF16_ROWS, :].astype(F32))
        else:
            before = cur_ref[r0 - 1:r0, :].astype(F32)
        if r0 + LRU_TILE == rows:
            after = jnp.where(last, 0.0, next_ref[0:2, :].astype(F32))
        else:
            after = cur_ref[r0 + LRU_TILE:r0 + LRU_TILE + 2, :].astype(F32)
        xc = _permuted_conv(xp, before, after, cw_ref, cb_ref, ext_ref)
        xcb = xc.astype(BF16)
        for blk in range(LRU_BLOCKS):
            _lru_gate_block(blk, xc, xcb, wg_ref, ba_ref, bx_ref, lam_ref, a_ref, u_ref)
            yield
        _lru_scan(a_ref, u_ref, h_ref, carry_ref, reverse)
        emit(r0, aux)
        yield


def _lru_in_stages(x_ref, g_ref, w_ref, mk_ref, mv_ref, xg_ref, mq_ref, mo_ref):
    xn = _rms(x_ref[...], g_ref[...]).astype(BF16)
    xg_w = xg_ref.shape[-1]
    piece = MLP_FINE[0]
    for c in range(0, LRU_IN_COLS, piece):
        acc = jnp.dot(xn, w_ref[:, c:c + piece], preferred_element_type=F32)
        if c < xg_w:
            xg_ref[:, c:c + piece] = acc.astype(BF16)
        else:
            mq_ref[:, c - xg_w:c - xg_w + piece] = (acc * (LOG2E * MEM_HEAD_DIM ** -0.5)).astype(BF16)
        yield
    yield from _mem_attention_stages(mq_ref, mk_ref, mv_ref, mo_ref)


def _lru_in_fwd_kernel(x_ref, g_ref, w_ref, mk_ref, mv_ref,
                       perm_ref, cw_ref, cb_ref, wg_ref, ba_ref, bx_ref, lam_ref,
                       xg_ref, mo_ref, hf_ref,
                       xprev_ref, tail_ref, mq_ref, ext_ref, a_ref, u_ref, h_ref, carry_ref,
                       *, blocks_per_row):
    k = pl.program_id(0)
    t_prev = jnp.maximum(k - 1, 0) % blocks_per_row

    @pl.when(k == 0)
    def _():
        xprev_ref[...] = jnp.zeros_like(xprev_ref)
        tail_ref[...] = jnp.zeros_like(tail_ref)

    @pl.when(t_prev == 0)
    def _():
        carry_ref[...] = jnp.zeros_like(carry_ref)

    def emit(r0, aux):
        hf_ref[r0:r0 + LRU_TILE, :] = h_ref[...].astype(BF16)

    proj = _lru_in_stages(x_ref, g_ref, w_ref, mk_ref, mv_ref, xg_ref, mq_ref, mo_ref)
    lru = _lru_stages(tail_ref, xprev_ref, xg_ref.at[:, 0:LRU_WIDTH],
                      t_prev == 0, t_prev == blocks_per_row - 1,
                      (perm_ref, cw_ref, cb_ref, wg_ref, ba_ref, bx_ref, lam_ref),
                      (ext_ref, a_ref, u_ref, h_ref, carry_ref), False, lambda r0: None, emit)
    done = object()
    next(lru)
    next(proj)
    traced = 1
    stages_per_tile = LRU_BLOCKS + 1
    for n, _ in enumerate(lru):
        if n == stages_per_tile - 1:
            assert traced * MLP_FINE[0] >= LRU_WIDTH
        if n % 2 == 1:
            next(proj, done)
            traced += 1
    for _ in proj:
        pass
    rows = xprev_ref.shape[0]
    tail_ref[...] = xprev_ref[rows - BF16_ROWS:rows, :]
    xprev_ref[...] = xg_ref[:, 0:LRU_WIDTH]


def _lru_rev_mlp_kernel(x_ref, mo_ref, xp_ref, xg_ref, xn_ref, hf_ref,
                        perm_ref, unperm_ref, cw_ref, cb_ref, wg_ref, ba_ref, bx_ref, lam_ref,
                        wo_ref, g_ref, wup_ref, wdn_ref, fg_ref, o_ref,
                        ext_ref, a_ref, u_ref, h_ref, carry_ref, y_ref, *, blocks_per_row, final):
    k = pl.program_id(0)
    kk = jnp.minimum(k, pl.num_programs(0) - 2)
    t_blk = blocks_per_row - 1 - kk % blocks_per_row

    @pl.when(k == 0)
    def _():
        y_ref[...] = jnp.zeros_like(y_ref)

    @pl.when(kk % blocks_per_row == 0)
    def _():
        carry_ref[...] = jnp.zeros_like(carry_ref)

    def pre(r0):
        return jnp.dot(perm_ref[...], xg_ref[r0:r0 + LRU_TILE, LRU_WIDTH:2 * LRU_WIDTH],
                       preferred_element_type=F32)

    def emit(r0, gate):
        inner = gate * (GELU_C0 + GELU_C1 * (gate * gate))
        hsum = hf_ref[r0:r0 + LRU_TILE, :].astype(F32) + h_ref[...]
        y = (hsum * gate) * (0.5 * jnp.tanh(inner) + 0.5)
        y_ref[r0:r0 + LRU_TILE, :] = jnp.dot(unperm_ref[...], y.astype(BF16),
                                              preferred_element_type=F32).astype(BF16)

    result = []
    mlp = _mlp_stages(x_ref, y_ref, mo_ref, wo_ref, g_ref, wup_ref, wdn_ref, result, MLP_FINE)
    lru = _lru_stages(xp_ref, xg_ref.at[:, 0:LRU_WIDTH], xn_ref, t_blk == 0, t_blk == blocks_per_row - 1,
                      (perm_ref, cw_ref, cb_ref, wg_ref, ba_ref, bx_ref, lam_ref),
                      (ext_ref, a_ref, u_ref, h_ref, carry_ref), True, pre, emit)
    next(lru)
    for _ in range(D_MODEL // MLP_FINE[0]):
        next(mlp)
    done = object()
    for _ in lru:
        next(mlp, done)
    for _ in mlp:
        pass
    acc = result[0]
    if final:
        acc = _rms(acc, fg_ref[...])
    o_ref[...] = acc


def _lru_weight_specs():
    return [_const_spec((CONV_WIDTH, LRU_WIDTH)), _const_spec((1, LRU_WIDTH)),
            _const_spec((LRU_BLOCKS, LRU_BLOCK_DIM, 2 * LRU_BLOCK_DIM)),
            _const_spec((1, LRU_WIDTH)), _const_spec((1, LRU_WIDTH)), _const_spec((1, LRU_WIDTH))]


def _lru_scratch():
    ts = LRU_TILE
    return [pltpu.VMEM((ts // SUBLANES + CONV_WIDTH - 1, SUBLANES, LRU_WIDTH), F32),
            pltpu.VMEM((ts, LRU_WIDTH), F32),
            pltpu.VMEM((ts, LRU_WIDTH), F32),
            pltpu.VMEM((ts, LRU_WIDTH), F32),
            pltpu.VMEM((1, LRU_WIDTH), F32)]


def _chunk_permutation(ts):
    n = ts // SUBLANES
    r = np.arange(ts)
    src = (r % SUBLANES) * n + r // SUBLANES
    return jnp.asarray(src[:, None] == np.arange(ts)[None, :], BF16)


def _lru_in_forward(x3, g, w_in, memkv, layer, perm, weights, batch, seq, blk):
    nt = seq // blk
    n_blocks = batch * nt
    d = x3.shape[-1]

    def cur_bt(k):
        kk = jnp.minimum(k, n_blocks - 1)
        return kk // nt, kk % nt

    def prev_bt(k):
        return cur_bt(jnp.maximum(k - 1, 0))

    return pl.pallas_call(
        functools.partial(_lru_in_fwd_kernel, blocks_per_row=nt),
        grid=(n_blocks + 1,),
        in_specs=[pl.BlockSpec((None, blk, d), lambda k: (*cur_bt(k), 0)),
                  _const_spec((1, d)), _const_spec((d, LRU_IN_COLS)),
                  pl.BlockSpec((None, MEM_LEN, MEM_W), lambda k: (cur_bt(k)[0], 0, 2 * layer)),
                  pl.BlockSpec((None, MEM_LEN, MEM_W), lambda k: (cur_bt(k)[0], 0, 2 * layer + 1)),
                  _const_spec((LRU_TILE, LRU_TILE))]
        + _lru_weight_specs(),
        out_specs=[pl.BlockSpec((None, blk, 2 * LRU_WIDTH), lambda k: (*cur_bt(k), 0)),
                   pl.BlockSpec((None, blk, MEM_W), lambda k: (*cur_bt(k), 0)),
                   pl.BlockSpec((None, blk, LRU_WIDTH), lambda k: (*prev_bt(k), 0))],
        out_shape=[jax.ShapeDtypeStruct((batch, seq, 2 * LRU_WIDTH), BF16),
                   jax.ShapeDtypeStruct((batch, seq, MEM_W), BF16),
                   jax.ShapeDtypeStruct((batch, seq, LRU_WIDTH), BF16)],
        scratch_shapes=[pltpu.VMEM((blk, LRU_WIDTH), BF16),
                        pltpu.VMEM((BF16_ROWS, LRU_WIDTH), BF16),
                        pltpu.VMEM((blk, MEM_W), BF16)]
        + _lru_scratch(),
        compiler_params=_params(1),
        name="lru_in_forward",
    )(x3, g, w_in, memkv, memkv, perm, *weights)


def _lru_reverse_mlp(x3, mo3, xg3, hf, perm, weights, wo, g, wup, wdn, fg, final, batch, seq, blk):
    nt = seq // blk
    n_blocks = batch * nt
    per = blk // BF16_ROWS
    last_halo = seq // BF16_ROWS - 1
    d = x3.shape[-1]

    def lru_bt(k):
        kk = jnp.minimum(k, n_blocks - 1)
        return kk // nt, nt - 1 - kk % nt

    def mlp_bt(k):
        return lru_bt(jnp.maximum(k - 1, 0))

    def at_mlp(col):
        return lambda k: (*mlp_bt(k), col)

    def at_lru(col):
        return lambda k: (*lru_bt(k), col)

    def prev_idx(k):
        b, t = lru_bt(k)
        return b, jnp.maximum(t * per - 1, 0), 0

    def next_idx(k):
        b, t = lru_bt(k)
        return b, jnp.minimum((t + 1) * per, last_halo), 0

    return pl.pallas_call(
        functools.partial(_lru_rev_mlp_kernel, blocks_per_row=nt, final=final),
        grid=(n_blocks + 1,),
        in_specs=[pl.BlockSpec((None, blk, d), at_mlp(0)),
                  pl.BlockSpec((None, blk, MEM_W), at_mlp(0)),
                  pl.BlockSpec((None, BF16_ROWS, LRU_WIDTH), prev_idx),
                  pl.BlockSpec((None, blk, 2 * LRU_WIDTH), at_lru(0)),
                  pl.BlockSpec((None, BF16_ROWS, LRU_WIDTH), next_idx),
                  pl.BlockSpec((None, blk, LRU_WIDTH), at_lru(0)),
                  _const_spec((LRU_TILE, LRU_TILE)), _const_spec((LRU_TILE, LRU_TILE))]
        + _lru_weight_specs()
        + [_const_spec(wo.shape), _const_spec((1, d)), _const_spec(wup.shape), _const_spec(wdn.shape),
           _const_spec((1, d))],
        out_specs=pl.BlockSpec((None, blk, d), at_mlp(0)),
        out_shape=jax.ShapeDtypeStruct(x3.shape, F32),
        scratch_shapes=_lru_scratch() + [pltpu.VMEM((blk, LRU_WIDTH), BF16)],
        compiler_params=_params(1),
        name="lru_reverse_mlp",
    )(x3, mo3, xg3, xg3, xg3, hf, perm, perm.T, *weights, wo, g, wup, wdn, fg)


def _dup_heads(w):
    d = w.shape[0]
    w4 = w.reshape(d, ATTN_KV_HEADS, 1, ATTN_HEAD_DIM)
    return jnp.broadcast_to(w4, (d, ATTN_KV_HEADS, 2, ATTN_HEAD_DIM)).reshape(d, KV_DUP_W)


def _rotary_tables(tile):
    half = ROPE_DIM // 2
    inv_freq = ROPE_THETA ** (-2.0 * jnp.arange(half, dtype=F32) / ROPE_DIM)
    freq = jnp.broadcast_to(inv_freq[:, None], (half, tile))
    d = np.arange(LANES) % ATTN_HEAD_DIM
    first = d < half
    second = (d >= half) & (d < ROPE_DIM)
    pick = np.arange(half)[:, None] == (d % half)[None, :]
    zeros8 = np.zeros((half, LANES), np.float32)
    cos_rows = np.where(pick & (d < ROPE_DIM)[None, :], 1.0, 0.0)
    lo_rows = np.where(pick & first[None, :], -1.0, 0.0)
    hi_rows = np.where(pick & second[None, :], 1.0, 0.0)
    ones_rows = np.zeros((BF16_ROWS, LANES), np.float32)
    ones_rows[0] = np.where(d < ROPE_DIM, 0.0, 1.0)
    zeros16 = np.zeros((BF16_ROWS, LANES), np.float32)
    cos_cols = np.concatenate([cos_rows, cos_rows, zeros8, zeros8, ones_rows], axis=0)
    lo_cols = np.concatenate([zeros8, zeros8, lo_rows, lo_rows, zeros16], axis=0)
    hi_cols = np.concatenate([zeros8, zeros8, hi_rows, hi_rows, zeros16], axis=0)
    return freq, jnp.asarray(np.concatenate([cos_cols, lo_cols, hi_cols], axis=1), BF16)


def _gate_weights(wa, wx):
    return (0.5 * jnp.concatenate([wa, wx], axis=-1)).astype(BF16)


def kernel(x, mem, positions, mix_norm, mlp_norm, mem_norm, final_norm, w_mem_kv, w_out, w_up, w_down,
           attn_w_in, attn_sinks, lru_w_in, lru_conv_w, lru_conv_b, lru_wa, lru_ba, lru_wx, lru_bx,
           lru_lambda):
    batch, seq, d = x.shape
    t = batch * seq
    depth = mix_norm.shape[0]
    xf = x.reshape(t, d)
    ones_d = jnp.ones((1, d), F32)

    w_kv_all = jnp.concatenate([w_mem_kv[l] for l in range(depth)], axis=1).astype(BF16)
    memkv = _norm_proj(mem.reshape(batch * MEM_LEN, d), mem_norm.reshape(1, d), w_kv_all,
                       tile=batch * MEM_LEN // 2)
    memkv = memkv.reshape(batch, MEM_LEN, depth * 2 * MEM_W)

    pos_rows = positions.reshape(t // PROJ_TILE, 1, PROJ_TILE).astype(jnp.int32)
    rot_freq, rot_expand = _rotary_tables(PROJ_TILE)

    for l in range(depth):
        kind, j = l % 2, l // 2
        g_mix = mix_norm[l].reshape(1, d)
        final = l == depth - 1
        mlp_weights = (w_out[l].astype(BF16), mlp_norm[l].reshape(1, d), w_up[l].astype(BF16),
                       w_down[l].astype(BF16), final_norm.reshape(1, d) if final else ones_d)
        if kind == 0:
            w = attn_w_in[j]
            w_in = jnp.concatenate([w[:, :Q_W], _dup_heads(w[:, Q_W:Q_W + KV_W]),
                                    _dup_heads(w[:, Q_W + KV_W:Q_W + 2 * KV_W]),
                                    w[:, Q_W + 2 * KV_W:]], axis=1).astype(BF16)
            p = _attn_in(xf, g_mix, pos_rows, rot_freq, rot_expand, w_in, PROJ_TILE)
            xf = _attention_mlp(xf.reshape(batch, seq, d), p, memkv, attn_sinks[j].astype(F32), l,
                                *mlp_weights, final, batch, seq, ATTN_TILE).reshape(t, d)
        else:
            x3 = xf.reshape(batch, seq, d)
            cw, cb = lru_conv_w[j], lru_conv_b[j].reshape(1, LRU_WIDTH)

            def dir_weights(dirn):
                return (cw, cb, _gate_weights(lru_wa[j, dirn], lru_wx[j, dirn]),
                        0.5 * lru_ba[j, dirn].reshape(1, LRU_WIDTH),
                        0.5 * lru_bx[j, dirn].reshape(1, LRU_WIDTH),
                        lru_lambda[j, dirn].reshape(1, LRU_WIDTH))

            perm = _chunk_permutation(LRU_TILE)
            xg, mo, hf = _lru_in_forward(x3, g_mix, lru_w_in[j].astype(BF16), memkv, l, perm,
                                         dir_weights(0), batch, seq, ROW_TILE)
            xf = _lru_reverse_mlp(x3, mo, xg, hf, perm, dir_weights(1),
                                  *mlp_weights, final, batch, seq, ROW_TILE).reshape(t, d)
    return xf.reshape(batch, seq, d)
```

```python
import functools

import jax
import jax.numpy as jnp
import numpy as np
from jax import lax
from jax.experimental import pallas as pl
from jax.experimental.pallas import tpu as pltpu

F32 = jnp.float32
BF16 = jnp.bfloat16

D_MODEL = 1024
EPS = 1e-6
ATTN_HEADS = 16
ATTN_KV_HEADS = 4
ATTN_HEAD_DIM = 64
ATTN_GROUP = 4
WINDOW = 128
BLOCK = 128
ROPE_THETA = 500000.0
ROPE_DIM = 16
Q_W = 1024
KV_W = 256
MEM_LEN = 256
MEM_HEADS = 4
MEM_HEAD_DIM = 128
MEM_W = 512
LRU_WIDTH = 1024
LRU_BLOCKS = 8
LRU_BLOCK_DIM = 128
LRU_C = 8.0
CONV_WIDTH = 4
D_FF = 4096
NEG = -1e30

LANES = 128
SUBLANES = 8
BF16_ROWS = 16
KV_DUP_W = 2 * KV_W
ATTN_IN_COLS = Q_W + 2 * KV_DUP_W + MEM_W
LRU_IN_COLS = 2 * LRU_WIDTH + MEM_W
VMEM_LIMIT = 56 * 1024 * 1024

ROW_TILE = 512
LRU_TILE = 256
FF_CHUNK = 1024
PROJ_CHUNK = 512
PROJ_TILE = 1024
ATTN_TILE = 512
LOG2E = 1.4426950408889634
GELU_C0 = 0.7978845608028654
GELU_C1 = GELU_C0 * 0.044715


def _const_spec(shape):
    nd = len(shape)
    return pl.BlockSpec(shape, lambda *_: (0,) * nd, pipeline_mode=pl.Buffered(1))


def _params(n_axes):
    return pltpu.CompilerParams(dimension_semantics=("arbitrary",) * n_axes,
                                vmem_limit_bytes=VMEM_LIMIT)


def _rms(x, g):
    ms = jnp.mean(x * x, axis=-1, keepdims=True)
    return x * lax.rsqrt(ms + EPS) * g


def _norm_proj_kernel(x_ref, g_ref, w_ref, o_ref, *, chunk):
    xn = _rms(x_ref[...], g_ref[...]).astype(BF16)
    for c in range(0, o_ref.shape[-1], chunk):
        o_ref[:, c:c + chunk] = jnp.dot(xn, w_ref[:, c:c + chunk],
                                        preferred_element_type=F32).astype(BF16)


def _norm_proj(x, g, w, tile):
    t, d = x.shape
    n = w.shape[1]
    return pl.pallas_call(
        functools.partial(_norm_proj_kernel, chunk=PROJ_CHUNK),
        grid=(t // tile,),
        in_specs=[pl.BlockSpec((tile, d), lambda i: (i, 0)),
                  _const_spec((1, d)), _const_spec((d, n))],
        out_specs=pl.BlockSpec((tile, n), lambda i: (i, 0)),
        out_shape=jax.ShapeDtypeStruct((t, n), BF16),
        compiler_params=_params(1),
        name="norm_proj",
    )(x, g, w)


def _split_bf16(x):
    hi = x.astype(BF16).astype(F32)
    return hi, x - hi


def _attn_in_kernel(x_ref, g_ref, pos_ref, freq_ref, expand_ref, w_ref, o_ref):
    xn = _rms(x_ref[...], g_ref[...]).astype(BF16)
    ang = freq_ref[...] * pos_ref[...].astype(F32)
    cos_hi, cos_lo = _split_bf16(jnp.cos(ang))
    sin_hi_, sin_lo_ = _split_bf16(jnp.sin(ang))
    one = jnp.where(lax.broadcasted_iota(jnp.int32, (BF16_ROWS, ang.shape[1]), 0) == 0, 1.0, 0.0)
    trig = jnp.concatenate([cos_hi, cos_lo, sin_hi_, sin_lo_, one], axis=0).astype(BF16)
    tables = lax.dot_general(trig, expand_ref[...], (((0,), (0,)), ((), ())),
                             preferred_element_type=F32)
    cos = tables[:, 0:LANES]
    sin_lo = tables[:, LANES:2 * LANES]
    sin_hi = tables[:, 2 * LANES:3 * LANES]
    q_scale = LOG2E * ATTN_HEAD_DIM ** -0.5
    mem_scale = LOG2E * MEM_HEAD_DIM ** -0.5
    half = ROPE_DIM // 2
    rot_cols = Q_W + KV_DUP_W
    for c in range(0, ATTN_IN_COLS, 2 * LANES):
        acc = jnp.dot(xn, w_ref[:, c:c + 2 * LANES], preferred_element_type=F32)
        for s in range(2):
            col = c + s * LANES
            slab = acc[:, s * LANES:(s + 1) * LANES]
            if col < rot_cols:
                slab = (slab * cos + pltpu.roll(slab, LANES - half, 1) * sin_lo
                        + pltpu.roll(slab, half, 1) * sin_hi)
                if col < Q_W:
                    slab = slab * q_scale
            elif col >= Q_W + 2 * KV_DUP_W:
                slab = slab * mem_scale
            o_ref[:, col:col + LANES] = slab.astype(BF16)


def _attn_in(x, g, pos, freq, expand, w, tile):
    t, d = x.shape
    n = w.shape[1]
    return pl.pallas_call(
        _attn_in_kernel,
        grid=(t // tile,),
        in_specs=[pl.BlockSpec((tile, d), lambda i: (i, 0)),
                  _const_spec((1, d)),
                  pl.BlockSpec((None, 1, tile), lambda i: (i, 0, 0)),
                  _const_spec(freq.shape), _const_spec(expand.shape), _const_spec((d, n))],
        out_specs=pl.BlockSpec((tile, n), lambda i: (i, 0)),
        out_shape=jax.ShapeDtypeStruct((t, n), BF16),
        compiler_params=_params(1),
        name="attn_in",
    )(x, g, pos, freq, expand, w)


def _score_ahead(items, scores, finish):
    pending = scores(items[0])
    yield "S"
    for n, item in enumerate(items):
        current = pending
        if n + 1 < len(items):
            pending = scores(items[n + 1])
            yield "S"
        finish(item, current)
        yield "F"


def _mem_attention_stages(mq_ref, mk_ref, mv_ref, mo_ref):
    rows = mq_ref.shape[0]

    def scores(item):
        r0, h = item
        sl = slice(h * MEM_HEAD_DIM, (h + 1) * MEM_HEAD_DIM)
        return lax.dot_general(mq_ref[r0:r0 + BLOCK, sl], mk_ref[:, sl], (((1,), (1,)), ((), ())),
                               preferred_element_type=F32)

    def finish(item, s):
        r0, h = item
        sl = slice(h * MEM_HEAD_DIM, (h + 1) * MEM_HEAD_DIM)
        m = jnp.max(s, axis=-1, keepdims=True)
        p = jnp.exp2(s - m)
        den = jnp.sum(p, axis=-1, keepdims=True)
        o = jnp.dot(p.astype(BF16), mv_ref[:, sl], preferred_element_type=F32)
        mo_ref[r0:r0 + BLOCK, sl] = (o / den).astype(BF16)

    return _score_ahead([(r0, h) for r0 in range(0, rows, BLOCK) for h in range(MEM_HEADS)], scores, finish)


def _mem_attention(mq_ref, mk_ref, mv_ref, mo_ref):
    for _ in _mem_attention_stages(mq_ref, mk_ref, mv_ref, mo_ref):
        pass


def _window_attention_stages(sink_ref, q_ref, k_refs, v_refs, o_ref, has_prev, has_next):
    nq = q_ref.shape[0] // BLOCK
    key = lax.broadcasted_iota(jnp.int32, (BLOCK, ATTN_GROUP * BLOCK), 0)
    qry = lax.broadcasted_iota(jnp.int32, (BLOCK, ATTN_GROUP * BLOCK), 1) % BLOCK
    band_prev = key >= qry
    band_next = key <= qry
    lo = lax.broadcasted_iota(jnp.int32, (BLOCK, LANES), 1) < ATTN_HEAD_DIM
    top = lax.broadcasted_iota(jnp.int32, (LANES, BLOCK), 0) < ATTN_HEAD_DIM
    k_all = jnp.concatenate([r[...] for r in k_refs], axis=0)
    v_all = jnp.concatenate([r[...] for r in v_refs], axis=0)
    zero = jnp.zeros((), BF16)
    def scores(item):
        j, h = item
        r0 = j * BLOCK
        kh = k_all[r0:r0 + 3 * BLOCK, h * LANES:(h + 1) * LANES]
        parts = []
        for g in range(ATTN_GROUP):
            slab = q_ref[r0:r0 + BLOCK, (2 * h + g // 2) * LANES:(2 * h + g // 2 + 1) * LANES]
            parts.append(jnp.where(lo if g % 2 == 0 else ~lo, slab, zero))
        qs = jnp.concatenate(parts, axis=0)
        return lax.dot_general(kh, qs, (((1,), (1,)), ((), ())),
                               preferred_element_type=F32)

    def finish(item, st):
        j, h = item
        r0 = j * BLOCK
        m_prev = band_prev if j > 0 else band_prev & has_prev
        m_next = band_next if j < nq - 1 else band_next & has_next
        vh = v_all[r0:r0 + 3 * BLOCK, h * LANES:(h + 1) * LANES]
        s0 = jnp.where(m_prev, st[0:BLOCK], NEG)
        s1 = st[BLOCK:2 * BLOCK]
        s2 = jnp.where(m_next, st[2 * BLOCK:3 * BLOCK], NEG)
        sink = jnp.concatenate(
            [jnp.full((1, BLOCK), sink_ref[ATTN_GROUP * h + g] * LOG2E, F32) for g in range(ATTN_GROUP)],
            axis=1)
        m = jnp.max(jnp.maximum(jnp.maximum(s0, s1), s2), axis=0, keepdims=True)
        m = jnp.maximum(m, sink)
        p0, p1, p2 = jnp.exp2(s0 - m), jnp.exp2(s1 - m), jnp.exp2(s2 - m)
        den = jnp.sum(p0 + p1 + p2, axis=0, keepdims=True) + jnp.exp2(sink - m)
        pt = jnp.concatenate([p0, p1, p2], axis=0).astype(BF16)
        ot = lax.dot_general(vh, pt, (((0,), (0,)), ((), ())),
                             preferred_element_type=F32)
        ot = ot / den
        for pair in range(2):
            even = ot[:, (2 * pair) * BLOCK:(2 * pair + 1) * BLOCK]
            odd = ot[:, (2 * pair + 1) * BLOCK:(2 * pair + 2) * BLOCK]
            slab = jnp.where(top, even, odd).T
            o_ref[r0:r0 + BLOCK, (2 * h + pair) * LANES:(2 * h + pair + 1) * LANES] = slab.astype(BF16)

    return _score_ahead([(j, h) for j in range(nq) for h in range(ATTN_KV_HEADS)], scores, finish)


MLP_COARSE = (D_MODEL, FF_CHUNK, D_MODEL // 2)
MLP_FINE = (2 * LANES, 2 * LANES, 2 * LANES)


def _mlp_stages(x_ref, y_ref, mo_ref, wo_ref, g_ref, wup_ref, wdn_ref, result, widths=MLP_COARSE):
    proj_w, up_w, down_w = widths
    mix_w = y_ref.shape[-1]
    pieces = []
    for c in range(0, D_MODEL, proj_w):
        pieces.append(x_ref[:, c:c + proj_w]
                      + jnp.dot(y_ref[...], wo_ref[0:mix_w, c:c + proj_w], preferred_element_type=F32)
                      + jnp.dot(mo_ref[...], wo_ref[mix_w:, c:c + proj_w], preferred_element_type=F32))
        yield
    x1 = jnp.concatenate(pieces, axis=1)
    xn = _rms(x1, g_ref[...]).astype(BF16)
    hids = []
    for c in range(0, D_FF, up_w):
        hid = jnp.dot(xn, wup_ref[:, c:c + up_w], preferred_element_type=F32)
        hid = jnp.maximum(hid, 0.0)
        hids.append((hid * hid).astype(BF16))
        yield
    hid = jnp.concatenate(hids, axis=1)
    pieces = []
    for c in range(0, D_MODEL, down_w):
        pieces.append(x1[:, c:c + down_w]
                      + jnp.dot(hid, wdn_ref[:, c:c + down_w], preferred_element_type=F32))
        if c + down_w == D_MODEL:
            result.append(jnp.concatenate(pieces, axis=1))
        yield


def _attn_mlp_kernel(sink_ref, x_ref, q_ref, kvp_ref, kvc_ref, kvn_ref,
                     mq_ref, mk_ref, mv_ref, wo_ref, g_ref, wup_ref, wdn_ref, fg_ref, o_ref,
                     y_ref, mo_ref, *, tiles_per_row, final):
    k = pl.program_id(0)
    kk = jnp.minimum(k, pl.num_programs(0) - 2)
    t_idx = kk % tiles_per_row

    @pl.when(k == 0)
    def _():
        y_ref[...] = jnp.zeros_like(y_ref)
        mo_ref[...] = jnp.zeros_like(mo_ref)

    def attention():
        kv_refs = (kvp_ref, kvc_ref, kvn_ref)
        yield from _window_attention_stages(sink_ref, q_ref,
                                            [r.at[:, 0:KV_DUP_W] for r in kv_refs],
                                            [r.at[:, KV_DUP_W:2 * KV_DUP_W] for r in kv_refs], y_ref,
                                            t_idx > 0, t_idx < tiles_per_row - 1)
        yield from _mem_attention_stages(mq_ref, mk_ref, mv_ref, mo_ref)

    result = []
    attn = attention()
    mlp = _mlp_stages(x_ref, y_ref, mo_ref, wo_ref, g_ref, wup_ref, wdn_ref, result, MLP_FINE)
    next(attn)
    next(mlp)
    next(attn)
    for _ in range(D_MODEL // MLP_FINE[0] - 1):
        next(mlp)
    done = object()
    for tag in attn:
        if tag == "S":
            next(mlp, done)
    for _ in mlp:
        pass
    acc = result[0]
    if final:
        acc = _rms(acc, fg_ref[...])
    o_ref[...] = acc


def _attention_mlp(x3, p, memkv, sinks, layer, wo, g, wup, wdn, fg, final, batch, seq, tq):
    nt = seq // tq
    n_tiles = batch * nt
    per = tq // BLOCK
    nb = seq // BLOCK
    d = x3.shape[-1]
    p3 = p.reshape(batch, seq, ATTN_IN_COLS)
    kv_w = 2 * KV_DUP_W
    kvcol = Q_W // kv_w
    mqcol = (Q_W + kv_w) // MEM_W

    def attn_bt(k):
        kk = jnp.minimum(k, n_tiles - 1)
        return kk // nt, kk % nt

    def mlp_bt(k):
        return attn_bt(jnp.maximum(k - 1, 0))

    def kv_prev(k):
        b, i = attn_bt(k)
        return b, jnp.maximum(i * per - 1, 0), kvcol

    def kv_next(k):
        b, i = attn_bt(k)
        return b, jnp.minimum((i + 1) * per, nb - 1), kvcol

    kv_specs = [pl.BlockSpec((None, BLOCK, kv_w), kv_prev),
                pl.BlockSpec((None, tq, kv_w), lambda k: (*attn_bt(k), kvcol)),
                pl.BlockSpec((None, BLOCK, kv_w), kv_next)]

    return pl.pallas_call(
        functools.partial(_attn_mlp_kernel, tiles_per_row=nt, final=final),
        grid=(n_tiles + 1,),
        in_specs=[pl.BlockSpec(memory_space=pltpu.SMEM),
                  pl.BlockSpec((None, tq, d), lambda k: (*mlp_bt(k), 0)),
                  pl.BlockSpec((None, tq, Q_W), lambda k: (*attn_bt(k), 0))]
        + kv_specs
        + [pl.BlockSpec((None, tq, MEM_W), lambda k: (*attn_bt(k), mqcol)),
           pl.BlockSpec((None, MEM_LEN, MEM_W), lambda k: (attn_bt(k)[0], 0, 2 * layer)),
           pl.BlockSpec((None, MEM_LEN, MEM_W), lambda k: (attn_bt(k)[0], 0, 2 * layer + 1)),
           _const_spec(wo.shape), _const_spec((1, d)), _const_spec(wup.shape), _const_spec(wdn.shape),
           _const_spec((1, d))],
        out_specs=pl.BlockSpec((None, tq, d), lambda k: (*mlp_bt(k), 0)),
        out_shape=jax.ShapeDtypeStruct(x3.shape, F32),
        scratch_shapes=[pltpu.VMEM((tq, Q_W), BF16), pltpu.VMEM((tq, MEM_W), BF16)],
        compiler_params=_params(1),
        name="attention_mlp",
    )(sinks, x3, p3, p3, p3, p3, p3, memkv, memkv, wo, g, wup, wdn, fg)


def _permuted_conv(xp, before, after, cw_ref, cb_ref, ext_ref):
    ts, w = xp.shape
    n = ts // SUBLANES
    xp = xp.reshape(n, SUBLANES, w)
    ext_ref[1:n + 1] = xp
    sub = lax.broadcasted_iota(jnp.int32, (SUBLANES, w), 0)
    ext_ref[0] = jnp.where(sub == 0, before, pltpu.roll(xp[n - 1], 1, 0))
    ext_ref[n + 1] = jnp.where(sub == SUBLANES - 1, after[0:1], pltpu.roll(xp[0], SUBLANES - 1, 0))
    ext_ref[n + 2] = jnp.where(sub == SUBLANES - 1, after[1:2], pltpu.roll(xp[1], SUBLANES - 1, 0))
    xc = cb_ref[...][None]
    for tap in range(CONV_WIDTH):
        xc = xc + ext_ref[tap:tap + n] * cw_ref[tap:tap + 1, :][None]
    return xc.reshape(ts, w)


def _lru_gate_block(blk, xc, xcb, wg_ref, ba_ref, bx_ref, lam_ref, a_ref, u_ref):
    sl = slice(blk * LRU_BLOCK_DIM, (blk + 1) * LRU_BLOCK_DIM)
    gates = jnp.dot(xcb[:, sl], wg_ref[blk], preferred_element_type=F32)
    lam = lam_ref[:, sl]
    softplus_neg_lam = jnp.maximum(-lam, 0.0) + jnp.log1p(jnp.exp(-jnp.abs(lam)))
    rate = (-0.5 * LRU_C * LOG2E) * softplus_neg_lam
    t_rec = jnp.tanh(gates[:, :LRU_BLOCK_DIM] + ba_ref[:, sl])
    t_in = jnp.tanh(gates[:, LRU_BLOCK_DIM:] + bx_ref[:, sl])
    a = jnp.exp2(rate * t_rec + rate)
    a_ref[:, sl] = a
    z = 1.0 - a * a
    root = jnp.where(z > 0.0, z * lax.rsqrt(z), 0.0)
    u_ref[:, sl] = root * ((0.5 * t_in + 0.5) * xc[:, sl])


def _lru_scan(a_ref, u_ref, h_ref, carry_ref, reverse):
    ts, w = a_ref.shape
    n = ts // SUBLANES
    order = range(n - 1, -1, -1) if reverse else range(n)

    def rows(ref, j):
        return ref[j * SUBLANES:(j + 1) * SUBLANES, :]

    local = jnp.zeros((SUBLANES, w), F32)
    prod = jnp.ones((SUBLANES, w), F32)
    for j in order:
        a = rows(a_ref, j)
        local = a * local + rows(u_ref, j)
        prod = a * prod
    c = carry_ref[...]
    inflow = [None] * SUBLANES
    for s in (range(SUBLANES - 1, -1, -1) if reverse else range(SUBLANES)):
        inflow[s] = c
        c = prod[s:s + 1] * c + local[s:s + 1]
    carry_ref[...] = c
    h = jnp.concatenate(inflow, axis=0)
    for j in order:
        h = rows(a_ref, j) * h + rows(u_ref, j)
        h_ref[j * SUBLANES:(j + 1) * SUBLANES, :] = h


def _lru_stages(prev_ref, cur_ref, next_ref, first, last, lru_refs, scratch, reverse, pre, emit):
    perm_ref, cw_ref, cb_ref, wg_ref, ba_ref, bx_ref, lam_ref = lru_refs
    ext_ref, a_ref, u_ref, h_ref, carry_ref = scratch
    rows = cur_ref.shape[0]
    starts = list(range(0, rows, LRU_TILE))
    starts = starts[::-1] if reverse else starts
    permuted = [(jnp.dot(perm_ref[...], cur_ref[r0:r0 + LRU_TILE, :], preferred_element_type=F32), pre(r0))
                for r0 in starts]
    yield
    for r0, (xp, aux) in zip(starts, permuted):
        if r0 == 0:
            before = jnp.where(first, 0.0, prev_ref[BF16_ROWS - 1:BF16_ROWS, :].astype(F32))
        else:
            before = cur_ref[r0 - 1:r0, :].astype(F32)
        if r0 + LRU_TILE == rows:
            after = jnp.where(last, 0.0, next_ref[0:2, :].astype(F32))
        else:
            after = cur_ref[r0 + LRU_TILE:r0 + LRU_TILE + 2, :].astype(F32)
        xc = _permuted_conv(xp, before, after, cw_ref, cb_ref, ext_ref)
        xcb = xc.astype(BF16)
        for blk in range(LRU_BLOCKS):
            _lru_gate_block(blk, xc, xcb, wg_ref, ba_ref, bx_ref, lam_ref, a_ref, u_ref)
            yield
        _lru_scan(a_ref, u_ref, h_ref, carry_ref, reverse)
        emit(r0, aux)
        yield


def _lru_in_stages(x_ref, g_ref, w_ref, mk_ref, mv_ref, xg_ref, mq_ref, mo_ref):
    xn = _rms(x_ref[...], g_ref[...]).astype(BF16)
    xg_w = xg_ref.shape[-1]
    piece = MLP_FINE[0]
    for c in range(0, LRU_IN_COLS, piece):
        acc = jnp.dot(xn, w_ref[:, c:c + piece], preferred_element_type=F32)
        if c < xg_w:
            xg_ref[:, c:c + piece] = acc.astype(BF16)
        else:
            mq_ref[:, c - xg_w:c - xg_w + piece] = (acc * (LOG2E * MEM_HEAD_DIM ** -0.5)).astype(BF16)
        yield
    yield from _mem_attention_stages(mq_ref, mk_ref, mv_ref, mo_ref)


def _lru_in_fwd_kernel(x_ref, g_ref, w_ref, mk_ref, mv_ref,
                       perm_ref, cw_ref, cb_ref, wg_ref, ba_ref, bx_ref, lam_ref,
                       xg_ref, mo_ref, hf_ref,
                       xprev_ref, tail_ref, mq_ref, ext_ref, a_ref, u_ref, h_ref, carry_ref,
                       *, blocks_per_row):
    k = pl.program_id(0)
    t_prev = jnp.maximum(k - 1, 0) % blocks_per_row

    @pl.when(k == 0)
    def _():
        xprev_ref[...] = jnp.zeros_like(xprev_ref)
        tail_ref[...] = jnp.zeros_like(tail_ref)

    @pl.when(t_prev == 0)
    def _():
        carry_ref[...] = jnp.zeros_like(carry_ref)

    def emit(r0, aux):
        hf_ref[r0:r0 + LRU_TILE, :] = h_ref[...].astype(BF16)

    proj = _lru_in_stages(x_ref, g_ref, w_ref, mk_ref, mv_ref, xg_ref, mq_ref, mo_ref)
    lru = _lru_stages(tail_ref, xprev_ref, xg_ref.at[:, 0:LRU_WIDTH],
                      t_prev == 0, t_prev == blocks_per_row - 1,
                      (perm_ref, cw_ref, cb_ref, wg_ref, ba_ref, bx_ref, lam_ref),
                      (ext_ref, a_ref, u_ref, h_ref, carry_ref), False, lambda r0: None, emit)
    done = object()
    next(lru)
    next(proj)
    traced = 1
    stages_per_tile = LRU_BLOCKS + 1
    for n, _ in enumerate(lru):
        if n == stages_per_tile - 1:
            assert traced * MLP_FINE[0] >= LRU_WIDTH
        if n % 2 == 1:
            next(proj, done)
            traced += 1
    for _ in proj:
        pass
    rows = xprev_ref.shape[0]
    tail_ref[...] = xprev_ref[rows - BF16_ROWS:rows, :]
    xprev_ref[...] = xg_ref[:, 0:LRU_WIDTH]


def _lru_rev_mlp_kernel(x_ref, mo_ref, xp_ref, xg_ref, xn_ref, hf_ref,
                        perm_ref, unperm_ref, cw_ref, cb_ref, wg_ref, ba_ref, bx_ref, lam_ref,
                        wo_ref, g_ref, wup_ref, wdn_ref, fg_ref, o_ref,
                        ext_ref, a_ref, u_ref, h_ref, carry_ref, y_ref, *, blocks_per_row, final):
    k = pl.program_id(0)
    kk = jnp.minimum(k, pl.num_programs(0) - 2)
    t_blk = blocks_per_row - 1 - kk % blocks_per_row

    @pl.when(k == 0)
    def _():
        y_ref[...] = jnp.zeros_like(y_ref)

    @pl.when(kk % blocks_per_row == 0)
    def _():
        carry_ref[...] = jnp.zeros_like(carry_ref)

    def pre(r0):
        return jnp.dot(perm_ref[...], xg_ref[r0:r0 + LRU_TILE, LRU_WIDTH:2 * LRU_WIDTH],
                       preferred_element_type=F32)

    def emit(r0, gate):
        inner = gate * (GELU_C0 + GELU_C1 * (gate * gate))
        hsum = hf_ref[r0:r0 + LRU_TILE, :].astype(F32) + h_ref[...]
        y = (hsum * gate) * (0.5 * jnp.tanh(inner) + 0.5)
        y_ref[r0:r0 + LRU_TILE, :] = jnp.dot(unperm_ref[...], y.astype(BF16),
                                              preferred_element_type=F32).astype(BF16)

    result = []
    mlp = _mlp_stages(x_ref, y_ref, mo_ref, wo_ref, g_ref, wup_ref, wdn_ref, result, MLP_FINE)
    lru = _lru_stages(xp_ref, xg_ref.at[:, 0:LRU_WIDTH], xn_ref, t_blk == 0, t_blk == blocks_per_row - 1,
                      (perm_ref, cw_ref, cb_ref, wg_ref, ba_ref, bx_ref, lam_ref),
                      (ext_ref, a_ref, u_ref, h_ref, carry_ref), True, pre, emit)
    next(lru)
    for _ in range(D_MODEL // MLP_FINE[0]):
        next(mlp)
    done = object()
    for _ in lru:
        next(mlp, done)
    for _ in mlp:
        pass
    acc = result[0]
    if final:
        acc = _rms(acc, fg_ref[...])
    o_ref[...] = acc


def _lru_weight_specs():
    return [_const_spec((CONV_WIDTH, LRU_WIDTH)), _const_spec((1, LRU_WIDTH)),
            _const_spec((LRU_BLOCKS, LRU_BLOCK_DIM, 2 * LRU_BLOCK_DIM)),
            _const_spec((1, LRU_WIDTH)), _const_spec((1, LRU_WIDTH)), _const_spec((1, LRU_WIDTH))]


def _lru_scratch():
    ts = LRU_TILE
    return [pltpu.VMEM((ts // SUBLANES + CONV_WIDTH - 1, SUBLANES, LRU_WIDTH), F32),
            pltpu.VMEM((ts, LRU_WIDTH), F32),
            pltpu.VMEM((ts, LRU_WIDTH), F32),
            pltpu.VMEM((ts, LRU_WIDTH), F32),
            pltpu.VMEM((1, LRU_WIDTH), F32)]


def _chunk_permutation(ts):
    n = ts // SUBLANES
    r = np.arange(ts)
    src = (r % SUBLANES) * n + r // SUBLANES
    return jnp.asarray(src[:, None] == np.arange(ts)[None, :], BF16)


def _lru_in_forward(x3, g, w_in, memkv, layer, perm, weights, batch, seq, blk):
    nt = seq // blk
    n_blocks = batch * nt
    d = x3.shape[-1]

    def cur_bt(k):
        kk = jnp.minimum(k, n_blocks - 1)
        return kk // nt, kk % nt

    def prev_bt(k):
        return cur_bt(jnp.maximum(k - 1, 0))

    return pl.pallas_call(
        functools.partial(_lru_in_fwd_kernel, blocks_per_row=nt),
        grid=(n_blocks + 1,),
        in_specs=[pl.BlockSpec((None, blk, d), lambda k: (*cur_bt(k), 0)),
                  _const_spec((1, d)), _const_spec((d, LRU_IN_COLS)),
                  pl.BlockSpec((None, MEM_LEN, MEM_W), lambda k: (cur_bt(k)[0], 0, 2 * layer)),
                  pl.BlockSpec((None, MEM_LEN, MEM_W), lambda k: (cur_bt(k)[0], 0, 2 * layer + 1)),
                  _const_spec((LRU_TILE, LRU_TILE))]
        + _lru_weight_specs(),
        out_specs=[pl.BlockSpec((None, blk, 2 * LRU_WIDTH), lambda k: (*cur_bt(k), 0)),
                   pl.BlockSpec((None, blk, MEM_W), lambda k: (*cur_bt(k), 0)),
                   pl.BlockSpec((None, blk, LRU_WIDTH), lambda k: (*prev_bt(k), 0))],
        out_shape=[jax.ShapeDtypeStruct((batch, seq, 2 * LRU_WIDTH), BF16),
                   jax.ShapeDtypeStruct((batch, seq, MEM_W), BF16),
                   jax.ShapeDtypeStruct((batch, seq, LRU_WIDTH), BF16)],
        scratch_shapes=[pltpu.VMEM((blk, LRU_WIDTH), BF16),
                        pltpu.VMEM((BF16_ROWS, LRU_WIDTH), BF16),
                        pltpu.VMEM((blk, MEM_W), BF16)]
        + _lru_scratch(),
        compiler_params=_params(1),
        name="lru_in_forward",
    )(x3, g, w_in, memkv, memkv, perm, *weights)


def _lru_reverse_mlp(x3, mo3, xg3, hf, perm, weights, wo, g, wup, wdn, fg, final, batch, seq, blk):
    nt = seq // blk
    n_blocks = batch * nt
    per = blk // BF16_ROWS
    last_halo = seq // BF16_ROWS - 1
    d = x3.shape[-1]

    def lru_bt(k):
        kk = jnp.minimum(k, n_blocks - 1)
        return kk // nt, nt - 1 - kk % nt

    def mlp_bt(k):
        return lru_bt(jnp.maximum(k - 1, 0))

    def at_mlp(col):
        return lambda k: (*mlp_bt(k), col)

    def at_lru(col):
        return lambda k: (*lru_bt(k), col)

    def prev_idx(k):
        b, t = lru_bt(k)
        return b, jnp.maximum(t * per - 1, 0), 0

    def next_idx(k):
        b, t = lru_bt(k)
        return b, jnp.minimum((t + 1) * per, last_halo), 0

    return pl.pallas_call(
        functools.partial(_lru_rev_mlp_kernel, blocks_per_row=nt, final=final),
        grid=(n_blocks + 1,),
        in_specs=[pl.BlockSpec((None, blk, d), at_mlp(0)),
                  pl.BlockSpec((None, blk, MEM_W), at_mlp(0)),
                  pl.BlockSpec((None, BF16_ROWS, LRU_WIDTH), prev_idx),
                  pl.BlockSpec((None, blk, 2 * LRU_WIDTH), at_lru(0)),
                  pl.BlockSpec((None, BF16_ROWS, LRU_WIDTH), next_idx),
                  pl.BlockSpec((None, blk, LRU_WIDTH), at_lru(0)),
                  _const_spec((LRU_TILE, LRU_TILE)), _const_spec((LRU_TILE, LRU_TILE))]
        + _lru_weight_specs()
        + [_const_spec(wo.shape), _const_spec((1, d)), _const_spec(wup.shape), _const_spec(wdn.shape),
           _const_spec((1, d))],
        out_specs=pl.BlockSpec((None, blk, d), at_mlp(0)),
        out_shape=jax.ShapeDtypeStruct(x3.shape, F32),
        scratch_shapes=_lru_scratch() + [pltpu.VMEM((blk, LRU_WIDTH), BF16)],
        compiler_params=_params(1),
        name="lru_reverse_mlp",
    )(x3, mo3, xg3, xg3, xg3, hf, perm, perm.T, *weights, wo, g, wup, wdn, fg)


def _dup_heads(w):
    d = w.shape[0]
    w4 = w.reshape(d, ATTN_KV_HEADS, 1, ATTN_HEAD_DIM)
    return jnp.broadcast_to(w4, (d, ATTN_KV_HEADS, 2, ATTN_HEAD_DIM)).reshape(d, KV_DUP_W)


def _rotary_tables(tile):
    half = ROPE_DIM // 2
    inv_freq = ROPE_THETA ** (-2.0 * jnp.arange(half, dtype=F32) / ROPE_DIM)
    freq = jnp.broadcast_to(inv_freq[:, None], (half, tile))
    d = np.arange(LANES) % ATTN_HEAD_DIM
    first = d < half
    second = (d >= half) & (d < ROPE_DIM)
    pick = np.arange(half)[:, None] == (d % half)[None, :]
    zeros8 = np.zeros((half, LANES), np.float32)
    cos_rows = np.where(pick & (d < ROPE_DIM)[None, :], 1.0, 0.0)
    lo_rows = np.where(pick & first[None, :], -1.0, 0.0)
    hi_rows = np.where(pick & second[None, :], 1.0, 0.0)
    ones_rows = np.zeros((BF16_ROWS, LANES), np.float32)
    ones_rows[0] = np.where(d < ROPE_DIM, 0.0, 1.0)
    zeros16 = np.zeros((BF16_ROWS, LANES), np.float32)
    cos_cols = np.concatenate([cos_rows, cos_rows, zeros8, zeros8, ones_rows], axis=0)
    lo_cols = np.concatenate([zeros8, zeros8, lo_rows, lo_rows, zeros16], axis=0)
    hi_cols = np.concatenate([zeros8, zeros8, hi_rows, hi_rows, zeros16], axis=0)
    return freq, jnp.asarray(np.concatenate([cos_cols, lo_cols, hi_cols], axis=1), BF16)


def _gate_weights(wa, wx):
    return (0.5 * jnp.concatenate([wa, wx], axis=-1)).astype(BF16)


def kernel(x, mem, positions, mix_norm, mlp_norm, mem_norm, final_norm, w_mem_kv, w_out, w_up, w_down,
           attn_w_in, attn_sinks, lru_w_in, lru_conv_w, lru_conv_b, lru_wa, lru_ba, lru_wx, lru_bx,
           lru_lambda):
    batch, seq, d = x.shape
    t = batch * seq
    depth = mix_norm.shape[0]
    xf = x.reshape(t, d)
    ones_d = jnp.ones((1, d), F32)

    w_kv_all = jnp.concatenate([w_mem_kv[l] for l in range(depth)], axis=1).astype(BF16)
    memkv = _norm_proj(mem.reshape(batch * MEM_LEN, d), mem_norm.reshape(1, d), w_kv_all,
                       tile=batch * MEM_LEN // 2)
    memkv = memkv.reshape(batch, MEM_LEN, depth * 2 * MEM_W)

    pos_rows = positions.reshape(t // PROJ_TILE, 1, PROJ_TILE).astype(jnp.int32)
    rot_freq, rot_expand = _rotary_tables(PROJ_TILE)

    for l in range(depth):
        kind, j = l % 2, l // 2
        g_mix = mix_norm[l].reshape(1, d)
        final = l == depth - 1
        mlp_weights = (w_out[l].astype(BF16), mlp_norm[l].reshape(1, d), w_up[l].astype(BF16),
                       w_down[l].astype(BF16), final_norm.reshape(1, d) if final else ones_d)
        if kind == 0:
            w = attn_w_in[j]
            w_in = jnp.concatenate([w[:, :Q_W], _dup_heads(w[:, Q_W:Q_W + KV_W]),
                                    _dup_heads(w[:, Q_W + KV_W:Q_W + 2 * KV_W]),
                                    w[:, Q_W + 2 * KV_W:]], axis=1).astype(BF16)
            p = _attn_in(xf, g_mix, pos_rows, rot_freq, rot_expand, w_in, PROJ_TILE)
            xf = _attention_mlp(xf.reshape(batch, seq, d), p, memkv, attn_sinks[j].astype(F32), l,
                                *mlp_weights, final, batch, seq, ATTN_TILE).reshape(t, d)
        else:
            x3 = xf.reshape(batch, seq, d)
            cw, cb = lru_conv_w[j], lru_conv_b[j].reshape(1, LRU_WIDTH)

            def dir_weights(dirn):
                return (cw, cb, _gate_weights(lru_wa[j, dirn], lru_wx[j, dirn]),
                        0.5 * lru_ba[j, dirn].reshape(1, LRU_WIDTH),
                        0.5 * lru_bx[j, dirn].reshape(1, LRU_WIDTH),
                        lru_lambda[j, dirn].reshape(1, LRU_WIDTH))

            perm = _chunk_permutation(LRU_TILE)
            xg, mo, hf = _lru_in_forward(x3, g_mix, lru_w_in[j].astype(BF16), memkv, l, perm,
                                         dir_weights(0), batch, seq, PROJ_TILE)
            xf = _lru_reverse_mlp(x3, mo, xg, hf, perm, dir_weights(1),
                                  *mlp_weights, final, batch, seq, ROW_TILE).reshape(t, d)
    return xf.reshape(batch, seq, d)
```

```python
import functools

import jax
import jax.numpy as jnp
import numpy as np
from jax import lax
from jax.experimental import pallas as pl
from jax.experimental.pallas import tpu as pltpu

F32 = jnp.float32
BF16 = jnp.bfloat16

D_MODEL = 1024
EPS = 1e-6
ATTN_HEADS = 16
ATTN_KV_HEADS = 4
ATTN_HEAD_DIM = 64
ATTN_GROUP = 4
WINDOW = 128
BLOCK = 128
ROPE_THETA = 500000.0
ROPE_DIM = 16
Q_W = 1024
KV_W = 256
MEM_LEN = 256
MEM_HEADS = 4
MEM_HEAD_DIM = 128
MEM_W = 512
LRU_WIDTH = 1024
LRU_BLOCKS = 8
LRU_BLOCK_DIM = 128
LRU_C = 8.0
CONV_WIDTH = 4
D_FF = 4096
NEG = -1e30

LANES = 128
SUBLANES = 8
BF16_ROWS = 16
KV_DUP_W = 2 * KV_W
ATTN_IN_COLS = Q_W + 2 * KV_DUP_W + MEM_W
LRU_IN_COLS = 2 * LRU_WIDTH + MEM_W
VMEM_LIMIT = 56 * 1024 * 1024

ROW_TILE = 512
LRU_TILE = 256
FF_CHUNK = 1024
PROJ_CHUNK = 512
PROJ_TILE = 1024
ATTN_TILE = 512
LOG2E = 1.4426950408889634
GELU_C0 = 0.7978845608028654
GELU_C1 = GELU_C0 * 0.044715


def _const_spec(shape):
    nd = len(shape)
    return pl.BlockSpec(shape, lambda *_: (0,) * nd, pipeline_mode=pl.Buffered(1))


def _params(n_axes):
    return pltpu.CompilerParams(dimension_semantics=("arbitrary",) * n_axes,
                                vmem_limit_bytes=VMEM_LIMIT)


def _rms(x, g):
    ms = jnp.mean(x * x, axis=-1, keepdims=True)
    return x * lax.rsqrt(ms + EPS) * g


def _norm_proj_kernel(x_ref, g_ref, w_ref, o_ref, *, chunk):
    xn = _rms(x_ref[...], g_ref[...]).astype(BF16)
    for c in range(0, o_ref.shape[-1], chunk):
        o_ref[:, c:c + chunk] = jnp.dot(xn, w_ref[:, c:c + chunk],
                                        preferred_element_type=F32).astype(BF16)


def _norm_proj(x, g, w, tile):
    t, d = x.shape
    n = w.shape[1]
    return pl.pallas_call(
        functools.partial(_norm_proj_kernel, chunk=PROJ_CHUNK),
        grid=(t // tile,),
        in_specs=[pl.BlockSpec((tile, d), lambda i: (i, 0)),
                  _const_spec((1, d)), _const_spec((d, n))],
        out_specs=pl.BlockSpec((tile, n), lambda i: (i, 0)),
        out_shape=jax.ShapeDtypeStruct((t, n), BF16),
        compiler_params=_params(1),
        name="norm_proj",
    )(x, g, w)


def _split_bf16(x):
    hi = x.astype(BF16).astype(F32)
    return hi, x - hi


def _attn_in_kernel(x_ref, g_ref, pos_ref, freq_ref, expand_ref, w_ref, o_ref):
    xn = _rms(x_ref[...], g_ref[...]).astype(BF16)
    ang = freq_ref[...] * pos_ref[...].astype(F32)
    cos_hi, cos_lo = _split_bf16(jnp.cos(ang))
    sin_hi_, sin_lo_ = _split_bf16(jnp.sin(ang))
    one = jnp.where(lax.broadcasted_iota(jnp.int32, (BF16_ROWS, ang.shape[1]), 0) == 0, 1.0, 0.0)
    trig = jnp.concatenate([cos_hi, cos_lo, sin_hi_, sin_lo_, one], axis=0).astype(BF16)
    tables = lax.dot_general(trig, expand_ref[...], (((0,), (0,)), ((), ())),
                             preferred_element_type=F32)
    cos = tables[:, 0:LANES]
    sin_lo = tables[:, LANES:2 * LANES]
    sin_hi = tables[:, 2 * LANES:3 * LANES]
    q_scale = LOG2E * ATTN_HEAD_DIM ** -0.5
    mem_scale = LOG2E * MEM_HEAD_DIM ** -0.5
    half = ROPE_DIM // 2
    rot_cols = Q_W + KV_DUP_W
    for c in range(0, ATTN_IN_COLS, 2 * LANES):
        acc = jnp.dot(xn, w_ref[:, c:c + 2 * LANES], preferred_element_type=F32)
        for s in range(2):
            col = c + s * LANES
            slab = acc[:, s * LANES:(s + 1) * LANES]
            if col < rot_cols:
                slab = (slab * cos + pltpu.roll(slab, LANES - half, 1) * sin_lo
                        + pltpu.roll(slab, half, 1) * sin_hi)
                if col < Q_W:
                    slab = slab * q_scale
            elif col >= Q_W + 2 * KV_DUP_W:
                slab = slab * mem_scale
            o_ref[:, col:col + LANES] = slab.astype(BF16)


def _attn_in(x, g, pos, freq, expand, w, tile):
    t, d = x.shape
    n = w.shape[1]
    return pl.pallas_call(
        _attn_in_kernel,
        grid=(t // tile,),
        in_specs=[pl.BlockSpec((tile, d), lambda i: (i, 0)),
                  _const_spec((1, d)),
                  pl.BlockSpec((None, 1, tile), lambda i: (i, 0, 0)),
                  _const_spec(freq.shape), _const_spec(expand.shape), _const_spec((d, n))],
        out_specs=pl.BlockSpec((tile, n), lambda i: (i, 0)),
        out_shape=jax.ShapeDtypeStruct((t, n), BF16),
        compiler_params=_params(1),
        name="attn_in",
    )(x, g, pos, freq, expand, w)


def _score_ahead(items, scores, finish):
    pending = scores(items[0])
    yield "S"
    for n, item in enumerate(items):
        current = pending
        if n + 1 < len(items):
            pending = scores(items[n + 1])
            yield "S"
        finish(item, current)
        yield "F"


def _mem_attention_stages(mq_ref, mk_ref, mv_ref, mo_ref):
    rows = mq_ref.shape[0]

    def scores(item):
        r0, h = item
        sl = slice(h * MEM_HEAD_DIM, (h + 1) * MEM_HEAD_DIM)
        return lax.dot_general(mq_ref[r0:r0 + BLOCK, sl], mk_ref[:, sl], (((1,), (1,)), ((), ())),
                               preferred_element_type=F32)

    def finish(item, s):
        r0, h = item
        sl = slice(h * MEM_HEAD_DIM, (h + 1) * MEM_HEAD_DIM)
        m = jnp.max(s, axis=-1, keepdims=True)
        p = jnp.exp2(s - m)
        den = jnp.sum(p, axis=-1, keepdims=True)
        o = jnp.dot(p.astype(BF16), mv_ref[:, sl], preferred_element_type=F32)
        mo_ref[r0:r0 + BLOCK, sl] = (o / den).astype(BF16)

    return _score_ahead([(r0, h) for r0 in range(0, rows, BLOCK) for h in range(MEM_HEADS)], scores, finish)


def _mem_attention(mq_ref, mk_ref, mv_ref, mo_ref):
    for _ in _mem_attention_stages(mq_ref, mk_ref, mv_ref, mo_ref):
        pass


def _window_attention_stages(sink_ref, q_ref, k_refs, v_refs, o_ref, has_prev, has_next):
    nq = q_ref.shape[0] // BLOCK
    key = lax.broadcasted_iota(jnp.int32, (BLOCK, ATTN_GROUP * BLOCK), 0)
    qry = lax.broadcasted_iota(jnp.int32, (BLOCK, ATTN_GROUP * BLOCK), 1) % BLOCK
    band_prev = key >= qry
    band_next = key <= qry
    lo = lax.broadcasted_iota(jnp.int32, (BLOCK, LANES), 1) < ATTN_HEAD_DIM
    top = lax.broadcasted_iota(jnp.int32, (LANES, BLOCK), 0) < ATTN_HEAD_DIM
    k_all = jnp.concatenate([r[...] for r in k_refs], axis=0)
    v_all = jnp.concatenate([r[...] for r in v_refs], axis=0)
    zero = jnp.zeros((), BF16)
    def scores(item):
        j, h = item
        r0 = j * BLOCK
        kh = k_all[r0:r0 + 3 * BLOCK, h * LANES:(h + 1) * LANES]
        parts = []
        for g in range(ATTN_GROUP):
            slab = q_ref[r0:r0 + BLOCK, (2 * h + g // 2) * LANES:(2 * h + g // 2 + 1) * LANES]
            parts.append(jnp.where(lo if g % 2 == 0 else ~lo, slab, zero))
        qs = jnp.concatenate(parts, axis=0)
        return lax.dot_general(kh, qs, (((1,), (1,)), ((), ())),
                               preferred_element_type=F32)

    def finish(item, st):
        j, h = item
        r0 = j * BLOCK
        m_prev = band_prev if j > 0 else band_prev & has_prev
        m_next = band_next if j < nq - 1 else band_next & has_next
        vh = v_all[r0:r0 + 3 * BLOCK, h * LANES:(h + 1) * LANES]
        s0 = jnp.where(m_prev, st[0:BLOCK], NEG)
        s1 = st[BLOCK:2 * BLOCK]
        s2 = jnp.where(m_next, st[2 * BLOCK:3 * BLOCK], NEG)
        sink = jnp.concatenate(
            [jnp.full((1, BLOCK), sink_ref[ATTN_GROUP * h + g] * LOG2E, F32) for g in range(ATTN_GROUP)],
            axis=1)
        m = jnp.max(jnp.maximum(jnp.maximum(s0, s1), s2), axis=0, keepdims=True)
        m = jnp.maximum(m, sink)
        p0, p1, p2 = jnp.exp2(s0 - m), jnp.exp2(s1 - m), jnp.exp2(s2 - m)
        den = jnp.sum(p0 + p1 + p2, axis=0, keepdims=True) + jnp.exp2(sink - m)
        pt = jnp.concatenate([p0, p1, p2], axis=0).astype(BF16)
        ot = lax.dot_general(vh, pt, (((0,), (0,)), ((), ())),
                             preferred_element_type=F32)
        ot = ot / den
        for pair in range(2):
            even = ot[:, (2 * pair) * BLOCK:(2 * pair + 1) * BLOCK]
            odd = ot[:, (2 * pair + 1) * BLOCK:(2 * pair + 2) * BLOCK]
            slab = jnp.where(top, even, odd).T
            o_ref[r0:r0 + BLOCK, (2 * h + pair) * LANES:(2 * h + pair + 1) * LANES] = slab.astype(BF16)

    return _score_ahead([(j, h) for j in range(nq) for h in range(ATTN_KV_HEADS)], scores, finish)


MLP_COARSE = (D_MODEL, FF_CHUNK, D_MODEL // 2)
MLP_FINE = (2 * LANES, 2 * LANES, 2 * LANES)


def _mlp_stages(x_ref, y_ref, mo_ref, wo_ref, g_ref, wup_ref, wdn_ref, result, widths=MLP_COARSE):
    proj_w, up_w, down_w = widths
    mix_w = y_ref.shape[-1]
    pieces = []
    for c in range(0, D_MODEL, proj_w):
        pieces.append(x_ref[:, c:c + proj_w]
                      + jnp.dot(y_ref[...], wo_ref[0:mix_w, c:c + proj_w], preferred_element_type=F32)
                      + jnp.dot(mo_ref[...], wo_ref[mix_w:, c:c + proj_w], preferred_element_type=F32))
        yield
    x1 = jnp.concatenate(pieces, axis=1)
    xn = _rms(x1, g_ref[...]).astype(BF16)
    hids = []
    for c in range(0, D_FF, up_w):
        hid = jnp.dot(xn, wup_ref[:, c:c + up_w], preferred_element_type=F32)
        hid = jnp.maximum(hid, 0.0)
        hids.append((hid * hid).astype(BF16))
        yield
    hid = jnp.concatenate(hids, axis=1)
    pieces = []
    for c in range(0, D_MODEL, down_w):
        pieces.append(x1[:, c:c + down_w]
                      + jnp.dot(hid, wdn_ref[:, c:c + down_w], preferred_element_type=F32))
        if c + down_w == D_MODEL:
            result.append(jnp.concatenate(pieces, axis=1))
        yield


def _attn_mlp_kernel(sink_ref, x_ref, q_ref, kvp_ref, kvc_ref, kvn_ref,
                     mq_ref, mk_ref, mv_ref, wo_ref, g_ref, wup_ref, wdn_ref, fg_ref, o_ref,
                     y_ref, mo_ref, *, tiles_per_row, final):
    k = pl.program_id(0)
    kk = jnp.minimum(k, pl.num_programs(0) - 2)
    t_idx = kk % tiles_per_row

    @pl.when(k == 0)
    def _():
        y_ref[...] = jnp.zeros_like(y_ref)
        mo_ref[...] = jnp.zeros_like(mo_ref)

    def attention():
        kv_refs = (kvp_ref, kvc_ref, kvn_ref)
        yield from _window_attention_stages(sink_ref, q_ref,
                                            [r.at[:, 0:KV_DUP_W] for r in kv_refs],
                                            [r.at[:, KV_DUP_W:2 * KV_DUP_W] for r in kv_refs], y_ref,
                                            t_idx > 0, t_idx < tiles_per_row - 1)
        yield from _mem_attention_stages(mq_ref, mk_ref, mv_ref, mo_ref)

    result = []
    attn = attention()
    mlp = _mlp_stages(x_ref, y_ref, mo_ref, wo_ref, g_ref, wup_ref, wdn_ref, result, MLP_FINE)
    next(attn)
    next(mlp)
    next(attn)
    for _ in range(D_MODEL // MLP_FINE[0] - 1):
        next(mlp)
    done = object()
    for tag in attn:
        if tag == "S":
            next(mlp, done)
    for _ in mlp:
        pass
    acc = result[0]
    if final:
        acc = _rms(acc, fg_ref[...])
    o_ref[...] = acc


def _attention_mlp(x3, p, memkv, sinks, layer, wo, g, wup, wdn, fg, final, batch, seq, tq):
    nt = seq // tq
    n_tiles = batch * nt
    per = tq // BLOCK
    nb = seq // BLOCK
    d = x3.shape[-1]
    p3 = p.reshape(batch, seq, ATTN_IN_COLS)
    kv_w = 2 * KV_DUP_W
    kvcol = Q_W // kv_w
    mqcol = (Q_W + kv_w) // MEM_W

    def attn_bt(k):
        kk = jnp.minimum(k, n_tiles - 1)
        return kk // nt, kk % nt

    def mlp_bt(k):
        return attn_bt(jnp.maximum(k - 1, 0))

    def kv_prev(k):
        b, i = attn_bt(k)
        return b, jnp.maximum(i * per - 1, 0), kvcol

    def kv_next(k):
        b, i = attn_bt(k)
        return b, jnp.minimum((i + 1) * per, nb - 1), kvcol

    kv_specs = [pl.BlockSpec((None, BLOCK, kv_w), kv_prev),
                pl.BlockSpec((None, tq, kv_w), lambda k: (*attn_bt(k), kvcol)),
                pl.BlockSpec((None, BLOCK, kv_w), kv_next)]

    return pl.pallas_call(
        functools.partial(_attn_mlp_kernel, tiles_per_row=nt, final=final),
        grid=(n_tiles + 1,),
        in_specs=[pl.BlockSpec(memory_space=pltpu.SMEM),
                  pl.BlockSpec((None, tq, d), lambda k: (*mlp_bt(k), 0)),
                  pl.BlockSpec((None, tq, Q_W), lambda k: (*attn_bt(k), 0))]
        + kv_specs
        + [pl.BlockSpec((None, tq, MEM_W), lambda k: (*attn_bt(k), mqcol)),
           pl.BlockSpec((None, MEM_LEN, MEM_W), lambda k: (attn_bt(k)[0], 0, 2 * layer)),
           pl.BlockSpec((None, MEM_LEN, MEM_W), lambda k: (attn_bt(k)[0], 0, 2 * layer + 1)),
           _const_spec(wo.shape), _const_spec((1, d)), _const_spec(wup.shape), _const_spec(wdn.shape),
           _const_spec((1, d))],
        out_specs=pl.BlockSpec((None, tq, d), lambda k: (*mlp_bt(k), 0)),
        out_shape=jax.ShapeDtypeStruct(x3.shape, F32),
        scratch_shapes=[pltpu.VMEM((tq, Q_W), BF16), pltpu.VMEM((tq, MEM_W), BF16)],
        compiler_params=_params(1),
        name="attention_mlp",
    )(sinks, x3, p3, p3, p3, p3, p3, memkv, memkv, wo, g, wup, wdn, fg)


def _permuted_conv(xp, before, after, cw_ref, cb_ref, ext_ref):
    ts, w = xp.shape
    n = ts // SUBLANES
    xp = xp.reshape(n, SUBLANES, w)
    ext_ref[1:n + 1] = xp
    sub = lax.broadcasted_iota(jnp.int32, (SUBLANES, w), 0)
    ext_ref[0] = jnp.where(sub == 0, before, pltpu.roll(xp[n - 1], 1, 0))
    ext_ref[n + 1] = jnp.where(sub == SUBLANES - 1, after[0:1], pltpu.roll(xp[0], SUBLANES - 1, 0))
    ext_ref[n + 2] = jnp.where(sub == SUBLANES - 1, after[1:2], pltpu.roll(xp[1], SUBLANES - 1, 0))
    xc = cb_ref[...][None]
    for tap in range(CONV_WIDTH):
        xc = xc + ext_ref[tap:tap + n] * cw_ref[tap:tap + 1, :][None]
    return xc.reshape(ts, w)


def _lru_gate_block(blk, xc, xcb, wg_ref, ba_ref, bx_ref, lam_ref, a_ref, u_ref):
    sl = slice(blk * LRU_BLOCK_DIM, (blk + 1) * LRU_BLOCK_DIM)
    gates = jnp.dot(xcb[:, sl], wg_ref[blk], preferred_element_type=F32)
    lam = lam_ref[:, sl]
    softplus_neg_lam = jnp.maximum(-lam, 0.0) + jnp.log1p(jnp.exp(-jnp.abs(lam)))
    rate = (-0.5 * LRU_C * LOG2E) * softplus_neg_lam
    t_rec = jnp.tanh(gates[:, :LRU_BLOCK_DIM] + ba_ref[:, sl])
    t_in = jnp.tanh(gates[:, LRU_BLOCK_DIM:] + bx_ref[:, sl])
    a = jnp.exp2(rate * t_rec + rate)
    a_ref[:, sl] = a
    z = 1.0 - a * a
    root = jnp.where(z > 0.0, z * lax.rsqrt(z), 0.0)
    u_ref[:, sl] = root * ((0.5 * t_in + 0.5) * xc[:, sl])


def _lru_scan(a_ref, u_ref, h_ref, carry_ref, reverse):
    ts, w = a_ref.shape
    n = ts // SUBLANES
    order = range(n - 1, -1, -1) if reverse else range(n)

    def rows(ref, j):
        return ref[j * SUBLANES:(j + 1) * SUBLANES, :]

    local = jnp.zeros((SUBLANES, w), F32)
    prod = jnp.ones((SUBLANES, w), F32)
    for j in order:
        a = rows(a_ref, j)
        local = a * local + rows(u_ref, j)
        prod = a * prod
    c = carry_ref[...]
    inflow = [None] * SUBLANES
    for s in (range(SUBLANES - 1, -1, -1) if reverse else range(SUBLANES)):
        inflow[s] = c
        c = prod[s:s + 1] * c + local[s:s + 1]
    carry_ref[...] = c
    h = jnp.concatenate(inflow, axis=0)
    for j in order:
        h = rows(a_ref, j) * h + rows(u_ref, j)
        h_ref[j * SUBLANES:(j + 1) * SUBLANES, :] = h


def _lru_stages(prev_ref, cur_ref, next_ref, first, last, lru_refs, scratch, reverse, pre, emit):
    perm_ref, cw_ref, cb_ref, wg_ref, ba_ref, bx_ref, lam_ref = lru_refs
    ext_ref, a_ref, u_ref, h_ref, carry_ref = scratch
    rows = cur_ref.shape[0]
    starts = list(range(0, rows, LRU_TILE))
    starts = starts[::-1] if reverse else starts
    permuted = [(jnp.dot(perm_ref[...], cur_ref[r0:r0 + LRU_TILE, :], preferred_element_type=F32), pre(r0))
                for r0 in starts]
    yield
    for r0, (xp, aux) in zip(starts, permuted):
        if r0 == 0:
            before = jnp.where(first, 0.0, prev_ref[BF16_ROWS - 1:BF16_ROWS, :].astype(F32))
        else:
            before = cur_ref[r0 - 1:r0, :].astype(F32)
        if r0 + LRU_TILE == rows:
            after = jnp.where(last, 0.0, next_ref[0:2, :].astype(F32))
        else:
            after = cur_ref[r0 + LRU_TILE:r0 + LRU_TILE + 2, :].astype(F32)
        xc = _permuted_conv(xp, before, after, cw_ref, cb_ref, ext_ref)
        xcb = xc.astype(BF16)
        for blk in range(LRU_BLOCKS):
            _lru_gate_block(blk, xc, xcb, wg_ref, ba_ref, bx_ref, lam_ref, a_ref, u_ref)
            yield
        _lru_scan(a_ref, u_ref, h_ref, carry_ref, reverse)
        emit(r0, aux)
        yield


def _lru_in_stages(x_ref, g_ref, w_ref, mk_ref, mv_ref, xg_ref, mq_ref, mo_ref):
    xn = _rms(x_ref[...], g_ref[...]).astype(BF16)
    xg_w = xg_ref.shape[-1]
    piece = MLP_FINE[0]
    for c in range(0, LRU_IN_COLS, piece):
        acc = jnp.dot(xn, w_ref[:, c:c + piece], preferred_element_type=F32)
        if c < xg_w:
            xg_ref[:, c:c + piece] = acc.astype(BF16)
        else:
            mq_ref[:, c - xg_w:c - xg_w + piece] = (acc * (LOG2E * MEM_HEAD_DIM ** -0.5)).astype(BF16)
        yield
    yield from _mem_attention_stages(mq_ref, mk_ref, mv_ref, mo_ref)


def _lru_in_fwd_kernel(x_ref, g_ref, w_ref, mk_ref, mv_ref,
                       perm_ref, cw_ref, cb_ref, wg_ref, ba_ref, bx_ref, lam_ref,
                       xg_ref, mo_ref, hf_ref,
                       xprev_ref, tail_ref, mq_ref, ext_ref, a_ref, u_ref, h_ref, carry_ref,
                       *, blocks_per_row):
    k = pl.program_id(0)
    t_prev = jnp.maximum(k - 1, 0) % blocks_per_row

    @pl.when(k == 0)
    def _():
        xprev_ref[...] = jnp.zeros_like(xprev_ref)
        tail_ref[...] = jnp.zeros_like(tail_ref)

    @pl.when(t_prev == 0)
    def _():
        carry_ref[...] = jnp.zeros_like(carry_ref)

    def emit(r0, aux):
        hf_ref[r0:r0 + LRU_TILE, :] = h_ref[...].astype(BF16)

    proj = _lru_in_stages(x_ref, g_ref, w_ref, mk_ref, mv_ref, xg_ref, mq_ref, mo_ref)
    lru = _lru_stages(tail_ref, xprev_ref, xg_ref.at[:, 0:LRU_WIDTH],
                      t_prev == 0, t_prev == blocks_per_row - 1,
                      (perm_ref, cw_ref, cb_ref, wg_ref, ba_ref, bx_ref, lam_ref),
                      (ext_ref, a_ref, u_ref, h_ref, carry_ref), False, lambda r0: None, emit)
    done = object()
    next(lru)
    next(proj)
    traced = 1
    stages_per_tile = LRU_BLOCKS + 1
    for n, _ in enumerate(lru):
        if n == stages_per_tile - 1:
            assert traced * MLP_FINE[0] >= LRU_WIDTH
        if n % 2 == 1:
            next(proj, done)
            traced += 1
    for _ in proj:
        pass
    rows = xprev_ref.shape[0]
    tail_ref[...] = xprev_ref[rows - BF16_ROWS:rows, :]
    xprev_ref[...] = xg_ref[:, 0:LRU_WIDTH]


def _lru_rev_mlp_kernel(x_ref, mo_ref, xp_ref, xg_ref, xn_ref, hf_ref,
                        perm_ref, unperm_ref, cw_ref, cb_ref, wg_ref, ba_ref, bx_ref, lam_ref,
                        wo_ref, g_ref, wup_ref, wdn_ref, fg_ref, o_ref,
                        ext_ref, a_ref, u_ref, h_ref, carry_ref, y_ref, *, blocks_per_row, final):
    k = pl.program_id(0)
    kk = jnp.minimum(k, pl.num_programs(0) - 2)
    t_blk = blocks_per_row - 1 - kk % blocks_per_row

    @pl.when(k == 0)
    def _():
        y_ref[...] = jnp.zeros_like(y_ref)

    @pl.when(kk % blocks_per_row == 0)
    def _():
        carry_ref[...] = jnp.zeros_like(carry_ref)

    def pre(r0):
        return jnp.dot(perm_ref[...], xg_ref[r0:r0 + LRU_TILE, LRU_WIDTH:2 * LRU_WIDTH],
                       preferred_element_type=F32)

    def emit(r0, gate):
        inner = gate * (GELU_C0 + GELU_C1 * (gate * gate))
        hsum = hf_ref[r0:r0 + LRU_TILE, :].astype(F32) + h_ref[...]
        y = (hsum * gate) * (0.5 * jnp.tanh(inner) + 0.5)
        y_ref[r0:r0 + LRU_TILE, :] = jnp.dot(unperm_ref[...], y.astype(BF16),
                                              preferred_element_type=F32).astype(BF16)

    result = []
    mlp = _mlp_stages(x_ref, y_ref, mo_ref, wo_ref, g_ref, wup_ref, wdn_ref, result, MLP_FINE)
    lru = _lru_stages(xp_ref, xg_ref.at[:, 0:LRU_WIDTH], xn_ref, t_blk == 0, t_blk == blocks_per_row - 1,
                      (perm_ref, cw_ref, cb_ref, wg_ref, ba_ref, bx_ref, lam_ref),
                      (ext_ref, a_ref, u_ref, h_ref, carry_ref), True, pre, emit)
    next(lru)
    for _ in range(D_MODEL // MLP_FINE[0]):
        next(mlp)
    done = object()
    for _ in lru:
        next(mlp, done)
    for _ in mlp:
        pass
    acc = result[0]
    if final:
        acc = _rms(acc, fg_ref[...])
    o_ref[...] = acc


def _lru_weight_specs():
    return [_const_spec((CONV_WIDTH, LRU_WIDTH)), _const_spec((1, LRU_WIDTH)),
            _const_spec((LRU_BLOCKS, LRU_BLOCK_DIM, 2 * LRU_BLOCK_DIM)),
            _const_spec((1, LRU_WIDTH)), _const_spec((1, LRU_WIDTH)), _const_spec((1, LRU_WIDTH))]


def _lru_scratch():
    ts = LRU_TILE
    return [pltpu.VMEM((ts // SUBLANES + CONV_WIDTH - 1, SUBLANES, LRU_WIDTH), F32),
            pltpu.VMEM((ts, LRU_WIDTH), F32),
            pltpu.VMEM((ts, LRU_WIDTH), F32),
            pltpu.VMEM((ts, LRU_WIDTH), F32),
            pltpu.VMEM((1, LRU_WIDTH), F32)]


def _chunk_permutation(ts):
    n = ts // SUBLANES
    r = np.arange(ts)
    src = (r % SUBLANES) * n + r // SUBLANES
    return jnp.asarray(src[:, None] == np.arange(ts)[None, :], BF16)


def _lru_in_forward(x3, g, w_in, memkv, layer, perm, weights, batch, seq, blk):
    nt = seq // blk
    n_blocks = batch * nt
    d = x3.shape[-1]

    def cur_bt(k):
        kk = jnp.minimum(k, n_blocks - 1)
        return kk // nt, kk % nt

    def prev_bt(k):
        return cur_bt(jnp.maximum(k - 1, 0))

    return pl.pallas_call(
        functools.partial(_lru_in_fwd_kernel, blocks_per_row=nt),
        grid=(n_blocks + 1,),
        in_specs=[pl.BlockSpec((None, blk, d), lambda k: (*cur_bt(k), 0)),
                  _const_spec((1, d)), _const_spec((d, LRU_IN_COLS)),
                  pl.BlockSpec((None, MEM_LEN, MEM_W), lambda k: (cur_bt(k)[0], 0, 2 * layer)),
                  pl.BlockSpec((None, MEM_LEN, MEM_W), lambda k: (cur_bt(k)[0], 0, 2 * layer + 1)),
                  _const_spec((LRU_TILE, LRU_TILE))]
        + _lru_weight_specs(),
        out_specs=[pl.BlockSpec((None, blk, 2 * LRU_WIDTH), lambda k: (*cur_bt(k), 0)),
                   pl.BlockSpec((None, blk, MEM_W), lambda k: (*cur_bt(k), 0)),
                   pl.BlockSpec((None, blk, LRU_WIDTH), lambda k: (*prev_bt(k), 0))],
        out_shape=[jax.ShapeDtypeStruct((batch, seq, 2 * LRU_WIDTH), BF16),
                   jax.ShapeDtypeStruct((batch, seq, MEM_W), BF16),
                   jax.ShapeDtypeStruct((batch, seq, LRU_WIDTH), BF16)],
        scratch_shapes=[pltpu.VMEM((blk, LRU_WIDTH), BF16),
                        pltpu.VMEM((BF16_ROWS, LRU_WIDTH), BF16),
                        pltpu.VMEM((blk, MEM_W), BF16)]
        + _lru_scratch(),
        compiler_params=_params(1),
        name="lru_in_forward",
    )(x3, g, w_in, memkv, memkv, perm, *weights)


def _lru_reverse_mlp(x3, mo3, xg3, hf, perm, weights, wo, g, wup, wdn, fg, final, batch, seq, blk):
    nt = seq // blk
    n_blocks = batch * nt
    per = blk // BF16_ROWS
    last_halo = seq // BF16_ROWS - 1
    d = x3.shape[-1]

    def lru_bt(k):
        kk = jnp.minimum(k, n_blocks - 1)
        return kk // nt, nt - 1 - kk % nt

    def mlp_bt(k):
        return lru_bt(jnp.maximum(k - 1, 0))

    def at_mlp(col):
        return lambda k: (*mlp_bt(k), col)

    def at_lru(col):
        return lambda k: (*lru_bt(k), col)

    def prev_idx(k):
        b, t = lru_bt(k)
        return b, jnp.maximum(t * per - 1, 0), 0

    def next_idx(k):
        b, t = lru_bt(k)
        return b, jnp.minimum((t + 1) * per, last_halo), 0

    return pl.pallas_call(
        functools.partial(_lru_rev_mlp_kernel, blocks_per_row=nt, final=final),
        grid=(n_blocks + 1,),
        in_specs=[pl.BlockSpec((None, blk, d), at_mlp(0)),
                  pl.BlockSpec((None, blk, MEM_W), at_mlp(0)),
                  pl.BlockSpec((None, BF16_ROWS, LRU_WIDTH), prev_idx),
                  pl.BlockSpec((None, blk, 2 * LRU_WIDTH), at_lru(0)),
                  pl.BlockSpec((None, BF16_ROWS, LRU_WIDTH), next_idx),
                  pl.BlockSpec((None, blk, LRU_WIDTH), at_lru(0)),
                  _const_spec((LRU_TILE, LRU_TILE)), _const_spec((LRU_TILE, LRU_TILE))]
        + _lru_weight_specs()
        + [_const_spec(wo.shape), _const_spec((1, d)), _const_spec(wup.shape), _const_spec(wdn.shape),
           _const_spec((1, d))],
        out_specs=pl.BlockSpec((None, blk, d), at_mlp(0)),
        out_shape=jax.ShapeDtypeStruct(x3.shape, F32),
        scratch_shapes=_lru_scratch() + [pltpu.VMEM((blk, LRU_WIDTH), BF16)],
        compiler_params=_params(1),
        name="lru_reverse_mlp",
    )(x3, mo3, xg3, xg3, xg3, hf, perm, perm.T, *weights, wo, g, wup, wdn, fg)


def _dup_heads(w):
    d = w.shape[0]
    w4 = w.reshape(d, ATTN_KV_HEADS, 1, ATTN_HEAD_DIM)
    return jnp.broadcast_to(w4, (d, ATTN_KV_HEADS, 2, ATTN_HEAD_DIM)).reshape(d, KV_DUP_W)


def _rotary_tables(tile):
    half = ROPE_DIM // 2
    inv_freq = ROPE_THETA ** (-2.0 * jnp.arange(half, dtype=F32) / ROPE_DIM)
    freq = jnp.broadcast_to(inv_freq[:, None], (half, tile))
    d = np.arange(LANES) % ATTN_HEAD_DIM
    first = d < half
    second = (d >= half) & (d < ROPE_DIM)
    pick = np.arange(half)[:, None] == (d % half)[None, :]
    zeros8 = np.zeros((half, LANES), np.float32)
    cos_rows = np.where(pick & (d < ROPE_DIM)[None, :], 1.0, 0.0)
    lo_rows = np.where(pick & first[None, :], -1.0, 0.0)
    hi_rows = np.where(pick & second[None, :], 1.0, 0.0)
    ones_rows = np.zeros((BF16_ROWS, LANES), np.float32)
    ones_rows[0] = np.where(d < ROPE_DIM, 0.0, 1.0)
    zeros16 = np.zeros((BF16_ROWS, LANES), np.float32)
    cos_cols = np.concatenate([cos_rows, cos_rows, zeros8, zeros8, ones_rows], axis=0)
    lo_cols = np.concatenate([zeros8, zeros8, lo_rows, lo_rows, zeros16], axis=0)
    hi_cols = np.concatenate([zeros8, zeros8, hi_rows, hi_rows, zeros16], axis=0)
    return freq, jnp.asarray(np.concatenate([cos_cols, lo_cols, hi_cols], axis=1), BF16)


def _gate_weights(wa, wx):
    return (0.5 * jnp.concatenate([wa, wx], axis=-1)).astype(BF16)


def kernel(x, mem, positions, mix_norm, mlp_norm, mem_norm, final_norm, w_mem_kv, w_out, w_up, w_down,
           attn_w_in, attn_sinks, lru_w_in, lru_conv_w, lru_conv_b, lru_wa, lru_ba, lru_wx, lru_bx,
           lru_lambda):
    batch, seq, d = x.shape
    t = batch * seq
    depth = mix_norm.shape[0]
    xf = x.reshape(t, d)
    ones_d = jnp.ones((1, d), F32)

    w_kv_all = jnp.concatenate([w_mem_kv[l] for l in range(depth)], axis=1).astype(BF16)
    memkv = _norm_proj(mem.reshape(batch * MEM_LEN, d), mem_norm.reshape(1, d), w_kv_all,
                       tile=batch * MEM_LEN // 2)
    memkv = memkv.reshape(batch, MEM_LEN, depth * 2 * MEM_W)

    pos_rows = positions.reshape(t // PROJ_TILE, 1, PROJ_TILE).astype(jnp.int32)
    rot_freq, rot_expand = _rotary_tables(PROJ_TILE)

    for l in range(depth):
        kind, j = l % 2, l // 2
        g_mix = mix_norm[l].reshape(1, d)
        final = l == depth - 1
        mlp_weights = (w_out[l].astype(BF16), mlp_norm[l].reshape(1, d), w_up[l].astype(BF16),
                       w_down[l].astype(BF16), final_norm.reshape(1, d) if final else ones_d)
        if kind == 0:
            w = attn_w_in[j]
            w_in = jnp.concatenate([w[:, :Q_W], _dup_heads(w[:, Q_W:Q_W + KV_W]),
                                    _dup_heads(w[:, Q_W + KV_W:Q_W + 2 * KV_W]),
                                    w[:, Q_W + 2 * KV_W:]], axis=1).astype(BF16)
            p = _attn_in(xf, g_mix, pos_rows, rot_freq, rot_expand, w_in, PROJ_TILE)
            xf = _attention_mlp(xf.reshape(batch, seq, d), p, memkv, attn_sinks[j].astype(F32), l,
                                *mlp_weights, final, batch, seq, ATTN_TILE).reshape(t, d)
        else:
            x3 = xf.reshape(batch, seq, d)
            cw, cb = lru_conv_w[j], lru_conv_b[j].reshape(1, LRU_WIDTH)

            def dir_weights(dirn):
                return (cw, cb, _gate_weights(lru_wa[j, dirn], lru_wx[j, dirn]),
                        0.5 * lru_ba[j, dirn].reshape(1, LRU_WIDTH),
                        0.5 * lru_bx[j, dirn].reshape(1, LRU_WIDTH),
                        lru_lambda[j, dirn].reshape(1, LRU_WIDTH))

            perm = _chunk_permutation(LRU_TILE)
            xg, mo, hf = _lru_in_forward(x3, g_mix, lru_w_in[j].astype(BF16), memkv, l, perm,
                                         dir_weights(0), batch, seq, ROW_TILE)
            xf = _lru_reverse_mlp(x3, mo, xg, hf, perm, dir_weights(1),
                                  *mlp_weights, final, batch, seq, ROW_TILE).reshape(t, d)
    return xf.reshape(batch, seq, d)
```

```python
import functools

import jax
import jax.numpy as jnp
import numpy as np
from jax import lax
from jax.experimental import pallas as pl
from jax.experimental.pallas import tpu as pltpu

F32 = jnp.float32
BF16 = jnp.bfloat16

D_MODEL = 1024
EPS = 1e-6
ATTN_HEADS = 16
ATTN_KV_HEADS = 4
ATTN_HEAD_DIM = 64
ATTN_GROUP = 4
WINDOW = 128
BLOCK = 128
ROPE_THETA = 500000.0
ROPE_DIM = 16
Q_W = 1024
KV_W = 256
MEM_LEN = 256
MEM_HEADS = 4
MEM_HEAD_DIM = 128
MEM_W = 512
LRU_WIDTH = 1024
LRU_BLOCKS = 8
LRU_BLOCK_DIM = 128
LRU_C = 8.0
CONV_WIDTH = 4
D_FF = 4096
NEG = -1e30

LANES = 128
SUBLANES = 8
BF16_ROWS = 16
KV_DUP_W = 2 * KV_W
ATTN_IN_COLS = Q_W + 2 * KV_DUP_W + MEM_W
LRU_IN_COLS = 2 * LRU_WIDTH + MEM_W
VMEM_LIMIT = 56 * 1024 * 1024

ROW_TILE = 512
LRU_TILE = 256
FF_CHUNK = 1024
PROJ_CHUNK = 512
PROJ_TILE = 1024
ATTN_TILE = 512
LOG2E = 1.4426950408889634
GELU_C0 = 0.7978845608028654
GELU_C1 = GELU_C0 * 0.044715


def _const_spec(shape):
    nd = len(shape)
    return pl.BlockSpec(shape, lambda *_: (0,) * nd, pipeline_mode=pl.Buffered(1))


def _params(n_axes):
    return pltpu.CompilerParams(dimension_semantics=("arbitrary",) * n_axes,
                                vmem_limit_bytes=VMEM_LIMIT)


def _rms(x, g):
    ms = jnp.mean(x * x, axis=-1, keepdims=True)
    return x * lax.rsqrt(ms + EPS) * g


def _norm_proj_kernel(x_ref, g_ref, w_ref, o_ref, *, chunk):
    xn = _rms(x_ref[...], g_ref[...]).astype(BF16)
    for c in range(0, o_ref.shape[-1], chunk):
        o_ref[:, c:c + chunk] = jnp.dot(xn, w_ref[:, c:c + chunk],
                                        preferred_element_type=F32).astype(BF16)


def _norm_proj(x, g, w, tile):
    t, d = x.shape
    n = w.shape[1]
    return pl.pallas_call(
        functools.partial(_norm_proj_kernel, chunk=PROJ_CHUNK),
        grid=(t // tile,),
        in_specs=[pl.BlockSpec((tile, d), lambda i: (i, 0)),
                  _const_spec((1, d)), _const_spec((d, n))],
        out_specs=pl.BlockSpec((tile, n), lambda i: (i, 0)),
        out_shape=jax.ShapeDtypeStruct((t, n), BF16),
        compiler_params=_params(1),
        name="norm_proj",
    )(x, g, w)


def _split_bf16(x):
    hi = x.astype(BF16).astype(F32)
    return hi, x - hi


def _attn_in_kernel(x_ref, g_ref, pos_ref, freq_ref, expand_ref, w_ref, o_ref):
    xn = _rms(x_ref[...], g_ref[...]).astype(BF16)
    ang = freq_ref[...] * pos_ref[...].astype(F32)
    cos_hi, cos_lo = _split_bf16(jnp.cos(ang))
    sin_hi_, sin_lo_ = _split_bf16(jnp.sin(ang))
    one = jnp.where(lax.broadcasted_iota(jnp.int32, (BF16_ROWS, ang.shape[1]), 0) == 0, 1.0, 0.0)
    trig = jnp.concatenate([cos_hi, cos_lo, sin_hi_, sin_lo_, one], axis=0).astype(BF16)
    tables = lax.dot_general(trig, expand_ref[...], (((0,), (0,)), ((), ())),
                             preferred_element_type=F32)
    cos = tables[:, 0:LANES]
    sin_lo = tables[:, LANES:2 * LANES]
    sin_hi = tables[:, 2 * LANES:3 * LANES]
    q_scale = LOG2E * ATTN_HEAD_DIM ** -0.5
    mem_scale = LOG2E * MEM_HEAD_DIM ** -0.5
    half = ROPE_DIM // 2
    rot_cols = Q_W + KV_DUP_W
    for c in range(0, ATTN_IN_COLS, 2 * LANES):
        acc = jnp.dot(xn, w_ref[:, c:c + 2 * LANES], preferred_element_type=F32)
        for s in range(2):
            col = c + s * LANES
            slab = acc[:, s * LANES:(s + 1) * LANES]
            if col < rot_cols:
                slab = (slab * cos + pltpu.roll(slab, LANES - half, 1) * sin_lo
                        + pltpu.roll(slab, half, 1) * sin_hi)
                if col < Q_W:
                    slab = slab * q_scale
            elif col >= Q_W + 2 * KV_DUP_W:
                slab = slab * mem_scale
            o_ref[:, col:col + LANES] = slab.astype(BF16)


def _attn_in(x, g, pos, freq, expand, w, tile):
    t, d = x.shape
    n = w.shape[1]
    return pl.pallas_call(
        _attn_in_kernel,
        grid=(t // tile,),
        in_specs=[pl.BlockSpec((tile, d), lambda i: (i, 0)),
                  _const_spec((1, d)),
                  pl.BlockSpec((None, 1, tile), lambda i: (i, 0, 0)),
                  _const_spec(freq.shape), _const_spec(expand.shape), _const_spec((d, n))],
        out_specs=pl.BlockSpec((tile, n), lambda i: (i, 0)),
        out_shape=jax.ShapeDtypeStruct((t, n), BF16),
        compiler_params=_params(1),
        name="attn_in",
    )(x, g, pos, freq, expand, w)


def _score_ahead(items, scores, finish):
    pending = scores(items[0])
    yield "S"
    for n, item in enumerate(items):
        current = pending
        if n + 1 < len(items):
            pending = scores(items[n + 1])
            yield "S"
        finish(item, current)
        yield "F"


def _mem_attention_stages(mq_ref, mk_ref, mv_ref, mo_ref):
    rows = mq_ref.shape[0]

    def scores(item):
        r0, h = item
        sl = slice(h * MEM_HEAD_DIM, (h + 1) * MEM_HEAD_DIM)
        return lax.dot_general(mq_ref[r0:r0 + BLOCK, sl], mk_ref[:, sl], (((1,), (1,)), ((), ())),
                               preferred_element_type=F32)

    def finish(item, s):
        r0, h = item
        sl = slice(h * MEM_HEAD_DIM, (h + 1) * MEM_HEAD_DIM)
        m = jnp.max(s, axis=-1, keepdims=True)
        p = jnp.exp2(s - m)
        den = jnp.sum(p, axis=-1, keepdims=True)
        o = jnp.dot(p.astype(BF16), mv_ref[:, sl], preferred_element_type=F32)
        mo_ref[r0:r0 + BLOCK, sl] = (o / den).astype(BF16)

    return _score_ahead([(r0, h) for r0 in range(0, rows, BLOCK) for h in range(MEM_HEADS)], scores, finish)


def _mem_attention(mq_ref, mk_ref, mv_ref, mo_ref):
    for _ in _mem_attention_stages(mq_ref, mk_ref, mv_ref, mo_ref):
        pass


def _window_attention_stages(sink_ref, q_ref, k_refs, v_refs, o_ref, has_prev, has_next):
    nq = q_ref.shape[0] // BLOCK
    key = lax.broadcasted_iota(jnp.int32, (BLOCK, ATTN_GROUP * BLOCK), 0)
    qry = lax.broadcasted_iota(jnp.int32, (BLOCK, ATTN_GROUP * BLOCK), 1) % BLOCK
    band_prev = key >= qry
    band_next = key <= qry
    lo = lax.broadcasted_iota(jnp.int32, (BLOCK, LANES), 1) < ATTN_HEAD_DIM
    top = lax.broadcasted_iota(jnp.int32, (LANES, BLOCK), 0) < ATTN_HEAD_DIM
    k_all = jnp.concatenate([r[...] for r in k_refs], axis=0)
    v_all = jnp.concatenate([r[...] for r in v_refs], axis=0)
    zero = jnp.zeros((), BF16)
    def scores(item):
        j, h = item
        r0 = j * BLOCK
        kh = k_all[r0:r0 + 3 * BLOCK, h * LANES:(h + 1) * LANES]
        parts = []
        for g in range(ATTN_GROUP):
            slab = q_ref[r0:r0 + BLOCK, (2 * h + g // 2) * LANES:(2 * h + g // 2 + 1) * LANES]
            parts.append(jnp.where(lo if g % 2 == 0 else ~lo, slab, zero))
        qs = jnp.concatenate(parts, axis=0)
        return lax.dot_general(kh, qs, (((1,), (1,)), ((), ())),
                               preferred_element_type=F32)

    def finish(item, st):
        j, h = item
        r0 = j * BLOCK
        m_prev = band_prev if j > 0 else band_prev & has_prev
        m_next = band_next if j < nq - 1 else band_next & has_next
        vh = v_all[r0:r0 + 3 * BLOCK, h * LANES:(h + 1) * LANES]
        s0 = jnp.where(m_prev, st[0:BLOCK], NEG)
        s1 = st[BLOCK:2 * BLOCK]
        s2 = jnp.where(m_next, st[2 * BLOCK:3 * BLOCK], NEG)
        sink = jnp.concatenate(
            [jnp.full((1, BLOCK), sink_ref[ATTN_GROUP * h + g] * LOG2E, F32) for g in range(ATTN_GROUP)],
            axis=1)
        m = jnp.max(jnp.maximum(jnp.maximum(s0, s1), s2), axis=0, keepdims=True)
        m = jnp.maximum(m, sink)
        p0, p1, p2 = jnp.exp2(s0 - m), jnp.exp2(s1 - m), jnp.exp2(s2 - m)
        den = jnp.sum(p0 + p1 + p2, axis=0, keepdims=True) + jnp.exp2(sink - m)
        pt = jnp.concatenate([p0, p1, p2], axis=0).astype(BF16)
        ot = lax.dot_general(vh, pt, (((0,), (0,)), ((), ())),
                             preferred_element_type=F32)
        ot = ot / den
        for pair in range(2):
            even = ot[:, (2 * pair) * BLOCK:(2 * pair + 1) * BLOCK]
            odd = ot[:, (2 * pair + 1) * BLOCK:(2 * pair + 2) * BLOCK]
            slab = jnp.where(top, even, odd).T
            o_ref[r0:r0 + BLOCK, (2 * h + pair) * LANES:(2 * h + pair + 1) * LANES] = slab.astype(BF16)

    return _score_ahead([(j, h) for j in range(nq) for h in range(ATTN_KV_HEADS)], scores, finish)


MLP_COARSE = (D_MODEL, FF_CHUNK, D_MODEL // 2)
MLP_FINE = (2 * LANES, 2 * LANES, 2 * LANES)


def _mlp_stages(x_ref, y_ref, mo_ref, wo_ref, g_ref, wup_ref, wdn_ref, result, widths=MLP_COARSE):
    proj_w, up_w, down_w = widths
    mix_w = y_ref.shape[-1]
    pieces = []
    for c in range(0, D_MODEL, proj_w):
        pieces.append(x_ref[:, c:c + proj_w]
                      + jnp.dot(y_ref[...], wo_ref[0:mix_w, c:c + proj_w], preferred_element_type=F32)
                      + jnp.dot(mo_ref[...], wo_ref[mix_w:, c:c + proj_w], preferred_element_type=F32))
        yield
    x1 = jnp.concatenate(pieces, axis=1)
    xn = _rms(x1, g_ref[...]).astype(BF16)
    hids = []
    for c in range(0, D_FF, up_w):
        hid = jnp.dot(xn, wup_ref[:, c:c + up_w], preferred_element_type=F32)
        hid = jnp.maximum(hid, 0.0)
        hids.append((hid * hid).astype(BF16))
        yield
    hid = jnp.concatenate(hids, axis=1)
    pieces = []
    for c in range(0, D_MODEL, down_w):
        pieces.append(x1[:, c:c + down_w]
                      + jnp.dot(hid, wdn_ref[:, c:c + down_w], preferred_element_type=F32))
        if c + down_w == D_MODEL:
            result.append(jnp.concatenate(pieces, axis=1))
        yield


def _attn_mlp_kernel(sink_ref, x_ref, q_ref, kvp_ref, kvc_ref, kvn_ref,
                     mq_ref, mk_ref, mv_ref, wo_ref, g_ref, wup_ref, wdn_ref, fg_ref, o_ref,
                     y_ref, mo_ref, *, tiles_per_row, final):
    k = pl.program_id(0)
    kk = jnp.minimum(k, pl.num_programs(0) - 2)
    t_idx = kk % tiles_per_row

    @pl.when(k == 0)
    def _():
        y_ref[...] = jnp.zeros_like(y_ref)
        mo_ref[...] = jnp.zeros_like(mo_ref)

    def attention():
        kv_refs = (kvp_ref, kvc_ref, kvn_ref)
        yield from _window_attention_stages(sink_ref, q_ref,
                                            [r.at[:, 0:KV_DUP_W] for r in kv_refs],
                                            [r.at[:, KV_DUP_W:2 * KV_DUP_W] for r in kv_refs], y_ref,
                                            t_idx > 0, t_idx < tiles_per_row - 1)
        yield from _mem_attention_stages(mq_ref, mk_ref, mv_ref, mo_ref)

    result = []
    attn = attention()
    mlp = _mlp_stages(x_ref, y_ref, mo_ref, wo_ref, g_ref, wup_ref, wdn_ref, result, MLP_FINE)
    next(attn)
    next(mlp)
    next(attn)
    for _ in range(D_MODEL // MLP_FINE[0] - 1):
        next(mlp)
    done = object()
    for tag in attn:
        if tag == "S":
            next(mlp, done)
    for _ in mlp:
        pass
    acc = result[0]
    if final:
        acc = _rms(acc, fg_ref[...])
    o_ref[...] = acc


def _attention_mlp(x3, p, memkv, sinks, layer, wo, g, wup, wdn, fg, final, batch, seq, tq):
    nt = seq // tq
    n_tiles = batch * nt
    per = tq // BLOCK
    nb = seq // BLOCK
    d = x3.shape[-1]
    p3 = p.reshape(batch, seq, ATTN_IN_COLS)
    kv_w = 2 * KV_DUP_W
    kvcol = Q_W // kv_w
    mqcol = (Q_W + kv_w) // MEM_W

    def attn_bt(k):
        kk = jnp.minimum(k, n_tiles - 1)
        return kk // nt, kk % nt

    def mlp_bt(k):
        return attn_bt(jnp.maximum(k - 1, 0))

    def kv_prev(k):
        b, i = attn_bt(k)
        return b, jnp.maximum(i * per - 1, 0), kvcol

    def kv_next(k):
        b, i = attn_bt(k)
        return b, jnp.minimum((i + 1) * per, nb - 1), kvcol

    kv_specs = [pl.BlockSpec((None, BLOCK, kv_w), kv_prev),
                pl.BlockSpec((None, tq, kv_w), lambda k: (*attn_bt(k), kvcol)),
                pl.BlockSpec((None, BLOCK, kv_w), kv_next)]

    return pl.pallas_call(
        functools.partial(_attn_mlp_kernel, tiles_per_row=nt, final=final),
        grid=(n_tiles + 1,),
        in_specs=[pl.BlockSpec(memory_space=pltpu.SMEM),
                  pl.BlockSpec((None, tq, d), lambda k: (*mlp_bt(k), 0)),
                  pl.BlockSpec((None, tq, Q_W), lambda k: (*attn_bt(k), 0))]
        + kv_specs
        + [pl.BlockSpec((None, tq, MEM_W), lambda k: (*attn_bt(k), mqcol)),
           pl.BlockSpec((None, MEM_LEN, MEM_W), lambda k: (attn_bt(k)[0], 0, 2 * layer)),
           pl.BlockSpec((None, MEM_LEN, MEM_W), lambda k: (attn_bt(k)[0], 0, 2 * layer + 1)),
           _const_spec(wo.shape), _const_spec((1, d)), _const_spec(wup.shape), _const_spec(wdn.shape),
           _const_spec((1, d))],
        out_specs=pl.BlockSpec((None, tq, d), lambda k: (*mlp_bt(k), 0)),
        out_shape=jax.ShapeDtypeStruct(x3.shape, F32),
        scratch_shapes=[pltpu.VMEM((tq, Q_W), BF16), pltpu.VMEM((tq, MEM_W), BF16)],
        compiler_params=_params(1),
        name="attention_mlp",
    )(sinks, x3, p3, p3, p3, p3, p3, memkv, memkv, wo, g, wup, wdn, fg)


def _permuted_conv(xp, before, after, cw_ref, cb_ref, ext_ref):
    ts, w = xp.shape
    n = ts // SUBLANES
    xp = xp.reshape(n, SUBLANES, w)
    ext_ref[1:n + 1] = xp
    sub = lax.broadcasted_iota(jnp.int32, (SUBLANES, w), 0)
    ext_ref[0] = jnp.where(sub == 0, before, pltpu.roll(xp[n - 1], 1, 0))
    ext_ref[n + 1] = jnp.where(sub == SUBLANES - 1, after[0:1], pltpu.roll(xp[0], SUBLANES - 1, 0))
    ext_ref[n + 2] = jnp.where(sub == SUBLANES - 1, after[1:2], pltpu.roll(xp[1], SUBLANES - 1, 0))
    xc = cb_ref[...][None]
    for tap in range(CONV_WIDTH):
        xc = xc + ext_ref[tap:tap + n] * cw_ref[tap:tap + 1, :][None]
    return xc.reshape(ts, w)


def _lru_gate_block(blk, xc, xcb, wg_ref, ba_ref, bx_ref, lam_ref, a_ref, u_ref):
    sl = slice(blk * LRU_BLOCK_DIM, (blk + 1) * LRU_BLOCK_DIM)
    gates = jnp.dot(xcb[:, sl], wg_ref[blk], preferred_element_type=F32)
    lam = lam_ref[:, sl]
    softplus_neg_lam = jnp.maximum(-lam, 0.0) + jnp.log1p(jnp.exp(-jnp.abs(lam)))
    rate = (-0.5 * LRU_C * LOG2E) * softplus_neg_lam
    t_rec = jnp.tanh(gates[:, :LRU_BLOCK_DIM] + ba_ref[:, sl])
    t_in = jnp.tanh(gates[:, LRU_BLOCK_DIM:] + bx_ref[:, sl])
    a = jnp.exp2(rate * t_rec + rate)
    a_ref[:, sl] = a
    z = 1.0 - a * a
    root = jnp.where(z > 0.0, z * lax.rsqrt(z), 0.0)
    u_ref[:, sl] = root * ((0.5 * t_in + 0.5) * xc[:, sl])


def _lru_scan(a_ref, u_ref, h_ref, carry_ref, reverse):
    ts, w = a_ref.shape
    n = ts // SUBLANES
    order = range(n - 1, -1, -1) if reverse else range(n)

    def rows(ref, j):
        return ref[j * SUBLANES:(j + 1) * SUBLANES, :]

    local = jnp.zeros((SUBLANES, w), F32)
    prod = jnp.ones((SUBLANES, w), F32)
    for j in order:
        a = rows(a_ref, j)
        local = a * local + rows(u_ref, j)
        prod = a * prod
    c = carry_ref[...]
    inflow = [None] * SUBLANES
    for s in (range(SUBLANES - 1, -1, -1) if reverse else range(SUBLANES)):
        inflow[s] = c
        c = prod[s:s + 1] * c + local[s:s + 1]
    carry_ref[...] = c
    h = jnp.concatenate(inflow, axis=0)
    for j in order:
        h = rows(a_ref, j) * h + rows(u_ref, j)
        h_ref[j * SUBLANES:(j + 1) * SUBLANES, :] = h


def _lru_stages(prev_ref, cur_ref, next_ref, first, last, lru_refs, scratch, reverse, pre, emit):
    perm_ref, cw_ref, cb_ref, wg_ref, ba_ref, bx_ref, lam_ref = lru_refs
    ext_ref, a_ref, u_ref, h_ref, carry_ref = scratch
    rows = cur_ref.shape[0]
    starts = list(range(0, rows, LRU_TILE))
    starts = starts[::-1] if reverse else starts
    permuted = [(jnp.dot(perm_ref[...], cur_ref[r0:r0 + LRU_TILE, :], preferred_element_type=F32), pre(r0))
                for r0 in starts]
    yield
    for r0, (xp, aux) in zip(starts, permuted):
        if r0 == 0:
            before = jnp.where(first, 0.0, prev_ref[BF16_ROWS - 1:BF16_ROWS, :].astype(F32))
        else:
            before = cur_ref[r0 - 1:r0, :].astype(F32)
        if r0 + LRU_TILE == rows:
            after = jnp.where(last, 0.0, next_ref[0:2, :].astype(F32))
        else:
            after = cur_ref[r0 + LRU_TILE:r0 + LRU_TILE + 2, :].astype(F32)
        xc = _permuted_conv(xp, before, after, cw_ref, cb_ref, ext_ref)
        xcb = xc.astype(BF16)
        for blk in range(LRU_BLOCKS):
            _lru_gate_block(blk, xc, xcb, wg_ref, ba_ref, bx_ref, lam_ref, a_ref, u_ref)
            yield
        _lru_scan(a_ref, u_ref, h_ref, carry_ref, reverse)
        emit(r0, aux)
        yield


def _lru_in_stages(x_ref, g_ref, w_ref, xg_ref, mq_ref):
    xn = _rms(x_ref[...], g_ref[...]).astype(BF16)
    xg_w = xg_ref.shape[-1]
    piece = MLP_FINE[0]
    for c in range(0, LRU_IN_COLS, piece):
        acc = jnp.dot(xn, w_ref[:, c:c + piece], preferred_element_type=F32)
        if c < xg_w:
            xg_ref[:, c:c + piece] = acc.astype(BF16)
        else:
            mq_ref[:, c - xg_w:c - xg_w + piece] = (acc * (LOG2E * MEM_HEAD_DIM ** -0.5)).astype(BF16)
        yield


def _lru_in_fwd_kernel(x_ref, g_ref, w_ref,
                       perm_ref, cw_ref, cb_ref, wg_ref, ba_ref, bx_ref, lam_ref,
                       xg_ref, mq_ref, hf_ref,
                       xprev_ref, tail_ref, ext_ref, a_ref, u_ref, h_ref, carry_ref,
                       *, blocks_per_row):
    k = pl.program_id(0)
    t_prev = jnp.maximum(k - 1, 0) % blocks_per_row

    @pl.when(k == 0)
    def _():
        xprev_ref[...] = jnp.zeros_like(xprev_ref)
        tail_ref[...] = jnp.zeros_like(tail_ref)

    @pl.when(t_prev == 0)
    def _():
        carry_ref[...] = jnp.zeros_like(carry_ref)

    def emit(r0, aux):
        hf_ref[r0:r0 + LRU_TILE, :] = h_ref[...].astype(BF16)

    proj = _lru_in_stages(x_ref, g_ref, w_ref, xg_ref, mq_ref)
    lru = _lru_stages(tail_ref, xprev_ref, xg_ref.at[:, 0:LRU_WIDTH],
                      t_prev == 0, t_prev == blocks_per_row - 1,
                      (perm_ref, cw_ref, cb_ref, wg_ref, ba_ref, bx_ref, lam_ref),
                      (ext_ref, a_ref, u_ref, h_ref, carry_ref), False, lambda r0: None, emit)
    done = object()
    next(lru)
    next(proj)
    traced = 1
    stages_per_tile = LRU_BLOCKS + 1
    for n, _ in enumerate(lru):
        if n == stages_per_tile - 1:
            assert traced * MLP_FINE[0] >= LRU_WIDTH
        if n % 2 == 1:
            next(proj, done)
            traced += 1
    for _ in proj:
        pass
    rows = xprev_ref.shape[0]
    tail_ref[...] = xprev_ref[rows - BF16_ROWS:rows, :]
    xprev_ref[...] = xg_ref[:, 0:LRU_WIDTH]


def _lru_rev_mlp_kernel(x_ref, mq_ref, mk_ref, mv_ref, xp_ref, xg_ref, xn_ref, hf_ref,
                        perm_ref, unperm_ref, cw_ref, cb_ref, wg_ref, ba_ref, bx_ref, lam_ref,
                        wo_ref, g_ref, wup_ref, wdn_ref, fg_ref, o_ref,
                        ext_ref, a_ref, u_ref, h_ref, carry_ref, y_ref, mo_ref, *, blocks_per_row, final):
    k = pl.program_id(0)
    kk = jnp.minimum(k, pl.num_programs(0) - 2)
    t_blk = blocks_per_row - 1 - kk % blocks_per_row

    @pl.when(k == 0)
    def _():
        y_ref[...] = jnp.zeros_like(y_ref)
        mo_ref[...] = jnp.zeros_like(mo_ref)

    @pl.when(kk % blocks_per_row == 0)
    def _():
        carry_ref[...] = jnp.zeros_like(carry_ref)

    def pre(r0):
        return jnp.dot(perm_ref[...], xg_ref[r0:r0 + LRU_TILE, LRU_WIDTH:2 * LRU_WIDTH],
                       preferred_element_type=F32)

    def emit(r0, gate):
        inner = gate * (GELU_C0 + GELU_C1 * (gate * gate))
        hsum = hf_ref[r0:r0 + LRU_TILE, :].astype(F32) + h_ref[...]
        y = (hsum * gate) * (0.5 * jnp.tanh(inner) + 0.5)
        y_ref[r0:r0 + LRU_TILE, :] = jnp.dot(unperm_ref[...], y.astype(BF16),
                                              preferred_element_type=F32).astype(BF16)

    result = []
    mlp = _mlp_stages(x_ref, y_ref, mo_ref, wo_ref, g_ref, wup_ref, wdn_ref, result, MLP_FINE)
    lru = _lru_stages(xp_ref, xg_ref.at[:, 0:LRU_WIDTH], xn_ref, t_blk == 0, t_blk == blocks_per_row - 1,
                      (perm_ref, cw_ref, cb_ref, wg_ref, ba_ref, bx_ref, lam_ref),
                      (ext_ref, a_ref, u_ref, h_ref, carry_ref), True, pre, emit)
    next(lru)
    for _ in range(D_MODEL // MLP_FINE[0]):
        next(mlp)
    done = object()
    for _ in lru:
        next(mlp, done)
    for _ in _mem_attention_stages(mq_ref, mk_ref, mv_ref, mo_ref):
        next(mlp, done)
    for _ in mlp:
        pass
    acc = result[0]
    if final:
        acc = _rms(acc, fg_ref[...])
    o_ref[...] = acc


def _lru_weight_specs():
    return [_const_spec((CONV_WIDTH, LRU_WIDTH)), _const_spec((1, LRU_WIDTH)),
            _const_spec((LRU_BLOCKS, LRU_BLOCK_DIM, 2 * LRU_BLOCK_DIM)),
            _const_spec((1, LRU_WIDTH)), _const_spec((1, LRU_WIDTH)), _const_spec((1, LRU_WIDTH))]


def _lru_scratch():
    ts = LRU_TILE
    return [pltpu.VMEM((ts // SUBLANES + CONV_WIDTH - 1, SUBLANES, LRU_WIDTH), F32),
            pltpu.VMEM((ts, LRU_WIDTH), F32),
            pltpu.VMEM((ts, LRU_WIDTH), F32),
            pltpu.VMEM((ts, LRU_WIDTH), F32),
            pltpu.VMEM((1, LRU_WIDTH), F32)]


def _chunk_permutation(ts):
    n = ts // SUBLANES
    r = np.arange(ts)
    src = (r % SUBLANES) * n + r // SUBLANES
    return jnp.asarray(src[:, None] == np.arange(ts)[None, :], BF16)


def _lru_in_forward(x3, g, w_in, perm, weights, batch, seq, blk):
    nt = seq // blk
    n_blocks = batch * nt
    d = x3.shape[-1]

    def cur_bt(k):
        kk = jnp.minimum(k, n_blocks - 1)
        return kk // nt, kk % nt

    def prev_bt(k):
        return cur_bt(jnp.maximum(k - 1, 0))

    return pl.pallas_call(
        functools.partial(_lru_in_fwd_kernel, blocks_per_row=nt),
        grid=(n_blocks + 1,),
        in_specs=[pl.BlockSpec((None, blk, d), lambda k: (*cur_bt(k), 0)),
                  _const_spec((1, d)), _const_spec((d, LRU_IN_COLS)),
                  _const_spec((LRU_TILE, LRU_TILE))]
        + _lru_weight_specs(),
        out_specs=[pl.BlockSpec((None, blk, 2 * LRU_WIDTH), lambda k: (*cur_bt(k), 0)),
                   pl.BlockSpec((None, blk, MEM_W), lambda k: (*cur_bt(k), 0)),
                   pl.BlockSpec((None, blk, LRU_WIDTH), lambda k: (*prev_bt(k), 0))],
        out_shape=[jax.ShapeDtypeStruct((batch, seq, 2 * LRU_WIDTH), BF16),
                   jax.ShapeDtypeStruct((batch, seq, MEM_W), BF16),
                   jax.ShapeDtypeStruct((batch, seq, LRU_WIDTH), BF16)],
        scratch_shapes=[pltpu.VMEM((blk, LRU_WIDTH), BF16),
                        pltpu.VMEM((BF16_ROWS, LRU_WIDTH), BF16)]
        + _lru_scratch(),
        compiler_params=_params(1),
        name="lru_in_forward",
    )(x3, g, w_in, perm, *weights)


def _lru_reverse_mlp(x3, mq3, memkv, layer, xg3, hf, perm, weights, wo, g, wup, wdn, fg, final, batch,
                     seq, blk):
    nt = seq // blk
    n_blocks = batch * nt
    per = blk // BF16_ROWS
    last_halo = seq // BF16_ROWS - 1
    d = x3.shape[-1]

    def lru_bt(k):
        kk = jnp.minimum(k, n_blocks - 1)
        return kk // nt, nt - 1 - kk % nt

    def mlp_bt(k):
        return lru_bt(jnp.maximum(k - 1, 0))

    def at_mlp(col):
        return lambda k: (*mlp_bt(k), col)

    def at_lru(col):
        return lambda k: (*lru_bt(k), col)

    def prev_idx(k):
        b, t = lru_bt(k)
        return b, jnp.maximum(t * per - 1, 0), 0

    def next_idx(k):
        b, t = lru_bt(k)
        return b, jnp.minimum((t + 1) * per, last_halo), 0

    return pl.pallas_call(
        functools.partial(_lru_rev_mlp_kernel, blocks_per_row=nt, final=final),
        grid=(n_blocks + 1,),
        in_specs=[pl.BlockSpec((None, blk, d), at_mlp(0)),
                  pl.BlockSpec((None, blk, MEM_W), at_lru(0)),
                  pl.BlockSpec((None, MEM_LEN, MEM_W), lambda k: (lru_bt(k)[0], 0, 2 * layer)),
                  pl.BlockSpec((None, MEM_LEN, MEM_W), lambda k: (lru_bt(k)[0], 0, 2 * layer + 1)),
                  pl.BlockSpec((None, BF16_ROWS, LRU_WIDTH), prev_idx),
                  pl.BlockSpec((None, blk, 2 * LRU_WIDTH), at_lru(0)),
                  pl.BlockSpec((None, BF16_ROWS, LRU_WIDTH), next_idx),
                  pl.BlockSpec((None, blk, LRU_WIDTH), at_lru(0)),
                  _const_spec((LRU_TILE, LRU_TILE)), _const_spec((LRU_TILE, LRU_TILE))]
        + _lru_weight_specs()
        + [_const_spec(wo.shape), _const_spec((1, d)), _const_spec(wup.shape), _const_spec(wdn.shape),
           _const_spec((1, d))],
        out_specs=pl.BlockSpec((None, blk, d), at_mlp(0)),
        out_shape=jax.ShapeDtypeStruct(x3.shape, F32),
        scratch_shapes=_lru_scratch() + [pltpu.VMEM((blk, LRU_WIDTH), BF16),
                                         pltpu.VMEM((blk, MEM_W), BF16)],
        compiler_params=_params(1),
        name="lru_reverse_mlp",
    )(x3, mq3, memkv, memkv, xg3, xg3, xg3, hf, perm, perm.T, *weights, wo, g, wup, wdn, fg)


def _dup_heads(w):
    d = w.shape[0]
    w4 = w.reshape(d, ATTN_KV_HEADS, 1, ATTN_HEAD_DIM)
    return jnp.broadcast_to(w4, (d, ATTN_KV_HEADS, 2, ATTN_HEAD_DIM)).reshape(d, KV_DUP_W)


def _rotary_tables(tile):
    half = ROPE_DIM // 2
    inv_freq = ROPE_THETA ** (-2.0 * jnp.arange(half, dtype=F32) / ROPE_DIM)
    freq = jnp.broadcast_to(inv_freq[:, None], (half, tile))
    d = np.arange(LANES) % ATTN_HEAD_DIM
    first = d < half
    second = (d >= half) & (d < ROPE_DIM)
    pick = np.arange(half)[:, None] == (d % half)[None, :]
    zeros8 = np.zeros((half, LANES), np.float32)
    cos_rows = np.where(pick & (d < ROPE_DIM)[None, :], 1.0, 0.0)
    lo_rows = np.where(pick & first[None, :], -1.0, 0.0)
    hi_rows = np.where(pick & second[None, :], 1.0, 0.0)
    ones_rows = np.zeros((BF16_ROWS, LANES), np.float32)
    ones_rows[0] = np.where(d < ROPE_DIM, 0.0, 1.0)
    zeros16 = np.zeros((BF16_ROWS, LANES), np.float32)
    cos_cols = np.concatenate([cos_rows, cos_rows, zeros8, zeros8, ones_rows], axis=0)
    lo_cols = np.concatenate([zeros8, zeros8, lo_rows, lo_rows, zeros16], axis=0)
    hi_cols = np.concatenate([zeros8, zeros8, hi_rows, hi_rows, zeros16], axis=0)
    return freq, jnp.asarray(np.concatenate([cos_cols, lo_cols, hi_cols], axis=1), BF16)


def _gate_weights(wa, wx):
    return (0.5 * jnp.concatenate([wa, wx], axis=-1)).astype(BF16)


def kernel(x, mem, positions, mix_norm, mlp_norm, mem_norm, final_norm, w_mem_kv, w_out, w_up, w_down,
           attn_w_in, attn_sinks, lru_w_in, lru_conv_w, lru_conv_b, lru_wa, lru_ba, lru_wx, lru_bx,
           lru_lambda):
    batch, seq, d = x.shape
    t = batch * seq
    depth = mix_norm.shape[0]
    xf = x.reshape(t, d)
    ones_d = jnp.ones((1, d), F32)

    w_kv_all = jnp.concatenate([w_mem_kv[l] for l in range(depth)], axis=1).astype(BF16)
    memkv = _norm_proj(mem.reshape(batch * MEM_LEN, d), mem_norm.reshape(1, d), w_kv_all,
                       tile=batch * MEM_LEN // 2)
    memkv = memkv.reshape(batch, MEM_LEN, depth * 2 * MEM_W)

    pos_rows = positions.reshape(t // PROJ_TILE, 1, PROJ_TILE).astype(jnp.int32)
    rot_freq, rot_expand = _rotary_tables(PROJ_TILE)

    for l in range(depth):
        kind, j = l % 2, l // 2
        g_mix = mix_norm[l].reshape(1, d)
        final = l == depth - 1
        mlp_weights = (w_out[l].astype(BF16), mlp_norm[l].reshape(1, d), w_up[l].astype(BF16),
                       w_down[l].astype(BF16), final_norm.reshape(1, d) if final else ones_d)
        if kind == 0:
            w = attn_w_in[j]
            w_in = jnp.concatenate([w[:, :Q_W], _dup_heads(w[:, Q_W:Q_W + KV_W]),
                                    _dup_heads(w[:, Q_W + KV_W:Q_W + 2 * KV_W]),
                                    w[:, Q_W + 2 * KV_W:]], axis=1).astype(BF16)
            p = _attn_in(xf, g_mix, pos_rows, rot_freq, rot_expand, w_in, PROJ_TILE)
            xf = _attention_mlp(xf.reshape(batch, seq, d), p, memkv, attn_sinks[j].astype(F32), l,
                                *mlp_weights, final, batch, seq, ATTN_TILE).reshape(t, d)
        else:
            x3 = xf.reshape(batch, seq, d)
            cw, cb = lru_conv_w[j], lru_conv_b[j].reshape(1, LRU_WIDTH)

            def dir_weights(dirn):
                return (cw, cb, _gate_weights(lru_wa[j, dirn], lru_wx[j, dirn]),
                        0.5 * lru_ba[j, dirn].reshape(1, LRU_WIDTH),
                        0.5 * lru_bx[j, dirn].reshape(1, LRU_WIDTH),
                        lru_lambda[j, dirn].reshape(1, LRU_WIDTH))

            perm = _chunk_permutation(LRU_TILE)
            xg, mq, hf = _lru_in_forward(x3, g_mix, lru_w_in[j].astype(BF16), perm,
                                         dir_weights(0), batch, seq, ROW_TILE)
            xf = _lru_reverse_mlp(x3, mq, memkv, l, xg, hf, perm, dir_weights(1),
                                  *mlp_weights, final, batch, seq, ROW_TILE).reshape(t, d)
    return xf.reshape(batch, seq, d)
```

```python
import functools

import jax
import jax.numpy as jnp
import numpy as np
from jax import lax
from jax.experimental import pallas as pl
from jax.experimental.pallas import tpu as pltpu

F32 = jnp.float32
BF16 = jnp.bfloat16

D_MODEL = 1024
EPS = 1e-6
ATTN_HEADS = 16
ATTN_KV_HEADS = 4
ATTN_HEAD_DIM = 64
ATTN_GROUP = 4
WINDOW = 128
BLOCK = 128
ROPE_THETA = 500000.0
ROPE_DIM = 16
Q_W = 1024
KV_W = 256
MEM_LEN = 256
MEM_HEADS = 4
MEM_HEAD_DIM = 128
MEM_W = 512
LRU_WIDTH = 1024
LRU_BLOCKS = 8
LRU_BLOCK_DIM = 128
LRU_C = 8.0
CONV_WIDTH = 4
D_FF = 4096
NEG = -1e30

LANES = 128
SUBLANES = 8
BF16_ROWS = 16
KV_DUP_W = 2 * KV_W
ATTN_IN_COLS = Q_W + 2 * KV_DUP_W + MEM_W
LRU_IN_COLS = 2 * LRU_WIDTH + MEM_W
VMEM_LIMIT = 56 * 1024 * 1024

ROW_TILE = 512
LRU_TILE = 256
FF_CHUNK = 1024
PROJ_CHUNK = 512
PROJ_TILE = 1024
ATTN_TILE = 512
LOG2E = 1.4426950408889634
GELU_C0 = 0.7978845608028654
GELU_C1 = GELU_C0 * 0.044715


def _const_spec(shape):
    nd = len(shape)
    return pl.BlockSpec(shape, lambda *_: (0,) * nd, pipeline_mode=pl.Buffered(1))


def _params(n_axes):
    return pltpu.CompilerParams(dimension_semantics=("arbitrary",) * n_axes,
                                vmem_limit_bytes=VMEM_LIMIT)


def _rms(x, g):
    ms = jnp.mean(x * x, axis=-1, keepdims=True)
    return x * lax.rsqrt(ms + EPS) * g


def _norm_proj_kernel(x_ref, g_ref, w_ref, o_ref, *, chunk):
    xn = _rms(x_ref[...], g_ref[...]).astype(BF16)
    for c in range(0, o_ref.shape[-1], chunk):
        o_ref[:, c:c + chunk] = jnp.dot(xn, w_ref[:, c:c + chunk],
                                        preferred_element_type=F32).astype(BF16)


def _norm_proj(x, g, w, tile):
    t, d = x.shape
    n = w.shape[1]
    return pl.pallas_call(
        functools.partial(_norm_proj_kernel, chunk=PROJ_CHUNK),
        grid=(t // tile,),
        in_specs=[pl.BlockSpec((tile, d), lambda i: (i, 0)),
                  _const_spec((1, d)), _const_spec((d, n))],
        out_specs=pl.BlockSpec((tile, n), lambda i: (i, 0)),
        out_shape=jax.ShapeDtypeStruct((t, n), BF16),
        compiler_params=_params(1),
        name="norm_proj",
    )(x, g, w)


def _split_bf16(x):
    hi = x.astype(BF16).astype(F32)
    return hi, x - hi


def _attn_in_kernel(x_ref, g_ref, pos_ref, freq_ref, expand_ref, w_ref, o_ref):
    xn = _rms(x_ref[...], g_ref[...]).astype(BF16)
    ang = freq_ref[...] * pos_ref[...].astype(F32)
    cos_hi, cos_lo = _split_bf16(jnp.cos(ang))
    sin_hi_, sin_lo_ = _split_bf16(jnp.sin(ang))
    one = jnp.where(lax.broadcasted_iota(jnp.int32, (BF16_ROWS, ang.shape[1]), 0) == 0, 1.0, 0.0)
    trig = jnp.concatenate([cos_hi, cos_lo, sin_hi_, sin_lo_, one], axis=0).astype(BF16)
    tables = lax.dot_general(trig, expand_ref[...], (((0,), (0,)), ((), ())),
                             preferred_element_type=F32)
    cos = tables[:, 0:LANES]
    sin_lo = tables[:, LANES:2 * LANES]
    sin_hi = tables[:, 2 * LANES:3 * LANES]
    q_scale = LOG2E * ATTN_HEAD_DIM ** -0.5
    mem_scale = LOG2E * MEM_HEAD_DIM ** -0.5
    half = ROPE_DIM // 2
    rot_cols = Q_W + KV_DUP_W
    for c in range(0, ATTN_IN_COLS, 2 * LANES):
        acc = jnp.dot(xn, w_ref[:, c:c + 2 * LANES], preferred_element_type=F32)
        for s in range(2):
            col = c + s * LANES
            slab = acc[:, s * LANES:(s + 1) * LANES]
            if col < rot_cols:
                slab = (slab * cos + pltpu.roll(slab, LANES - half, 1) * sin_lo
                        + pltpu.roll(slab, half, 1) * sin_hi)
                if col < Q_W:
                    slab = slab * q_scale
            elif col >= Q_W + 2 * KV_DUP_W:
                slab = slab * mem_scale
            o_ref[:, col:col + LANES] = slab.astype(BF16)


def _attn_in(x, g, pos, freq, expand, w, tile):
    t, d = x.shape
    n = w.shape[1]
    return pl.pallas_call(
        _attn_in_kernel,
        grid=(t // tile,),
        in_specs=[pl.BlockSpec((tile, d), lambda i: (i, 0)),
                  _const_spec((1, d)),
                  pl.BlockSpec((None, 1, tile), lambda i: (i, 0, 0)),
                  _const_spec(freq.shape), _const_spec(expand.shape), _const_spec((d, n))],
        out_specs=pl.BlockSpec((tile, n), lambda i: (i, 0)),
        out_shape=jax.ShapeDtypeStruct((t, n), BF16),
        compiler_params=_params(1),
        name="attn_in",
    )(x, g, pos, freq, expand, w)


def _score_ahead(items, scores, finish):
    pending = scores(items[0])
    yield "S"
    for n, item in enumerate(items):
        current = pending
        if n + 1 < len(items):
            pending = scores(items[n + 1])
            yield "S"
        finish(item, current)
        yield "F"


def _mem_attention_stages(mq_ref, mk_ref, mv_ref, mo_ref):
    rows = mq_ref.shape[0]

    def scores(item):
        r0, h = item
        sl = slice(h * MEM_HEAD_DIM, (h + 1) * MEM_HEAD_DIM)
        return lax.dot_general(mq_ref[r0:r0 + BLOCK, sl], mk_ref[:, sl], (((1,), (1,)), ((), ())),
                               preferred_element_type=F32)

    def finish(item, s):
        r0, h = item
        sl = slice(h * MEM_HEAD_DIM, (h + 1) * MEM_HEAD_DIM)
        m = jnp.max(s, axis=-1, keepdims=True)
        p = jnp.exp2(s - m)
        den = jnp.sum(p, axis=-1, keepdims=True)
        o = jnp.dot(p.astype(BF16), mv_ref[:, sl], preferred_element_type=F32)
        mo_ref[r0:r0 + BLOCK, sl] = (o / den).astype(BF16)

    return _score_ahead([(r0, h) for r0 in range(0, rows, BLOCK) for h in range(MEM_HEADS)], scores, finish)


def _mem_attention(mq_ref, mk_ref, mv_ref, mo_ref):
    for _ in _mem_attention_stages(mq_ref, mk_ref, mv_ref, mo_ref):
        pass


def _window_attention_stages(sink_ref, q_ref, k_refs, v_refs, o_ref, has_prev, has_next):
    nq = q_ref.shape[0] // BLOCK
    key = lax.broadcasted_iota(jnp.int32, (BLOCK, ATTN_GROUP * BLOCK), 0)
    qry = lax.broadcasted_iota(jnp.int32, (BLOCK, ATTN_GROUP * BLOCK), 1) % BLOCK
    band_prev = key >= qry
    band_next = key <= qry
    lo = lax.broadcasted_iota(jnp.int32, (BLOCK, LANES), 1) < ATTN_HEAD_DIM
    top = lax.broadcasted_iota(jnp.int32, (LANES, BLOCK), 0) < ATTN_HEAD_DIM
    k_all = jnp.concatenate([r[...] for r in k_refs], axis=0)
    v_all = jnp.concatenate([r[...] for r in v_refs], axis=0)
    zero = jnp.zeros((), BF16)
    def scores(item):
        j, h = item
        r0 = j * BLOCK
        kh = k_all[r0:r0 + 3 * BLOCK, h * LANES:(h + 1) * LANES]
        parts = []
        for g in range(ATTN_GROUP):
            slab = q_ref[r0:r0 + BLOCK, (2 * h + g // 2) * LANES:(2 * h + g // 2 + 1) * LANES]
            parts.append(jnp.where(lo if g % 2 == 0 else ~lo, slab, zero))
        qs = jnp.concatenate(parts, axis=0)
        return lax.dot_general(kh, qs, (((1,), (1,)), ((), ())),
                               preferred_element_type=F32)

    def finish(item, st):
        j, h = item
        r0 = j * BLOCK
        m_prev = band_prev if j > 0 else band_prev & has_prev
        m_next = band_next if j < nq - 1 else band_next & has_next
        vh = v_all[r0:r0 + 3 * BLOCK, h * LANES:(h + 1) * LANES]
        pts, dens = [], []
        for g in range(ATTN_GROUP):
            cols = slice(g * BLOCK, (g + 1) * BLOCK)
            s0 = jnp.where(m_prev[:, cols], st[0:BLOCK, cols], NEG)
            s1 = st[BLOCK:2 * BLOCK, cols]
            s2 = jnp.where(m_next[:, cols], st[2 * BLOCK:3 * BLOCK, cols], NEG)
            sink = sink_ref[ATTN_GROUP * h + g] * LOG2E
            m = jnp.max(jnp.maximum(jnp.maximum(s0, s1), s2), axis=0, keepdims=True)
            m = jnp.maximum(m, sink)
            p0, p1, p2 = jnp.exp2(s0 - m), jnp.exp2(s1 - m), jnp.exp2(s2 - m)
            dens.append(jnp.sum(p0 + p1 + p2, axis=0, keepdims=True) + jnp.exp2(sink - m))
            pts.append(jnp.concatenate([p0, p1, p2], axis=0).astype(BF16))
        den = jnp.concatenate(dens, axis=1)
        pt = jnp.concatenate(pts, axis=1)
        ot = lax.dot_general(vh, pt, (((0,), (0,)), ((), ())),
                             preferred_element_type=F32)
        ot = ot / den
        for pair in range(2):
            even = ot[:, (2 * pair) * BLOCK:(2 * pair + 1) * BLOCK]
            odd = ot[:, (2 * pair + 1) * BLOCK:(2 * pair + 2) * BLOCK]
            slab = jnp.where(top, even, odd).T
            o_ref[r0:r0 + BLOCK, (2 * h + pair) * LANES:(2 * h + pair + 1) * LANES] = slab.astype(BF16)

    return _score_ahead([(j, h) for j in range(nq) for h in range(ATTN_KV_HEADS)], scores, finish)


MLP_COARSE = (D_MODEL, FF_CHUNK, D_MODEL // 2)
MLP_FINE = (2 * LANES, 2 * LANES, 2 * LANES)


def _mlp_stages(x_ref, y_ref, mo_ref, wo_ref, g_ref, wup_ref, wdn_ref, result, widths=MLP_COARSE):
    proj_w, up_w, down_w = widths
    mix_w = y_ref.shape[-1]
    pieces = []
    for c in range(0, D_MODEL, proj_w):
        pieces.append(x_ref[:, c:c + proj_w]
                      + jnp.dot(y_ref[...], wo_ref[0:mix_w, c:c + proj_w], preferred_element_type=F32)
                      + jnp.dot(mo_ref[...], wo_ref[mix_w:, c:c + proj_w], preferred_element_type=F32))
        yield
    x1 = jnp.concatenate(pieces, axis=1)
    xn = _rms(x1, g_ref[...]).astype(BF16)
    hids = []
    for c in range(0, D_FF, up_w):
        hid = jnp.dot(xn, wup_ref[:, c:c + up_w], preferred_element_type=F32)
        hid = jnp.maximum(hid, 0.0)
        hids.append((hid * hid).astype(BF16))
        yield
    hid = jnp.concatenate(hids, axis=1)
    pieces = []
    for c in range(0, D_MODEL, down_w):
        pieces.append(x1[:, c:c + down_w]
                      + jnp.dot(hid, wdn_ref[:, c:c + down_w], preferred_element_type=F32))
        if c + down_w == D_MODEL:
            result.append(jnp.concatenate(pieces, axis=1))
        yield


def _attn_mlp_kernel(sink_ref, x_ref, q_ref, kvp_ref, kvc_ref, kvn_ref,
                     mq_ref, mk_ref, mv_ref, wo_ref, g_ref, wup_ref, wdn_ref, fg_ref, o_ref,
                     y_ref, mo_ref, *, tiles_per_row, final):
    k = pl.program_id(0)
    kk = jnp.minimum(k, pl.num_programs(0) - 2)
    t_idx = kk % tiles_per_row

    @pl.when(k == 0)
    def _():
        y_ref[...] = jnp.zeros_like(y_ref)
        mo_ref[...] = jnp.zeros_like(mo_ref)

    def attention():
        kv_refs = (kvp_ref, kvc_ref, kvn_ref)
        yield from _window_attention_stages(sink_ref, q_ref,
                                            [r.at[:, 0:KV_DUP_W] for r in kv_refs],
                                            [r.at[:, KV_DUP_W:2 * KV_DUP_W] for r in kv_refs], y_ref,
                                            t_idx > 0, t_idx < tiles_per_row - 1)
        yield from _mem_attention_stages(mq_ref, mk_ref, mv_ref, mo_ref)

    result = []
    attn = attention()
    mlp = _mlp_stages(x_ref, y_ref, mo_ref, wo_ref, g_ref, wup_ref, wdn_ref, result, MLP_FINE)
    next(attn)
    next(mlp)
    next(attn)
    for _ in range(D_MODEL // MLP_FINE[0] - 1):
        next(mlp)
    done = object()
    for tag in attn:
        if tag == "S":
            next(mlp, done)
    for _ in mlp:
        pass
    acc = result[0]
    if final:
        acc = _rms(acc, fg_ref[...])
    o_ref[...] = acc


def _attention_mlp(x3, p, memkv, sinks, layer, wo, g, wup, wdn, fg, final, batch, seq, tq):
    nt = seq // tq
    n_tiles = batch * nt
    per = tq // BLOCK
    nb = seq // BLOCK
    d = x3.shape[-1]
    p3 = p.reshape(batch, seq, ATTN_IN_COLS)
    kv_w = 2 * KV_DUP_W
    kvcol = Q_W // kv_w
    mqcol = (Q_W + kv_w) // MEM_W

    def attn_bt(k):
        kk = jnp.minimum(k, n_tiles - 1)
        return kk // nt, kk % nt

    def mlp_bt(k):
        return attn_bt(jnp.maximum(k - 1, 0))

    def kv_prev(k):
        b, i = attn_bt(k)
        return b, jnp.maximum(i * per - 1, 0), kvcol

    def kv_next(k):
        b, i = attn_bt(k)
        return b, jnp.minimum((i + 1) * per, nb - 1), kvcol

    kv_specs = [pl.BlockSpec((None, BLOCK, kv_w), kv_prev),
                pl.BlockSpec((None, tq, kv_w), lambda k: (*attn_bt(k), kvcol)),
                pl.BlockSpec((None, BLOCK, kv_w), kv_next)]

    return pl.pallas_call(
        functools.partial(_attn_mlp_kernel, tiles_per_row=nt, final=final),
        grid=(n_tiles + 1,),
        in_specs=[pl.BlockSpec(memory_space=pltpu.SMEM),
                  pl.BlockSpec((None, tq, d), lambda k: (*mlp_bt(k), 0)),
                  pl.BlockSpec((None, tq, Q_W), lambda k: (*attn_bt(k), 0))]
        + kv_specs
        + [pl.BlockSpec((None, tq, MEM_W), lambda k: (*attn_bt(k), mqcol)),
           pl.BlockSpec((None, MEM_LEN, MEM_W), lambda k: (attn_bt(k)[0], 0, 2 * layer)),
           pl.BlockSpec((None, MEM_LEN, MEM_W), lambda k: (attn_bt(k)[0], 0, 2 * layer + 1)),
           _const_spec(wo.shape), _const_spec((1, d)), _const_spec(wup.shape), _const_spec(wdn.shape),
           _const_spec((1, d))],
        out_specs=pl.BlockSpec((None, tq, d), lambda k: (*mlp_bt(k), 0)),
        out_shape=jax.ShapeDtypeStruct(x3.shape, F32),
        scratch_shapes=[pltpu.VMEM((tq, Q_W), BF16), pltpu.VMEM((tq, MEM_W), BF16)],
        compiler_params=_params(1),
        name="attention_mlp",
    )(sinks, x3, p3, p3, p3, p3, p3, memkv, memkv, wo, g, wup, wdn, fg)


def _permuted_conv(xp, before, after, cw_ref, cb_ref, ext_ref):
    ts, w = xp.shape
    n = ts // SUBLANES
    xp = xp.reshape(n, SUBLANES, w)
    ext_ref[1:n + 1] = xp
    sub = lax.broadcasted_iota(jnp.int32, (SUBLANES, w), 0)
    ext_ref[0] = jnp.where(sub == 0, before, pltpu.roll(xp[n - 1], 1, 0))
    ext_ref[n + 1] = jnp.where(sub == SUBLANES - 1, after[0:1], pltpu.roll(xp[0], SUBLANES - 1, 0))
    ext_ref[n + 2] = jnp.where(sub == SUBLANES - 1, after[1:2], pltpu.roll(xp[1], SUBLANES - 1, 0))
    xc = cb_ref[...][None]
    for tap in range(CONV_WIDTH):
        xc = xc + ext_ref[tap:tap + n] * cw_ref[tap:tap + 1, :][None]
    return xc.reshape(ts, w)


def _lru_gate_block(blk, xc, xcb, wg_ref, ba_ref, bx_ref, lam_ref, a_ref, u_ref):
    sl = slice(blk * LRU_BLOCK_DIM, (blk + 1) * LRU_BLOCK_DIM)
    gates = jnp.dot(xcb[:, sl], wg_ref[blk], preferred_element_type=F32)
    lam = lam_ref[:, sl]
    softplus_neg_lam = jnp.maximum(-lam, 0.0) + jnp.log1p(jnp.exp(-jnp.abs(lam)))
    rate = (-0.5 * LRU_C * LOG2E) * softplus_neg_lam
    t_rec = jnp.tanh(gates[:, :LRU_BLOCK_DIM] + ba_ref[:, sl])
    t_in = jnp.tanh(gates[:, LRU_BLOCK_DIM:] + bx_ref[:, sl])
    a = jnp.exp2(rate * t_rec + rate)
    a_ref[:, sl] = a
    z = 1.0 - a * a
    root = jnp.where(z > 0.0, z * lax.rsqrt(z), 0.0)
    u_ref[:, sl] = root * ((0.5 * t_in + 0.5) * xc[:, sl])


def _lru_scan(a_ref, u_ref, h_ref, carry_ref, reverse):
    ts, w = a_ref.shape
    n = ts // SUBLANES
    order = range(n - 1, -1, -1) if reverse else range(n)

    def rows(ref, j):
        return ref[j * SUBLANES:(j + 1) * SUBLANES, :]

    local = jnp.zeros((SUBLANES, w), F32)
    prod = jnp.ones((SUBLANES, w), F32)
    for j in order:
        a = rows(a_ref, j)
        local = a * local + rows(u_ref, j)
        prod = a * prod
    c = carry_ref[...]
    inflow = [None] * SUBLANES
    for s in (range(SUBLANES - 1, -1, -1) if reverse else range(SUBLANES)):
        inflow[s] = c
        c = prod[s:s + 1] * c + local[s:s + 1]
    carry_ref[...] = c
    h = jnp.concatenate(inflow, axis=0)
    for j in order:
        h = rows(a_ref, j) * h + rows(u_ref, j)
        h_ref[j * SUBLANES:(j + 1) * SUBLANES, :] = h


def _lru_stages(prev_ref, cur_ref, next_ref, first, last, lru_refs, scratch, reverse, pre, emit):
    perm_ref, cw_ref, cb_ref, wg_ref, ba_ref, bx_ref, lam_ref = lru_refs
    ext_ref, a_ref, u_ref, h_ref, carry_ref = scratch
    rows = cur_ref.shape[0]
    starts = list(range(0, rows, LRU_TILE))
    starts = starts[::-1] if reverse else starts
    permuted = [(jnp.dot(perm_ref[...], cur_ref[r0:r0 + LRU_TILE, :], preferred_element_type=F32), pre(r0))
                for r0 in starts]
    yield
    for r0, (xp, aux) in zip(starts, permuted):
        if r0 == 0:
            before = jnp.where(first, 0.0, prev_ref[BF16_ROWS - 1:BF16_ROWS, :].astype(F32))
        else:
            before = cur_ref[r0 - 1:r0, :].astype(F32)
        if r0 + LRU_TILE == rows:
            after = jnp.where(last, 0.0, next_ref[0:2, :].astype(F32))
        else:
            after = cur_ref[r0 + LRU_TILE:r0 + LRU_TILE + 2, :].astype(F32)
        xc = _permuted_conv(xp, before, after, cw_ref, cb_ref, ext_ref)
        xcb = xc.astype(BF16)
        for blk in range(LRU_BLOCKS):
            _lru_gate_block(blk, xc, xcb, wg_ref, ba_ref, bx_ref, lam_ref, a_ref, u_ref)
            yield
        _lru_scan(a_ref, u_ref, h_ref, carry_ref, reverse)
        emit(r0, aux)
        yield


def _lru_in_stages(x_ref, g_ref, w_ref, xg_ref, mq_ref):
    xn = _rms(x_ref[...], g_ref[...]).astype(BF16)
    xg_w = xg_ref.shape[-1]
    piece = MLP_FINE[0]
    for c in range(0, LRU_IN_COLS, piece):
        acc = jnp.dot(xn, w_ref[:, c:c + piece], preferred_element_type=F32)
        if c < xg_w:
            xg_ref[:, c:c + piece] = acc.astype(BF16)
        else:
            mq_ref[:, c - xg_w:c - xg_w + piece] = (acc * (LOG2E * MEM_HEAD_DIM ** -0.5)).astype(BF16)
        yield


def _lru_in_fwd_kernel(x_ref, g_ref, w_ref,
                       perm_ref, cw_ref, cb_ref, wg_ref, ba_ref, bx_ref, lam_ref,
                       xg_ref, mq_ref, hf_ref,
                       xprev_ref, tail_ref, ext_ref, a_ref, u_ref, h_ref, carry_ref,
                       *, blocks_per_row):
    k = pl.program_id(0)
    t_prev = jnp.maximum(k - 1, 0) % blocks_per_row

    @pl.when(k == 0)
    def _():
        xprev_ref[...] = jnp.zeros_like(xprev_ref)
        tail_ref[...] = jnp.zeros_like(tail_ref)

    @pl.when(t_prev == 0)
    def _():
        carry_ref[...] = jnp.zeros_like(carry_ref)

    def emit(r0, aux):
        hf_ref[r0:r0 + LRU_TILE, :] = h_ref[...].astype(BF16)

    proj = _lru_in_stages(x_ref, g_ref, w_ref, xg_ref, mq_ref)
    lru = _lru_stages(tail_ref, xprev_ref, xg_ref.at[:, 0:LRU_WIDTH],
                      t_prev == 0, t_prev == blocks_per_row - 1,
                      (perm_ref, cw_ref, cb_ref, wg_ref, ba_ref, bx_ref, lam_ref),
                      (ext_ref, a_ref, u_ref, h_ref, carry_ref), False, lambda r0: None, emit)
    done = object()
    next(lru)
    next(proj)
    traced = 1
    stages_per_tile = LRU_BLOCKS + 1
    for n, _ in enumerate(lru):
        if n == stages_per_tile - 1:
            assert traced * MLP_FINE[0] >= LRU_WIDTH
        if n % 2 == 1:
            next(proj, done)
            traced += 1
    for _ in proj:
        pass
    rows = xprev_ref.shape[0]
    tail_ref[...] = xprev_ref[rows - BF16_ROWS:rows, :]
    xprev_ref[...] = xg_ref[:, 0:LRU_WIDTH]


def _lru_rev_mlp_kernel(x_ref, mq_ref, mk_ref, mv_ref, xp_ref, xg_ref, xn_ref, hf_ref,
                        perm_ref, unperm_ref, cw_ref, cb_ref, wg_ref, ba_ref, bx_ref, lam_ref,
                        wo_ref, g_ref, wup_ref, wdn_ref, fg_ref, o_ref,
                        ext_ref, a_ref, u_ref, h_ref, carry_ref, y_ref, mo_ref, *, blocks_per_row, final):
    k = pl.program_id(0)
    kk = jnp.minimum(k, pl.num_programs(0) - 2)
    t_blk = blocks_per_row - 1 - kk % blocks_per_row

    @pl.when(k == 0)
    def _():
        y_ref[...] = jnp.zeros_like(y_ref)
        mo_ref[...] = jnp.zeros_like(mo_ref)

    @pl.when(kk % blocks_per_row == 0)
    def _():
        carry_ref[...] = jnp.zeros_like(carry_ref)

    def pre(r0):
        return jnp.dot(perm_ref[...], xg_ref[r0:r0 + LRU_TILE, LRU_WIDTH:2 * LRU_WIDTH],
                       preferred_element_type=F32)

    def emit(r0, gate):
        inner = gate * (GELU_C0 + GELU_C1 * (gate * gate))
        hsum = hf_ref[r0:r0 + LRU_TILE, :].astype(F32) + h_ref[...]
        y = (hsum * gate) * (0.5 * jnp.tanh(inner) + 0.5)
        y_ref[r0:r0 + LRU_TILE, :] = jnp.dot(unperm_ref[...], y.astype(BF16),
                                              preferred_element_type=F32).astype(BF16)

    result = []
    mlp = _mlp_stages(x_ref, y_ref, mo_ref, wo_ref, g_ref, wup_ref, wdn_ref, result, MLP_FINE)
    lru = _lru_stages(xp_ref, xg_ref.at[:, 0:LRU_WIDTH], xn_ref, t_blk == 0, t_blk == blocks_per_row - 1,
                      (perm_ref, cw_ref, cb_ref, wg_ref, ba_ref, bx_ref, lam_ref),
                      (ext_ref, a_ref, u_ref, h_ref, carry_ref), True, pre, emit)
    next(lru)
    for _ in range(D_MODEL // MLP_FINE[0]):
        next(mlp)
    done = object()
    for _ in lru:
        next(mlp, done)
    for _ in _mem_attention_stages(mq_ref, mk_ref, mv_ref, mo_ref):
        next(mlp, done)
    for _ in mlp:
        pass
    acc = result[0]
    if final:
        acc = _rms(acc, fg_ref[...])
    o_ref[...] = acc


def _lru_weight_specs():
    return [_const_spec((CONV_WIDTH, LRU_WIDTH)), _const_spec((1, LRU_WIDTH)),
            _const_spec((LRU_BLOCKS, LRU_BLOCK_DIM, 2 * LRU_BLOCK_DIM)),
            _const_spec((1, LRU_WIDTH)), _const_spec((1, LRU_WIDTH)), _const_spec((1, LRU_WIDTH))]


def _lru_scratch():
    ts = LRU_TILE
    return [pltpu.VMEM((ts // SUBLANES + CONV_WIDTH - 1, SUBLANES, LRU_WIDTH), F32),
            pltpu.VMEM((ts, LRU_WIDTH), F32),
            pltpu.VMEM((ts, LRU_WIDTH), F32),
            pltpu.VMEM((ts, LRU_WIDTH), F32),
            pltpu.VMEM((1, LRU_WIDTH), F32)]


def _chunk_permutation(ts):
    n = ts // SUBLANES
    r = np.arange(ts)
    src = (r % SUBLANES) * n + r // SUBLANES
    return jnp.asarray(src[:, None] == np.arange(ts)[None, :], BF16)


def _lru_in_forward(x3, g, w_in, perm, weights, batch, seq, blk):
    nt = seq // blk
    n_blocks = batch * nt
    d = x3.shape[-1]

    def cur_bt(k):
        kk = jnp.minimum(k, n_blocks - 1)
        return kk // nt, kk % nt

    def prev_bt(k):
        return cur_bt(jnp.maximum(k - 1, 0))

    return pl.pallas_call(
        functools.partial(_lru_in_fwd_kernel, blocks_per_row=nt),
        grid=(n_blocks + 1,),
        in_specs=[pl.BlockSpec((None, blk, d), lambda k: (*cur_bt(k), 0)),
                  _const_spec((1, d)), _const_spec((d, LRU_IN_COLS)),
                  _const_spec((LRU_TILE, LRU_TILE))]
        + _lru_weight_specs(),
        out_specs=[pl.BlockSpec((None, blk, 2 * LRU_WIDTH), lambda k: (*cur_bt(k), 0)),
                   pl.BlockSpec((None, blk, MEM_W), lambda k: (*cur_bt(k), 0)),
                   pl.BlockSpec((None, blk, LRU_WIDTH), lambda k: (*prev_bt(k), 0))],
        out_shape=[jax.ShapeDtypeStruct((batch, seq, 2 * LRU_WIDTH), BF16),
                   jax.ShapeDtypeStruct((batch, seq, MEM_W), BF16),
                   jax.ShapeDtypeStruct((batch, seq, LRU_WIDTH), BF16)],
        scratch_shapes=[pltpu.VMEM((blk, LRU_WIDTH), BF16),
                        pltpu.VMEM((BF16_ROWS, LRU_WIDTH), BF16)]
        + _lru_scratch(),
        compiler_params=_params(1),
        name="lru_in_forward",
    )(x3, g, w_in, perm, *weights)


def _lru_reverse_mlp(x3, mq3, memkv, layer, xg3, hf, perm, weights, wo, g, wup, wdn, fg, final, batch,
                     seq, blk):
    nt = seq // blk
    n_blocks = batch * nt
    per = blk // BF16_ROWS
    last_halo = seq // BF16_ROWS - 1
    d = x3.shape[-1]

    def lru_bt(k):
        kk = jnp.minimum(k, n_blocks - 1)
        return kk // nt, nt - 1 - kk % nt

    def mlp_bt(k):
        return lru_bt(jnp.maximum(k - 1, 0))

    def at_mlp(col):
        return lambda k: (*mlp_bt(k), col)

    def at_lru(col):
        return lambda k: (*lru_bt(k), col)

    def prev_idx(k):
        b, t = lru_bt(k)
        return b, jnp.maximum(t * per - 1, 0), 0

    def next_idx(k):
        b, t = lru_bt(k)
        return b, jnp.minimum((t + 1) * per, last_halo), 0

    return pl.pallas_call(
        functools.partial(_lru_rev_mlp_kernel, blocks_per_row=nt, final=final),
        grid=(n_blocks + 1,),
        in_specs=[pl.BlockSpec((None, blk, d), at_mlp(0)),
                  pl.BlockSpec((None, blk, MEM_W), at_lru(0)),
                  pl.BlockSpec((None, MEM_LEN, MEM_W), lambda k: (lru_bt(k)[0], 0, 2 * layer)),
                  pl.BlockSpec((None, MEM_LEN, MEM_W), lambda k: (lru_bt(k)[0], 0, 2 * layer + 1)),
                  pl.BlockSpec((None, BF16_ROWS, LRU_WIDTH), prev_idx),
                  pl.BlockSpec((None, blk, 2 * LRU_WIDTH), at_lru(0)),
                  pl.BlockSpec((None, BF16_ROWS, LRU_WIDTH), next_idx),
                  pl.BlockSpec((None, blk, LRU_WIDTH), at_lru(0)),
                  _const_spec((LRU_TILE, LRU_TILE)), _const_spec((LRU_TILE, LRU_TILE))]
        + _lru_weight_specs()
        + [_const_spec(wo.shape), _const_spec((1, d)), _const_spec(wup.shape), _const_spec(wdn.shape),
           _const_spec((1, d))],
        out_specs=pl.BlockSpec((None, blk, d), at_mlp(0)),
        out_shape=jax.ShapeDtypeStruct(x3.shape, F32),
        scratch_shapes=_lru_scratch() + [pltpu.VMEM((blk, LRU_WIDTH), BF16),
                                         pltpu.VMEM((blk, MEM_W), BF16)],
        compiler_params=_params(1),
        name="lru_reverse_mlp",
    )(x3, mq3, memkv, memkv, xg3, xg3, xg3, hf, perm, perm.T, *weights, wo, g, wup, wdn, fg)


def _dup_heads(w):
    d = w.shape[0]
    w4 = w.reshape(d, ATTN_KV_HEADS, 1, ATTN_HEAD_DIM)
    return jnp.broadcast_to(w4, (d, ATTN_KV_HEADS, 2, ATTN_HEAD_DIM)).reshape(d, KV_DUP_W)


def _rotary_tables(tile):
    half = ROPE_DIM // 2
    inv_freq = ROPE_THETA ** (-2.0 * jnp.arange(half, dtype=F32) / ROPE_DIM)
    freq = jnp.broadcast_to(inv_freq[:, None], (half, tile))
    d = np.arange(LANES) % ATTN_HEAD_DIM
    first = d < half
    second = (d >= half) & (d < ROPE_DIM)
    pick = np.arange(half)[:, None] == (d % half)[None, :]
    zeros8 = np.zeros((half, LANES), np.float32)
    cos_rows = np.where(pick & (d < ROPE_DIM)[None, :], 1.0, 0.0)
    lo_rows = np.where(pick & first[None, :], -1.0, 0.0)
    hi_rows = np.where(pick & second[None, :], 1.0, 0.0)
    ones_rows = np.zeros((BF16_ROWS, LANES), np.float32)
    ones_rows[0] = np.where(d < ROPE_DIM, 0.0, 1.0)
    zeros16 = np.zeros((BF16_ROWS, LANES), np.float32)
    cos_cols = np.concatenate([cos_rows, cos_rows, zeros8, zeros8, ones_rows], axis=0)
    lo_cols = np.concatenate([zeros8, zeros8, lo_rows, lo_rows, zeros16], axis=0)
    hi_cols = np.concatenate([zeros8, zeros8, hi_rows, hi_rows, zeros16], axis=0)
    return freq, jnp.asarray(np.concatenate([cos_cols, lo_cols, hi_cols], axis=1), BF16)


def _gate_weights(wa, wx):
    return (0.5 * jnp.concatenate([wa, wx], axis=-1)).astype(BF16)


def kernel(x, mem, positions, mix_norm, mlp_norm, mem_norm, final_norm, w_mem_kv, w_out, w_up, w_down,
           attn_w_in, attn_sinks, lru_w_in, lru_conv_w, lru_conv_b, lru_wa, lru_ba, lru_wx, lru_bx,
           lru_lambda):
    batch, seq, d = x.shape
    t = batch * seq
    depth = mix_norm.shape[0]
    xf = x.reshape(t, d)
    ones_d = jnp.ones((1, d), F32)

    w_kv_all = jnp.concatenate([w_mem_kv[l] for l in range(depth)], axis=1).astype(BF16)
    memkv = _norm_proj(mem.reshape(batch * MEM_LEN, d), mem_norm.reshape(1, d), w_kv_all,
                       tile=batch * MEM_LEN // 2)
    memkv = memkv.reshape(batch, MEM_LEN, depth * 2 * MEM_W)

    pos_rows = positions.reshape(t // PROJ_TILE, 1, PROJ_TILE).astype(jnp.int32)
    rot_freq, rot_expand = _rotary_tables(PROJ_TILE)

    for l in range(depth):
        kind, j = l % 2, l // 2
        g_mix = mix_norm[l].reshape(1, d)
        final = l == depth - 1
        mlp_weights = (w_out[l].astype(BF16), mlp_norm[l].reshape(1, d), w_up[l].astype(BF16),
                       w_down[l].astype(BF16), final_norm.reshape(1, d) if final else ones_d)
        if kind == 0:
            w = attn_w_in[j]
            w_in = jnp.concatenate([w[:, :Q_W], _dup_heads(w[:, Q_W:Q_W + KV_W]),
                                    _dup_heads(w[:, Q_W + KV_W:Q_W + 2 * KV_W]),
                                    w[:, Q_W + 2 * KV_W:]], axis=1).astype(BF16)
            p = _attn_in(xf, g_mix, pos_rows, rot_freq, rot_expand, w_in, PROJ_TILE)
            xf = _attention_mlp(xf.reshape(batch, seq, d), p, memkv, attn_sinks[j].astype(F32), l,
                                *mlp_weights, final, batch, seq, ATTN_TILE).reshape(t, d)
        else:
            x3 = xf.reshape(batch, seq, d)
            cw, cb = lru_conv_w[j], lru_conv_b[j].reshape(1, LRU_WIDTH)

            def dir_weights(dirn):
                return (cw, cb, _gate_weights(lru_wa[j, dirn], lru_wx[j, dirn]),
                        0.5 * lru_ba[j, dirn].reshape(1, LRU_WIDTH),
                        0.5 * lru_bx[j, dirn].reshape(1, LRU_WIDTH),
                        lru_lambda[j, dirn].reshape(1, LRU_WIDTH))

            perm = _chunk_permutation(LRU_TILE)
            xg, mq, hf = _lru_in_forward(x3, g_mix, lru_w_in[j].astype(BF16), perm,
                                         dir_weights(0), batch, seq, ROW_TILE)
            xf = _lru_reverse_mlp(x3, mq, memkv, l, xg, hf, perm, dir_weights(1),
                                  *mlp_weights, final, batch, seq, ROW_TILE).reshape(t, d)
    return xf.reshape(batch, seq, d)
```
